```python
import math
import jax, jax.numpy as jnp
from jax import lax
import numpy as np

D_MODEL = 1024
BATCH = 16
SEQ = 2048
DEPTH = 2

N_MIXERS = 2
N_A_LAYERS = (DEPTH + 1) // 2
N_B_LAYERS = DEPTH // 2
EPS = 1e-6

A_HEADS = D_MODEL // 128
A_DK = 128
A_DV = 256
A_QK = A_HEADS * A_DK
A_VW = A_HEADS * A_DV
A_CONV = 4
A_CHUNK = 64
A_IN = 2 * A_QK + 2 * A_VW + 2 * A_HEADS

B_WINDOWS = (128, 512, 2048)
B_DILATIONS = (1, 4, 16)
B_GROUPS = len(B_WINDOWS)
B_HEADS = D_MODEL // 128
B_DH = 128
B_W = B_HEADS * B_DH
B_BLOCK = 128
B_IN = 3 * B_GROUPS * B_W + B_W
ROPE_THETA = 500000.0
ROPE_DIMS = B_DH // 4

kernel_name = "hybrid_gdn_dilated_swa"


def rms_norm(x, g):
    xf = x.astype(jnp.float32)
    y = xf * lax.rsqrt(jnp.mean(xf * xf, axis=-1, keepdims=True) + EPS)
    return (y * g.astype(jnp.float32)).astype(x.dtype)


def l2_norm(x):
    xf = x.astype(jnp.float32)
    return xf * lax.rsqrt(jnp.sum(xf * xf, axis=-1, keepdims=True) + EPS)


def causal_dwconv(x, w):
    K = w.shape[0]
    return lax.conv_general_dilated(
        x, w[:, None, :].astype(x.dtype), window_strides=(1,), padding=((K - 1, 0),),
        dimension_numbers=("NWC", "WIO", "NWC"), feature_group_count=x.shape[-1])


def gated_delta_rule(q, k, v, g, beta):
    Bn, H, S, DK = q.shape
    DV = v.shape[-1]
    C = A_CHUNK
    N = S // C
    f32 = jnp.float32
    q, k, v, g, beta = (t.astype(f32) for t in (q, k, v, g, beta))
    q = q * (DK ** -0.5)

    def chunk(t):
        return t.reshape((Bn, H, N, C) + t.shape[3:])

    q, k, v, g, beta = map(chunk, (q, k, v, g, beta))
    g = jnp.cumsum(g, axis=-1)
    idx = jnp.arange(C)
    causal = idx[:, None] >= idx[None, :]
    strict = idx[:, None] > idx[None, :]
    decay = jnp.exp(jnp.where(causal, g[..., :, None] - g[..., None, :], -jnp.inf))
    k_beta = k * beta[..., None]
    lower = jnp.where(strict, jnp.einsum("bhnid,bhnjd->bhnij", k_beta, k) * decay, 0.0)
    rhs = jnp.concatenate([v * beta[..., None], k_beta * jnp.exp(g)[..., None]], axis=-1)
    sol = lax.linalg.triangular_solve(lower + jnp.eye(C, dtype=f32), rhs,
                                      left_side=True, lower=True, unit_diagonal=True)
    u, w = sol[..., :DV], sol[..., DV:]
    qk = jnp.einsum("bhnid,bhnjd->bhnij", q, k) * decay

    def step(state, inp):
        q_c, k_c, u_c, w_c, g_c, qk_c = inp
        v_new = u_c - jnp.einsum("bhck,bhkv->bhcv", w_c, state)
        o = (jnp.einsum("bhck,bhkv->bhcv", q_c * jnp.exp(g_c)[..., None], state)
             + jnp.einsum("bhij,bhjv->bhiv", qk_c, v_new))
        g_last = g_c[..., -1:]
        state = (state * jnp.exp(g_last)[..., None]
                 + jnp.einsum("bhck,bhcv->bhkv", k_c * jnp.exp(g_last - g_c)[..., None], v_new))
        return state, o

    xs = tuple(jnp.moveaxis(t, 2, 0) for t in (q, k, u, w, g, qk))
    state0 = jnp.zeros((Bn, H, DK, DV), f32)
    _, o = lax.scan(step, state0, xs)
    return jnp.moveaxis(o, 0, 2).reshape(Bn, H, S, DV)


def gated_deltanet_mixer(h, w_in, conv_w, a_log, dt_bias, out_norm_g, w_out):
    Bn, S, _ = h.shape
    proj = h @ w_in
    n_qkv = 2 * A_QK + A_VW
    qkv = proj[..., :n_qkv]
    z = proj[..., n_qkv:n_qkv + A_VW]
    b_logit = proj[..., n_qkv + A_VW:n_qkv + A_VW + A_HEADS]
    a_logit = proj[..., n_qkv + A_VW + A_HEADS:]
    qkv = jax.nn.silu(causal_dwconv(qkv, conv_w))
    q = qkv[..., :A_QK].reshape(Bn, S, A_HEADS, A_DK).transpose(0, 2, 1, 3)
    k = qkv[..., A_QK:2 * A_QK].reshape(Bn, S, A_HEADS, A_DK).transpose(0, 2, 1, 3)
    v = qkv[..., 2 * A_QK:].reshape(Bn, S, A_HEADS, A_DV).transpose(0, 2, 1, 3)
    q, k = l2_norm(q), l2_norm(k)
    beta = jax.nn.sigmoid(b_logit.astype(jnp.float32)).transpose(0, 2, 1)
    g = (-jnp.exp(a_log.astype(jnp.float32))
         * jax.nn.softplus(a_logit.astype(jnp.float32) + dt_bias.astype(jnp.float32))).transpose(0, 2, 1)
    o = gated_delta_rule(q, k, v, g, beta).transpose(0, 2, 1, 3)
    o = rms_norm(o, out_norm_g).astype(h.dtype)
    o = o * jax.nn.silu(z.reshape(Bn, S, A_HEADS, A_DV))
    return o.reshape(Bn, S, A_VW) @ w_out


def partial_rope(x, cos, sin):
    half = ROPE_DIMS // 2
    x1, x2, xp = x[..., :half], x[..., half:ROPE_DIMS], x[..., ROPE_DIMS:]
    xf1, xf2 = x1.astype(jnp.float32), x2.astype(jnp.float32)
    r1 = (xf1 * cos - xf2 * sin).astype(x.dtype)
    r2 = (xf2 * cos + xf1 * sin).astype(x.dtype)
    return jnp.concatenate([r1, r2, xp], axis=-1)


def dilated_window_attention(q, k, v, dilation, span):
    Bn, S, H, DH = q.shape
    L = S // dilation
    nb = -(-L // B_BLOCK)
    Lp = nb * B_BLOCK

    def to_sub(t):
        t = t.reshape(Bn, L, dilation, H, DH).transpose(0, 2, 3, 1, 4)
        return jnp.pad(t, ((0, 0), (0, 0), (0, 0), (0, Lp - L), (0, 0)))

    qs, ks, vs = map(to_sub, (q, k, v))
    qb = qs.reshape(Bn, dilation, H, nb, B_BLOCK, DH)

    def band(t):
        tp = jnp.pad(t, ((0, 0), (0, 0), (0, 0), (B_BLOCK, 0), (0, 0)))
        prev = tp[..., :Lp, :].reshape(Bn, dilation, H, nb, B_BLOCK, DH)
        cur = tp[..., B_BLOCK:, :].reshape(Bn, dilation, H, nb, B_BLOCK, DH)
        return jnp.concatenate([prev, cur], axis=-2)

    kb, vb = band(ks), band(vs)
    qi = jnp.arange(B_BLOCK)[:, None]
    kj = jnp.arange(2 * B_BLOCK)[None, :]
    dist = qi + B_BLOCK - kj
    key_pos = jnp.arange(nb)[:, None, None] * B_BLOCK + kj[None] - B_BLOCK
    mask = (dist >= 0) & (dist <= span) & (key_pos >= 0)
    s = jnp.einsum("bdhnqe,bdhnke->bdhnqk", qb, kb,
                   preferred_element_type=jnp.float32) * (DH ** -0.5)
    s = jnp.where(mask, s, -jnp.inf)
    lse = jax.nn.logsumexp(s, axis=-1)
    p = jnp.exp(s - lse[..., None])
    o = jnp.einsum("bdhnqk,bdhnke->bdhnqe", p.astype(v.dtype), vb,
                   preferred_element_type=jnp.float32)

    def from_sub(t):
        t = t.reshape((Bn, dilation, H, Lp) + t.shape[5:])[:, :, :, :L]
        t = jnp.moveaxis(t, 3, 1)
        return t.reshape((Bn, S, H) + t.shape[4:])

    return from_sub(o), from_sub(lse)


def dilated_attention_mixer(h, positions, w_in, q_norm_g, k_norm_g, w_out):
    Bn, S, _ = h.shape
    proj = h @ w_in
    n_qkv = 3 * B_GROUPS * B_W
    qkv = proj[..., :n_qkv].reshape(Bn, S, 3, B_GROUPS, B_HEADS, B_DH)
    z = proj[..., n_qkv:]
    q, k, v = qkv[:, :, 0], qkv[:, :, 1], qkv[:, :, 2]
    q = rms_norm(q, q_norm_g[:, None, :])
    k = rms_norm(k, k_norm_g[:, None, :])
    inv_freq = ROPE_THETA ** (-jnp.arange(0, ROPE_DIMS, 2, dtype=jnp.float32) / ROPE_DIMS)
    ang = positions.astype(jnp.float32)[..., None] * inv_freq
    cos = jnp.cos(ang)[:, :, None, None, :]
    sin = jnp.sin(ang)[:, :, None, None, :]
    q, k = partial_rope(q, cos, sin), partial_rope(k, cos, sin)
    outs, lses = [], []
    for gi in range(B_GROUPS):
        o_g, lse_g = dilated_window_attention(q[:, :, gi], k[:, :, gi], v[:, :, gi],
                                              B_DILATIONS[gi], B_WINDOWS[gi] // B_DILATIONS[gi])
        outs.append(o_g)
        lses.append(lse_g)
    wts = jax.nn.softmax(jnp.stack(lses, axis=0), axis=0)
    o = jnp.sum(wts[..., None] * jnp.stack(outs, axis=0), axis=0).astype(h.dtype)
    o = o.reshape(Bn, S, B_W) * jax.nn.silu(z)
    return o @ w_out


def setup_inputs(seed: int = 0) -> dict:
    key = jax.random.key(seed)
    ks = jax.random.split(key, 16)
    f32 = jnp.float32
    x = jax.random.normal(ks[0], (BATCH, SEQ, D_MODEL), f32)
    positions = jnp.broadcast_to(jnp.arange(SEQ, dtype=jnp.int32)[None, :], (BATCH, SEQ))
    norm_g = 1.0 + 0.02 * jax.random.normal(ks[1], (DEPTH, D_MODEL), f32)
    a_w_in = jax.random.normal(ks[2], (N_A_LAYERS, D_MODEL, A_IN), f32) * D_MODEL ** -0.5
    a_conv_w = jax.random.normal(ks[3], (N_A_LAYERS, A_CONV, 2 * A_QK + A_VW), f32) * A_CONV ** -0.5
    a_log = jnp.log(jax.random.uniform(ks[4], (N_A_LAYERS, A_HEADS), f32, 1.0, 16.0))
    dt = jnp.exp(jax.random.uniform(ks[5], (N_A_LAYERS, A_HEADS), f32,
                                    math.log(1e-3), math.log(1e-1)))
    a_dt_bias = dt + jnp.log(-jnp.expm1(-dt))
    a_norm_g = 1.0 + 0.02 * jax.random.normal(ks[6], (N_A_LAYERS, A_DV), f32)
    a_w_out = jax.random.normal(ks[7], (N_A_LAYERS, A_VW, D_MODEL), f32) * A_VW ** -0.5
    b_w_in = jax.random.normal(ks[8], (N_B_LAYERS, D_MODEL, B_IN), f32) * D_MODEL ** -0.5
    b_q_norm_g = 1.0 + 0.02 * jax.random.normal(ks[9], (N_B_LAYERS, B_GROUPS, B_DH), f32)
    b_k_norm_g = 1.0 + 0.02 * jax.random.normal(ks[10], (N_B_LAYERS, B_GROUPS, B_DH), f32)
    b_w_out = jax.random.normal(ks[11], (N_B_LAYERS, B_W, D_MODEL), f32) * B_W ** -0.5
    return {"x": x, "positions": positions, "norm_g": norm_g,
            "a_w_in": a_w_in, "a_conv_w": a_conv_w, "a_log": a_log, "a_dt_bias": a_dt_bias,
            "a_norm_g": a_norm_g, "a_w_out": a_w_out,
            "b_w_in": b_w_in, "b_q_norm_g": b_q_norm_g, "b_k_norm_g": b_k_norm_g,
            "b_w_out": b_w_out}


def reference(x, positions, norm_g, a_w_in, a_conv_w, a_log, a_dt_bias, a_norm_g, a_w_out,
              b_w_in, b_q_norm_g, b_k_norm_g, b_w_out):
    for i in range(DEPTH):
        h = rms_norm(x, norm_g[i])
        j = i // N_MIXERS
        if i % N_MIXERS == 0:
            y = gated_deltanet_mixer(h, a_w_in[j], a_conv_w[j], a_log[j], a_dt_bias[j],
                                     a_norm_g[j], a_w_out[j])
        else:
            y = dilated_attention_mixer(h, positions, b_w_in[j], b_q_norm_g[j],
                                        b_k_norm_g[j], b_w_out[j])
        x = x + y.astype(x.dtype)
    return x
```

```python
import functools

import jax
import jax.numpy as jnp
from jax import lax
from jax.experimental import pallas as pl
from jax.experimental.pallas import tpu as pltpu

F32 = jnp.float32
BF16 = jnp.bfloat16
EPS = 1e-6

D_MODEL = 1024
A_HEADS = 8
A_DK = 128
A_DV = 256
A_QK = A_HEADS * A_DK
A_VW = A_HEADS * A_DV
A_MAIN = 2 * A_QK + 2 * A_VW
CHUNK = 128
GROUP = 4
B_WINDOWS = (128, 512, 2048)
B_DILATIONS = (1, 4, 16)
B_GROUPS = 3
B_HEADS = 8
B_DH = 128
B_W = B_HEADS * B_DH
B_BLOCK = 128
ROPE_THETA = 500000.0
ROPE_DIMS = B_DH // 4
ROPE_HALF = ROPE_DIMS // 2

VMEM_LIMIT_BYTES = 56 * 1024 * 1024
ROW_TILE = 512


def _split2(a):
    hi = a.astype(BF16)
    lo = (a - hi.astype(F32)).astype(BF16)
    return hi, lo


def _split3(a):
    hi = a.astype(BF16)
    r = a - hi.astype(F32)
    mid = r.astype(BF16)
    lo = (r - mid.astype(F32)).astype(BF16)
    return hi, mid, lo


def _bdot(a, b):
    return jnp.einsum("cik,ckj->cij", a, b, preferred_element_type=F32)


def _bdot3(a, b):
    ah, al = _split2(a)
    bh, bl = _split2(b)
    return _bdot(ah, bh) + _bdot(ah, bl) + _bdot(al, bh)


def _normed(x_ref, g_ref):
    x = x_ref[...]
    ms = jnp.mean(x * x, axis=-1, keepdims=True)
    return (x * lax.rsqrt(ms + EPS) * g_ref[...]).astype(BF16)


def _proj_cols(hn, w_ref, o_ref, tn):
    n_total = o_ref.shape[1]
    for n0 in range(0, n_total, tn):
        o_ref[:, n0:n0 + tn] = jnp.dot(
            hn, w_ref[:, n0:n0 + tn], preferred_element_type=F32).astype(o_ref.dtype)


def _norm_proj_gdn_kernel(x_ref, g_ref, w_ref, wg_ref, alog_ref, dtb_ref, o_ref, gates_ref):
    hn = _normed(x_ref, g_ref)
    _proj_cols(hn, w_ref, o_ref, 512)
    logits = jnp.dot(hn, wg_ref[...], preferred_element_type=F32)
    lane = lax.broadcasted_iota(jnp.int32, logits.shape, 1)
    xs = logits + dtb_ref[...]
    softplus = jnp.maximum(xs, 0.0) + jnp.log1p(jnp.exp(-jnp.abs(xs)))
    decay = -jnp.exp(alog_ref[...]) * softplus
    gates_ref[...] = jnp.where(lane < A_HEADS, jax.nn.sigmoid(logits), decay)


def _norm_proj_kernel(x_ref, g_ref, w_ref, o_ref):
    hn = _normed(x_ref, g_ref)
    _proj_cols(hn, w_ref, o_ref, 512)


def _resident(shape):
    return pl.BlockSpec(shape, lambda i: (0,) * len(shape), pipeline_mode=pl.Buffered(1))


def _norm_proj_gdn(x2, g, w, wg, alog16, dtb16):
    t, d = x2.shape
    n = w.shape[1]
    tm = ROW_TILE
    return pl.pallas_call(
        _norm_proj_gdn_kernel,
        grid=(t // tm,),
        in_specs=[
            pl.BlockSpec((tm, d), lambda i: (i, 0)),
            _resident((1, d)),
            _resident((d, n)),
            _resident((d, 2 * A_HEADS)),
            _resident((1, 2 * A_HEADS)),
            _resident((1, 2 * A_HEADS)),
        ],
        out_specs=[
            pl.BlockSpec((tm, n), lambda i: (i, 0)),
            pl.BlockSpec((tm, 2 * A_HEADS), lambda i: (i, 0)),
        ],
        out_shape=[
            jax.ShapeDtypeStruct((t, n), BF16),
            jax.ShapeDtypeStruct((t, 2 * A_HEADS), F32),
        ],
        compiler_params=pltpu.CompilerParams(
            dimension_semantics=("parallel",), vmem_limit_bytes=VMEM_LIMIT_BYTES),
        name="gdn_norm_proj",
    )(x2, g, w, wg, alog16, dtb16)


def _norm_proj(x2, g, w):
    t, d = x2.shape
    n = w.shape[1]
    tm = ROW_TILE // 2
    return pl.pallas_call(
        _norm_proj_kernel,
        grid=(t // tm,),
        in_specs=[
            pl.BlockSpec((tm, d), lambda i: (i, 0)),
            _resident((1, d)),
            _resident((d, n)),
        ],
        out_specs=pl.BlockSpec((tm, n), lambda i: (i, 0)),
        out_shape=jax.ShapeDtypeStruct((t, n), BF16),
        compiler_params=pltpu.CompilerParams(
            dimension_semantics=("parallel",), vmem_limit_bytes=VMEM_LIMIT_BYTES),
        name="attn_norm_proj",
    )(x2, g, w)


def _out_proj_kernel(o_ref, w_ref, x_ref, y_ref):
    y_ref[...] = x_ref[...] + jnp.dot(o_ref[...], w_ref[...], preferred_element_type=F32)


def _out_proj(o2, w, x2, name):
    t, k = o2.shape
    d = w.shape[1]
    tm = ROW_TILE
    return pl.pallas_call(
        _out_proj_kernel,
        grid=(t // tm,),
        in_specs=[
            pl.BlockSpec((tm, k), lambda i: (i, 0)),
            _resident((k, d)),
            pl.BlockSpec((tm, d), lambda i: (i, 0)),
        ],
        out_specs=pl.BlockSpec((tm, d), lambda i: (i, 0)),
        out_shape=jax.ShapeDtypeStruct((t, d), F32),
        compiler_params=pltpu.CompilerParams(
            dimension_semantics=("parallel",), vmem_limit_bytes=VMEM_LIMIT_BYTES),
        name=name,
    )(o2, w, x2)


def _gdn_kernel(q_ref, k_ref, v_ref, z_ref, gates_ref, cwq_ref, cwk_ref, cwv_ref, ng_ref,
                o_ref, s_ref):
    head = pl.program_id(1)
    rows = GROUP * CHUNK
    n_groups = q_ref.shape[0] // rows
    c = CHUNK

    ii = lax.broadcasted_iota(jnp.int32, (c, c), 0)
    jj = lax.broadcasted_iota(jnp.int32, (c, c), 1)
    eye = jnp.where(ii == jj, 1.0, 0.0).astype(F32)
    lower_incl = ii >= jj
    lower_strict = ii > jj
    cum_lhs = jnp.broadcast_to(jnp.where(lower_incl, 1.0, 0.0).astype(BF16)[None], (GROUP, c, c))

    def same_block(size):
        return (ii // size) == (jj // size)

    sel_row = lax.broadcasted_iota(jnp.int32, (2 * A_HEADS, 128), 0)
    sel_beta = jnp.where(sel_row == head, 1.0, 0.0).astype(BF16)
    sel_decay = jnp.where(sel_row == head + A_HEADS, 1.0, 0.0).astype(BF16)

    s_ref[...] = jnp.zeros_like(s_ref)

    def group_body(g, carry):
        r0 = pl.multiple_of(g * rows, rows)
        h0 = pl.multiple_of(jnp.maximum(r0 - 16, 0), 16)

        def conv_silu(ref, cw_ref):
            main = ref[pl.ds(r0, rows), :].astype(F32)
            halo = ref[pl.ds(h0, 16), :].astype(F32)
            halo = jnp.where(g > 0, halo, 0.0)
            xcat = jnp.concatenate([halo, main], axis=0)
            w = cw_ref[...]
            acc = xcat * w[3:4, :]
            for tap in range(3):
                acc = acc + pltpu.roll(xcat, 3 - tap, 0) * w[tap:tap + 1, :]
            acc = acc[16:, :]
            return acc * jax.nn.sigmoid(acc)

        def l2n(x):
            return x * lax.rsqrt(jnp.sum(x * x, axis=-1, keepdims=True) + EPS)

        q = l2n(conv_silu(q_ref, cwq_ref)) * (A_DK ** -0.5)
        k = l2n(conv_silu(k_ref, cwk_ref))
        v = conv_silu(v_ref, cwv_ref)

        gh, gm, gl = _split3(gates_ref[pl.ds(r0, rows), :])

        def select(sel):
            return (jnp.dot(gh, sel, preferred_element_type=F32)
                    + jnp.dot(gm, sel, preferred_element_type=F32)
                    + jnp.dot(gl, sel, preferred_element_type=F32))

        beta = select(sel_beta).reshape(GROUP, c, 128)
        gstep = select(sel_decay).reshape(GROUP, c, 128)
        sh, sm, sl = _split3(gstep)
        gc = _bdot(cum_lhs, sh) + _bdot(cum_lhs, sm) + _bdot(cum_lhs, sl)
        gc_row = jnp.swapaxes(gc, 1, 2)
        decay = jnp.exp(jnp.where(lower_incl[None], gc - gc_row, -jnp.inf))
        gc_last = gc[:, c - 1:c, :]
        exp_gc = jnp.exp(gc)

        q3 = q.reshape(GROUP, c, A_DK)
        k3 = k.reshape(GROUP, c, A_DK)
        v3 = v.reshape(GROUP, c, A_DV)
        kb3 = k3 * beta
        k3b = k3.astype(BF16)
        kk = jnp.einsum("cid,cjd->cij", kb3.astype(BF16), k3b, preferred_element_type=F32)
        qk = jnp.einsum("cid,cjd->cij", q3.astype(BF16), k3b, preferred_element_type=F32) * decay
        low = jnp.where(lower_strict[None], kk * decay, 0.0)

        m1 = jnp.where(same_block(8)[None], -low, 0.0)
        tinv = eye[None] + m1
        m2 = _bdot3(m1, m1)
        tinv = tinv + _bdot3(tinv, m2)
        m4 = _bdot3(m2, m2)
        tinv = tinv + _bdot3(tinv, m4)
        size = 8
        while size < c:
            off = jnp.where((same_block(2 * size) & jnp.logical_not(same_block(size)))[None], low, 0.0)
            tinv = tinv - _bdot3(_bdot3(tinv, off), tinv)
            size *= 2

        th, tl = _split2(tinv)
        vb = (v3 * jnp.concatenate([beta, beta], axis=-1)).astype(BF16)
        kbg = (kb3 * exp_gc).astype(BF16)
        u = _bdot(th, vb) + _bdot(tl, vb)
        w = _bdot(th, kbg) + _bdot(tl, kbg)

        qt = q3 * exp_gc
        kd = (k3 * jnp.exp(gc_last - gc)).astype(BF16)
        e_last = jnp.exp(gc_last)
        e_last = jnp.concatenate([e_last, e_last], axis=-1)
        ub = u.astype(BF16)
        wb = w.astype(BF16)
        qkb = qk.astype(BF16)
        kdt = jnp.swapaxes(kd, 1, 2)
        a_neg = -_bdot(kdt, wb)
        b_add = _bdot(kdt, ub)
        q_eff = qt - _bdot(qkb, wb)
        o_loc = _bdot(qkb, ub)
        p_all = jnp.concatenate([a_neg, q_eff], axis=1).astype(BF16)

        ng = ng_ref[...]
        for cc in range(GROUP):
            state = s_ref[...]
            r = jnp.dot(p_all[cc], state.astype(BF16), preferred_element_type=F32)
            o = r[A_DK:, :] + o_loc[cc]
            s_ref[...] = state * e_last[cc] + r[:A_DK, :] + b_add[cc]
            on = o * lax.rsqrt(jnp.mean(o * o, axis=-1, keepdims=True) + EPS) * ng
            row = pl.multiple_of(r0 + cc * c, c)
            zc = z_ref[pl.ds(row, c), :].astype(F32)
            o_ref[pl.ds(row, c), :] = (on * (zc * jax.nn.sigmoid(zc))).astype(o_ref.dtype)
        return carry

    lax.fori_loop(0, n_groups, group_body, 0)


def _gdn_core(proj3, gates3, conv_w, norm_g):
    bsz, seq, _ = proj3.shape
    qb = A_QK // A_DK
    vb = 2 * A_QK // A_DV
    zb = vb + A_VW // A_DV
    return pl.pallas_call(
        _gdn_kernel,
        grid=(bsz, A_HEADS),
        in_specs=[
            pl.BlockSpec((None, seq, A_DK), lambda b, h: (b, 0, h)),
            pl.BlockSpec((None, seq, A_DK), lambda b, h: (b, 0, qb + h)),
            pl.BlockSpec((None, seq, A_DV), lambda b, h: (b, 0, vb + h)),
            pl.BlockSpec((None, seq, A_DV), lambda b, h: (b, 0, zb + h)),
            pl.BlockSpec((None, seq, 2 * A_HEADS), lambda b, h: (b, 0, 0)),
            pl.BlockSpec((4, A_DK), lambda b, h: (0, h)),
            pl.BlockSpec((4, A_DK), lambda b, h: (0, qb + h)),
            pl.BlockSpec((4, A_DV), lambda b, h: (0, vb + h)),
            pl.BlockSpec((1, A_DV), lambda b, h: (0, 0)),
        ],
        out_specs=pl.BlockSpec((None, seq, A_DV), lambda b, h: (b, 0, h)),
        out_shape=jax.ShapeDtypeStruct((bsz, seq, A_VW), BF16),
        scratch_shapes=[pltpu.VMEM((A_DK, A_DV), F32)],
        compiler_params=pltpu.CompilerParams(
            dimension_semantics=("parallel", "parallel"), vmem_limit_bytes=VMEM_LIMIT_BYTES),
        name="gdn_core",
    )(proj3, proj3, proj3, proj3, gates3, conv_w, conv_w, conv_w, norm_g)


def _attn_kernel(pos_ref, invf_ref, qg_ref, kg_ref,
                 q0_ref, k0_ref, v0_ref, q1_ref, k1_ref, v1_ref, q2_ref, k2_ref, v2_ref, z_ref,
                 o_ref, cos_ref, sin_ref, qs_ref, ks_ref, vs_ref, og_ref, lg_ref):
    seq = q0_ref.shape[0]
    blk = B_BLOCK
    tile = 512
    n_tiles = seq // tile

    @pl.when(pl.program_id(1) == 0)
    def _():
        ang = pos_ref[...].astype(F32) * invf_ref[...]
        lane = lax.broadcasted_iota(jnp.int32, ang.shape, 1)
        cos_ref[...] = jnp.cos(ang)
        sin_ref[...] = jnp.where(lane < ROPE_HALF, -1.0, 1.0) * jnp.sin(ang)

    qi = lax.broadcasted_iota(jnp.int32, (blk, blk), 0)
    kj = lax.broadcasted_iota(jnp.int32, (blk, blk), 1)
    cur_mask = kj <= qi
    prev_mask = kj >= qi
    span_ok = all(w // d == blk for w, d in zip(B_WINDOWS, B_DILATIONS))
    assert span_ok

    q_refs = (q0_ref, q1_ref, q2_ref)
    k_refs = (k0_ref, k1_ref, k2_ref)
    v_refs = (v0_ref, v1_ref, v2_ref)

    for gi in range(B_GROUPS):
        dil = B_DILATIONS[gi]
        length = seq // dil
        n_blocks = length // blk

        def prep(t, carry, gi=gi):
            r0 = pl.multiple_of(t * tile, tile)
            cos = cos_ref[pl.ds(r0, tile), :]
            sin = sin_ref[pl.ds(r0, tile), :]
            lane = lax.broadcasted_iota(jnp.int32, cos.shape, 1)

            def norm_rope(ref, gain):
                x = ref[pl.ds(r0, tile), :].astype(F32)
                y = x * lax.rsqrt(jnp.mean(x * x, axis=-1, keepdims=True) + EPS) * gain
                partner = jnp.where(lane < ROPE_HALF,
                                    pltpu.roll(y, B_DH - ROPE_HALF, 1), pltpu.roll(y, ROPE_HALF, 1))
                return y * cos + partner * sin

            qs_ref[pl.ds(r0, tile), :] = norm_rope(q_refs[gi], qg_ref[gi:gi + 1, :]) * (B_DH ** -0.5)
            ks_ref[pl.ds(r0, tile), :] = norm_rope(k_refs[gi], kg_ref[gi:gi + 1, :])
            vs_ref[pl.ds(r0, tile), :] = v_refs[gi][pl.ds(r0, tile), :].astype(F32)
            return carry

        lax.fori_loop(0, n_tiles, prep, 0)

        def rows_of(base, dil=dil):
            if dil == 1:
                return pl.ds(base, blk)
            return pl.ds(base, blk, stride=dil)

        def scores(qb, ref, base):
            kb = ref[rows_of(base), :].astype(BF16)
            return jnp.einsum("qd,kd->qk", qb, kb, preferred_element_type=F32)

        def attend(base, with_prev, gi=gi, dil=dil):
            qb = qs_ref[rows_of(base), :].astype(BF16)
            s_cur = jnp.where(cur_mask, scores(qb, ks_ref, base), -jnp.inf)
            m = jnp.max(s_cur, axis=-1, keepdims=True)
            if with_prev:
                pbase = base - blk * dil
                s_prev = jnp.where(prev_mask, scores(qb, ks_ref, pbase), -jnp.inf)
                m = jnp.maximum(m, jnp.max(s_prev, axis=-1, keepdims=True))
            p_cur = jnp.exp(s_cur - m)
            denom = jnp.sum(p_cur, axis=-1, keepdims=True)
            acc = jnp.dot(p_cur.astype(BF16), vs_ref[rows_of(base), :].astype(BF16),
                          preferred_element_type=F32)
            if with_prev:
                p_prev = jnp.exp(s_prev - m)
                denom = denom + jnp.sum(p_prev, axis=-1, keepdims=True)
                acc = acc + jnp.dot(p_prev.astype(BF16), vs_ref[rows_of(pbase), :].astype(BF16),
                                    preferred_element_type=F32)
            og_ref[gi, rows_of(base), :] = acc / denom
            lg_ref[gi, rows_of(base), :] = jnp.broadcast_to(m + jnp.log(denom), (blk, B_DH))

        def stream(r, carry, n_blocks=n_blocks, dil=dil):
            attend(r, False)

            def block(n, c2):
                attend(r + n * (blk * dil), True)
                return c2

            if n_blocks > 1:
                lax.fori_loop(1, n_blocks, block, 0)
            return carry

        lax.fori_loop(0, dil, stream, 0)

    def merge(t, carry):
        r0 = pl.multiple_of(t * tile, tile)
        l0 = lg_ref[0, pl.ds(r0, tile), :]
        l1 = lg_ref[1, pl.ds(r0, tile), :]
        l2 = lg_ref[2, pl.ds(r0, tile), :]
        m = jnp.maximum(jnp.maximum(l0, l1), l2)
        w0 = jnp.exp(l0 - m)
        w1 = jnp.exp(l1 - m)
        w2 = jnp.exp(l2 - m)
        num = (w0 * og_ref[0, pl.ds(r0, tile), :] + w1 * og_ref[1, pl.ds(r0, tile), :]
               + w2 * og_ref[2, pl.ds(r0, tile), :])
        o = num / (w0 + w1 + w2)
        z = z_ref[pl.ds(r0, tile), :].astype(F32)
        o_ref[pl.ds(r0, tile), :] = (o * (z * jax.nn.sigmoid(z))).astype(o_ref.dtype)
        return carry

    lax.fori_loop(0, n_tiles, merge, 0)


def _attn_core(proj3, pos3, inv_freq, q_gain, k_gain):
    bsz, seq, _ = proj3.shape

    def col(which, gi):
        first = (which * B_GROUPS + gi) * B_HEADS
        return pl.BlockSpec((None, seq, B_DH), lambda b, h: (b, 0, first + h))

    qkv_specs = []
    for gi in range(B_GROUPS):
        qkv_specs += [col(0, gi), col(1, gi), col(2, gi)]
    z_first = 3 * B_GROUPS * B_HEADS
    return pl.pallas_call(
        _attn_kernel,
        grid=(bsz, B_HEADS),
        in_specs=[
            pl.BlockSpec((None, seq, 1), lambda b, h: (b, 0, 0)),
            pl.BlockSpec((1, B_DH), lambda b, h: (0, 0)),
            pl.BlockSpec((B_GROUPS, B_DH), lambda b, h: (0, 0)),
            pl.BlockSpec((B_GROUPS, B_DH), lambda b, h: (0, 0)),
            *qkv_specs,
            pl.BlockSpec((None, seq, B_DH), lambda b, h: (b, 0, z_first + h)),
        ],
        out_specs=pl.BlockSpec((None, seq, B_DH), lambda b, h: (b, 0, h)),
        out_shape=jax.ShapeDtypeStruct((bsz, seq, B_W), BF16),
        scratch_shapes=[
            pltpu.VMEM((seq, B_DH), F32),
            pltpu.VMEM((seq, B_DH), F32),
            pltpu.VMEM((seq, B_DH), F32),
            pltpu.VMEM((seq, B_DH), F32),
            pltpu.VMEM((seq, B_DH), F32),
            pltpu.VMEM((B_GROUPS, seq, B_DH), F32),
            pltpu.VMEM((B_GROUPS, seq, B_DH), F32),
        ],
        compiler_params=pltpu.CompilerParams(
            dimension_semantics=("parallel", "arbitrary"), vmem_limit_bytes=VMEM_LIMIT_BYTES),
        name="dilated_attn",
    )(pos3, inv_freq, q_gain, k_gain, *([proj3] * 9), proj3)


def kernel(x, positions, norm_g, a_w_in, a_conv_w, a_log, a_dt_bias, a_norm_g, a_w_out,
           b_w_in, b_q_norm_g, b_k_norm_g, b_w_out):
    bsz, seq, d = x.shape
    t = bsz * seq
    x2 = x.reshape(t, d)

    w_main = a_w_in[0, :, :A_MAIN].astype(BF16)
    w_gate = a_w_in[0, :, A_MAIN:].astype(BF16)
    zeros8 = jnp.zeros((A_HEADS,), F32)
    alog16 = jnp.concatenate([zeros8, a_log[0].astype(F32)])[None, :]
    dtb16 = jnp.concatenate([zeros8, a_dt_bias[0].astype(F32)])[None, :]
    proj, gates = _norm_proj_gdn(x2, norm_g[0][None, :], w_main, w_gate, alog16, dtb16)
    o = _gdn_core(proj.reshape(bsz, seq, A_MAIN), gates.reshape(bsz, seq, 2 * A_HEADS),
                  a_conv_w[0], a_norm_g[0][None, :])
    x2 = _out_proj(o.reshape(t, A_VW), a_w_out[0].astype(BF16), x2, "gdn_out_proj")

    proj = _norm_proj(x2, norm_g[1][None, :], b_w_in[0].astype(BF16))
    inv_freq = ROPE_THETA ** (-jnp.arange(0, ROPE_DIMS, 2, dtype=F32) / ROPE_DIMS)
    inv_freq = jnp.concatenate(
        [inv_freq, inv_freq, jnp.zeros((B_DH - ROPE_DIMS,), F32)])[None, :]
    o = _attn_core(proj.reshape(bsz, seq, proj.shape[1]), positions[:, :, None], inv_freq,
                   b_q_norm_g[0], b_k_norm_g[0])
    x2 = _out_proj(o.reshape(t, B_W), b_w_out[0].astype(BF16), x2, "attn_out_proj")
    return x2.reshape(bsz, seq, d)
```

```python
import functools

import jax
import jax.numpy as jnp
from jax import lax
from jax.experimental import pallas as pl
from jax.experimental.pallas import tpu as pltpu

F32 = jnp.float32
BF16 = jnp.bfloat16
EPS = 1e-6

D_MODEL = 1024
A_HEADS = 8
A_DK = 128
A_DV = 256
A_QK = A_HEADS * A_DK
A_VW = A_HEADS * A_DV
A_MAIN = 2 * A_QK + 2 * A_VW
CHUNK = 128
GROUP = 4
B_WINDOWS = (128, 512, 2048)
B_DILATIONS = (1, 4, 16)
B_GROUPS = 3
B_HEADS = 8
B_DH = 128
B_W = B_HEADS * B_DH
B_BLOCK = 128
ROPE_THETA = 500000.0
ROPE_DIMS = B_DH // 4
ROPE_HALF = ROPE_DIMS // 2
ATTN_BATCH = 4
ATTN_LOOKAHEAD = 2
ATTN_PAD = B_BLOCK * max(B_DILATIONS)

VMEM_LIMIT_BYTES = 56 * 1024 * 1024
ROW_TILE = 512


def _split3(a):
    hi = a.astype(BF16)
    r = a - hi.astype(F32)
    mid = r.astype(BF16)
    lo = (r - mid.astype(F32)).astype(BF16)
    return hi, mid, lo


def _bdot(a, b):
    return jnp.einsum("cik,ckj->cij", a, b, preferred_element_type=F32)


def _normed(x_ref, g_ref):
    x = x_ref[...]
    ms = jnp.mean(x * x, axis=-1, keepdims=True)
    return (x * lax.rsqrt(ms + EPS) * g_ref[...]).astype(BF16)


def _proj_cols(hn, w_ref, o_ref, tn):
    n_total = o_ref.shape[1]
    for n0 in range(0, n_total, tn):
        o_ref[:, n0:n0 + tn] = jnp.dot(
            hn, w_ref[:, n0:n0 + tn], preferred_element_type=F32).astype(o_ref.dtype)


def _norm_proj_gdn_kernel(x_ref, g_ref, w_ref, wg_ref, alog_ref, dtb_ref, o_ref, gates_ref):
    hn = _normed(x_ref, g_ref)
    _proj_cols(hn, w_ref, o_ref, 512)
    logits = jnp.dot(hn, wg_ref[...], preferred_element_type=F32)
    lane = lax.broadcasted_iota(jnp.int32, logits.shape, 1)
    xs = logits + dtb_ref[...]
    softplus = jnp.maximum(xs, 0.0) + jnp.log1p(jnp.exp(-jnp.abs(xs)))
    decay = -jnp.exp(alog_ref[...]) * softplus
    gates_ref[...] = jnp.where(lane < A_HEADS, jax.nn.sigmoid(logits), decay)


def _norm_proj_kernel(x_ref, g_ref, w_ref, o_ref):
    hn = _normed(x_ref, g_ref)
    _proj_cols(hn, w_ref, o_ref, 512)


def _resident(shape):
    return pl.BlockSpec(shape, lambda i: (0,) * len(shape), pipeline_mode=pl.Buffered(1))


def _norm_proj_gdn(x2, g, w, wg, alog16, dtb16):
    t, d = x2.shape
    n = w.shape[1]
    tm = ROW_TILE
    return pl.pallas_call(
        _norm_proj_gdn_kernel,
        grid=(t // tm,),
        in_specs=[
            pl.BlockSpec((tm, d), lambda i: (i, 0)),
            _resident((1, d)),
            _resident((d, n)),
            _resident((d, 2 * A_HEADS)),
            _resident((1, 2 * A_HEADS)),
            _resident((1, 2 * A_HEADS)),
        ],
        out_specs=[
            pl.BlockSpec((tm, n), lambda i: (i, 0)),
            pl.BlockSpec((tm, 2 * A_HEADS), lambda i: (i, 0)),
        ],
        out_shape=[
            jax.ShapeDtypeStruct((t, n), BF16),
            jax.ShapeDtypeStruct((t, 2 * A_HEADS), F32),
        ],
        compiler_params=pltpu.CompilerParams(
            dimension_semantics=("parallel",), vmem_limit_bytes=VMEM_LIMIT_BYTES),
        name="gdn_norm_proj",
    )(x2, g, w, wg, alog16, dtb16)


def _norm_proj(x2, g, w):
    t, d = x2.shape
    n = w.shape[1]
    tm = ROW_TILE // 2
    return pl.pallas_call(
        _norm_proj_kernel,
        grid=(t // tm,),
        in_specs=[
            pl.BlockSpec((tm, d), lambda i: (i, 0)),
            _resident((1, d)),
            _resident((d, n)),
        ],
        out_specs=pl.BlockSpec((tm, n), lambda i: (i, 0)),
        out_shape=jax.ShapeDtypeStruct((t, n), BF16),
        compiler_params=pltpu.CompilerParams(
            dimension_semantics=("parallel",), vmem_limit_bytes=VMEM_LIMIT_BYTES),
        name="attn_norm_proj",
    )(x2, g, w)


def _out_proj_kernel(o_ref, w_ref, x_ref, y_ref):
    y_ref[...] = x_ref[...] + jnp.dot(o_ref[...], w_ref[...], preferred_element_type=F32)


def _out_proj(o2, w, x2, name):
    t, k = o2.shape
    d = w.shape[1]
    tm = ROW_TILE
    return pl.pallas_call(
        _out_proj_kernel,
        grid=(t // tm,),
        in_specs=[
            pl.BlockSpec((tm, k), lambda i: (i, 0)),
            _resident((k, d)),
            pl.BlockSpec((tm, d), lambda i: (i, 0)),
        ],
        out_specs=pl.BlockSpec((tm, d), lambda i: (i, 0)),
        out_shape=jax.ShapeDtypeStruct((t, d), F32),
        compiler_params=pltpu.CompilerParams(
            dimension_semantics=("parallel",), vmem_limit_bytes=VMEM_LIMIT_BYTES),
        name=name,
    )(o2, w, x2)


def _gdn_kernel(q_ref, k_ref, v_ref, z_ref, gates_ref, cwq_ref, cwk_ref, cwv_ref, ng_ref,
                o_ref, s_ref):
    head = pl.program_id(1)
    rows = GROUP * CHUNK
    n_groups = q_ref.shape[0] // rows
    c = CHUNK

    ii = lax.broadcasted_iota(jnp.int32, (c, c), 0)
    jj = lax.broadcasted_iota(jnp.int32, (c, c), 1)
    eye = jnp.where(ii == jj, 1.0, 0.0).astype(F32)
    lower_incl = ii >= jj
    lower_strict = ii > jj
    cum_lhs = jnp.broadcast_to(jnp.where(lower_incl, 1.0, 0.0).astype(BF16)[None], (GROUP, c, c))

    def same_block(size):
        return (ii // size) == (jj // size)

    sel_row = lax.broadcasted_iota(jnp.int32, (2 * A_HEADS, 128), 0)
    sel_beta = jnp.where(sel_row == head, 1.0, 0.0).astype(BF16)
    sel_decay = jnp.where(sel_row == head + A_HEADS, 1.0, 0.0).astype(BF16)

    s_ref[...] = jnp.zeros_like(s_ref)

    def group_body(g, carry):
        r0 = pl.multiple_of(g * rows, rows)
        h0 = pl.multiple_of(jnp.maximum(r0 - 16, 0), 16)

        def conv_silu(ref, cw_ref):
            main = ref[pl.ds(r0, rows), :].astype(F32)
            halo = ref[pl.ds(h0, 16), :].astype(F32)
            halo = jnp.where(g > 0, halo, 0.0)
            xcat = jnp.concatenate([halo, main], axis=0)
            w = cw_ref[...]
            acc = xcat * w[3:4, :]
            for tap in range(3):
                acc = acc + pltpu.roll(xcat, 3 - tap, 0) * w[tap:tap + 1, :]
            acc = acc[16:, :]
            return acc * jax.nn.sigmoid(acc)

        def l2n(x):
            return x * lax.rsqrt(jnp.sum(x * x, axis=-1, keepdims=True) + EPS)

        q = l2n(conv_silu(q_ref, cwq_ref)) * (A_DK ** -0.5)
        k = l2n(conv_silu(k_ref, cwk_ref))
        v = conv_silu(v_ref, cwv_ref)

        gh, gm, gl = _split3(gates_ref[pl.ds(r0, rows), :])

        def select(sel):
            return (jnp.dot(gh, sel, preferred_element_type=F32)
                    + jnp.dot(gm, sel, preferred_element_type=F32)
                    + jnp.dot(gl, sel, preferred_element_type=F32))

        beta = select(sel_beta).reshape(GROUP, c, 128)
        gstep = select(sel_decay).reshape(GROUP, c, 128)
        sh, sm, sl = _split3(gstep)
        gc = _bdot(cum_lhs, sh) + _bdot(cum_lhs, sm) + _bdot(cum_lhs, sl)
        gc_row = jnp.swapaxes(gc, 1, 2)
        decay = jnp.exp(jnp.where(lower_incl[None], gc - gc_row, -jnp.inf))
        gc_last = gc[:, c - 1:c, :]
        exp_gc = jnp.exp(gc)

        q3 = q.reshape(GROUP, c, A_DK)
        k3 = k.reshape(GROUP, c, A_DK)
        v3 = v.reshape(GROUP, c, A_DV)
        kb3 = k3 * beta
        k3b = k3.astype(BF16)
        kk = jnp.einsum("cid,cjd->cij", kb3.astype(BF16), k3b, preferred_element_type=F32)
        qk = jnp.einsum("cid,cjd->cij", q3.astype(BF16), k3b, preferred_element_type=F32) * decay
        low = jnp.where(lower_strict[None], kk * decay, 0.0)

        m1 = jnp.where(same_block(8)[None], -low, 0.0)
        m1b = m1.astype(BF16)
        tinv = eye[None] + m1
        m2b = _bdot(m1b, m1b).astype(BF16)
        tinv = tinv + _bdot(tinv.astype(BF16), m2b)
        m4b = _bdot(m2b, m2b).astype(BF16)
        tinv = tinv + _bdot(tinv.astype(BF16), m4b)
        size = 8
        while size < c:
            off = jnp.where((same_block(2 * size) & jnp.logical_not(same_block(size)))[None], low, 0.0)
            tb = tinv.astype(BF16)
            tinv = tinv - _bdot(_bdot(tb, off.astype(BF16)).astype(BF16), tb)
            size *= 2

        tb = tinv.astype(BF16)
        vb = (v3 * jnp.concatenate([beta, beta], axis=-1)).astype(BF16)
        kbg = (kb3 * exp_gc).astype(BF16)
        u = _bdot(tb, vb)
        w = _bdot(tb, kbg)

        qt = q3 * exp_gc
        kd = (k3 * jnp.exp(gc_last - gc)).astype(BF16)
        e_last = jnp.exp(gc_last)
        e_last = jnp.concatenate([e_last, e_last], axis=-1)
        ub = u.astype(BF16)
        wb = w.astype(BF16)
        qkb = qk.astype(BF16)
        kdt = jnp.swapaxes(kd, 1, 2)
        a_neg = -_bdot(kdt, wb)
        b_add = _bdot(kdt, ub)
        q_eff = qt - _bdot(qkb, wb)
        o_loc = _bdot(qkb, ub)
        p_all = jnp.concatenate([a_neg, q_eff], axis=1).astype(BF16)

        ng = ng_ref[...]
        for cc in range(GROUP):
            state = s_ref[...]
            r = jnp.dot(p_all[cc], state.astype(BF16), preferred_element_type=F32)
            o = r[A_DK:, :] + o_loc[cc]
            s_ref[...] = state * e_last[cc] + r[:A_DK, :] + b_add[cc]
            on = o * lax.rsqrt(jnp.mean(o * o, axis=-1, keepdims=True) + EPS) * ng
            row = pl.multiple_of(r0 + cc * c, c)
            zc = z_ref[pl.ds(row, c), :].astype(F32)
            o_ref[pl.ds(row, c), :] = (on * (zc * jax.nn.sigmoid(zc))).astype(o_ref.dtype)
        return carry

    lax.fori_loop(0, n_groups, group_body, 0)


def _gdn_core(proj3, gates3, conv_w, norm_g):
    bsz, seq, _ = proj3.shape
    qb = A_QK // A_DK
    vb = 2 * A_QK // A_DV
    zb = vb + A_VW // A_DV
    return pl.pallas_call(
        _gdn_kernel,
        grid=(bsz, A_HEADS),
        in_specs=[
            pl.BlockSpec((None, seq, A_DK), lambda b, h: (b, 0, h)),
            pl.BlockSpec((None, seq, A_DK), lambda b, h: (b, 0, qb + h)),
            pl.BlockSpec((None, seq, A_DV), lambda b, h: (b, 0, vb + h)),
            pl.BlockSpec((None, seq, A_DV), lambda b, h: (b, 0, zb + h)),
            pl.BlockSpec((None, seq, 2 * A_HEADS), lambda b, h: (b, 0, 0)),
            pl.BlockSpec((4, A_DK), lambda b, h: (0, h)),
            pl.BlockSpec((4, A_DK), lambda b, h: (0, qb + h)),
            pl.BlockSpec((4, A_DV), lambda b, h: (0, vb + h)),
            pl.BlockSpec((1, A_DV), lambda b, h: (0, 0)),
        ],
        out_specs=pl.BlockSpec((None, seq, A_DV), lambda b, h: (b, 0, h)),
        out_shape=jax.ShapeDtypeStruct((bsz, seq, A_VW), BF16),
        scratch_shapes=[pltpu.VMEM((A_DK, A_DV), F32)],
        compiler_params=pltpu.CompilerParams(
            dimension_semantics=("parallel", "parallel"), vmem_limit_bytes=VMEM_LIMIT_BYTES),
        name="gdn_core",
    )(proj3, proj3, proj3, proj3, gates3, conv_w, conv_w, conv_w, norm_g)


def _attn_kernel(pos_ref, invf_ref, qg_ref, kg_ref,
                 q0_ref, k0_ref, v0_ref, q1_ref, k1_ref, v1_ref, q2_ref, k2_ref, v2_ref, z_ref,
                 o_ref, cos_ref, sin_ref, qs_ref, ks_ref, vs_ref, og_ref, lg_ref):
    seq = q0_ref.shape[0]
    blk = B_BLOCK
    nb = ATTN_BATCH
    tile = nb * blk
    n_tiles = seq // tile
    pad = ATTN_PAD

    @pl.when(pl.program_id(1) == 0)
    def _():
        ang = pos_ref[...].astype(F32) * invf_ref[...]
        lane = lax.broadcasted_iota(jnp.int32, ang.shape, 1)
        cos_ref[...] = jnp.cos(ang)
        sin_ref[...] = jnp.where(lane < ROPE_HALF, -1.0, 1.0) * jnp.sin(ang)

    ks_ref[0:pad, :] = jnp.zeros((pad, B_DH), F32)
    vs_ref[0:pad, :] = jnp.zeros((pad, B_DH), F32)

    qi = lax.broadcasted_iota(jnp.int32, (blk, blk), 0)
    kj = lax.broadcasted_iota(jnp.int32, (blk, blk), 1)
    cur_mask = (kj <= qi)[None]
    prev_mask = (kj >= qi)[None]
    assert all(w // d == blk for w, d in zip(B_WINDOWS, B_DILATIONS))
    batch_idx = lax.broadcasted_iota(jnp.int32, (nb, blk, blk), 0)

    pk = lax.broadcasted_iota(jnp.int32, (B_DH, B_DH), 0)
    pc = lax.broadcasted_iota(jnp.int32, (B_DH, B_DH), 1)
    swap = ((pc < ROPE_HALF) & (pk == pc + ROPE_HALF)) | (
        (pc >= ROPE_HALF) & (pc < ROPE_DIMS) & (pk == pc - ROPE_HALF))
    swap = jnp.where(swap, 1.0, 0.0).astype(BF16)
    mean_mat = jnp.full((B_DH, B_DH), 1.0 / B_DH, BF16)
    ones_v = jnp.ones((nb, blk, B_DH), BF16)

    q_refs = (q0_ref, q1_ref, q2_ref)
    k_refs = (k0_ref, k1_ref, k2_ref)
    v_refs = (v0_ref, v1_ref, v2_ref)

    def load_blocks(ref, offset, pieces, dil):
        parts = []
        for first, count in pieces:
            rows = pl.ds(offset + first, count * blk) if dil == 1 else pl.ds(
                offset + first, count * blk, stride=dil)
            parts.append(ref[rows, :].astype(BF16).reshape(count, blk, B_DH))
        return parts[0] if len(parts) == 1 else jnp.concatenate(parts, axis=0)

    def scores(pieces, dil, pmask):
        qb = load_blocks(qs_ref, 0, pieces, dil)
        kc = load_blocks(ks_ref, pad, pieces, dil)
        s_cur = jnp.where(cur_mask, jnp.einsum("bqd,bkd->bqk", qb, kc, preferred_element_type=F32),
                          -jnp.inf)
        if pmask is None:
            return s_cur, None
        kp = load_blocks(ks_ref, pad - blk * dil, pieces, dil)
        s_prev = jnp.where(pmask, jnp.einsum("bqd,bkd->bqk", qb, kp, preferred_element_type=F32),
                           -jnp.inf)
        return s_cur, s_prev

    def finish(gi, pieces, dil, s_cur, s_prev):
        if s_prev is None:
            m = jnp.max(s_cur, axis=-1, keepdims=True)
        else:
            m = jnp.max(jnp.maximum(s_cur, s_prev), axis=-1, keepdims=True)
        vc = load_blocks(vs_ref, pad, pieces, dil)
        acc = jnp.einsum("bqk,bkd->bqd", jnp.exp(s_cur - m).astype(BF16),
                         jnp.concatenate([vc, ones_v], axis=-1), preferred_element_type=F32)
        if s_prev is not None:
            vp = load_blocks(vs_ref, pad - blk * dil, pieces, dil)
            acc = acc + jnp.einsum("bqk,bkd->bqd", jnp.exp(s_prev - m).astype(BF16),
                                   jnp.concatenate([vp, ones_v], axis=-1), preferred_element_type=F32)
        den = acc[..., B_DH:]
        o = acc[..., :B_DH] / den
        lse = m + jnp.log(den)
        b0 = 0
        for first, count in pieces:
            rows = pl.ds(first, count * blk) if dil == 1 else pl.ds(first, count * blk, stride=dil)
            og_ref[gi, rows, :] = o[b0:b0 + count].reshape(count * blk, B_DH)
            lg_ref[gi, rows, :] = lse[b0:b0 + count].reshape(count * blk, B_DH)
            b0 += count

    for gi in range(B_GROUPS):
        dil = B_DILATIONS[gi]

        def prep(t, carry, gi=gi):
            r0 = pl.multiple_of(t * tile, tile)
            cos = cos_ref[pl.ds(r0, tile), :]
            sin = sin_ref[pl.ds(r0, tile), :]

            def norm_rope(ref, gain):
                xb = ref[pl.ds(r0, tile), :]
                ms = jnp.dot(xb * xb, mean_mat, preferred_element_type=F32)
                y = xb.astype(F32) * lax.rsqrt(ms + EPS) * gain
                partner = jnp.dot(y.astype(BF16), swap, preferred_element_type=F32)
                return y * cos + partner * sin

            qs_ref[pl.ds(r0, tile), :] = norm_rope(q_refs[gi], qg_ref[gi:gi + 1, :] * (B_DH ** -0.5))
            ks_ref[pl.ds(pad + r0, tile), :] = norm_rope(k_refs[gi], kg_ref[gi:gi + 1, :])
            vs_ref[pl.ds(pad + r0, tile), :] = v_refs[gi][pl.ds(r0, tile), :].astype(F32)
            return carry

        lax.fori_loop(0, n_tiles, prep, 0)

        stream_blocks = seq // dil // blk
        batches = []
        if stream_blocks >= nb:
            for r in range(dil):
                for b in range(stream_blocks // nb):
                    first_of_stream = b == 0
                    pmask = prev_mask & (batch_idx > 0) if first_of_stream else prev_mask
                    batches.append(([(r + b * nb * blk * dil, nb)], pmask))
        else:
            assert stream_blocks == 1
            for r in range(0, dil, nb):
                batches.append(([(r + j, 1) for j in range(nb)], None))

        pending = {}
        for i in range(len(batches) + ATTN_LOOKAHEAD):
            if i < len(batches):
                pending[i] = scores(batches[i][0], dil, batches[i][1])
            if i >= ATTN_LOOKAHEAD:
                j = i - ATTN_LOOKAHEAD
                finish(gi, batches[j][0], dil, *pending.pop(j))

    def merge(t, carry):
        r0 = pl.multiple_of(t * tile, tile)
        l0 = lg_ref[0, pl.ds(r0, tile), :]
        l1 = lg_ref[1, pl.ds(r0, tile), :]
        l2 = lg_ref[2, pl.ds(r0, tile), :]
        m = jnp.maximum(jnp.maximum(l0, l1), l2)
        w0 = jnp.exp(l0 - m)
        w1 = jnp.exp(l1 - m)
        w2 = jnp.exp(l2 - m)
        num = (w0 * og_ref[0, pl.ds(r0, tile), :] + w1 * og_ref[1, pl.ds(r0, tile), :]
               + w2 * og_ref[2, pl.ds(r0, tile), :])
        o = num / (w0 + w1 + w2)
        z = z_ref[pl.ds(r0, tile), :].astype(F32)
        o_ref[pl.ds(r0, tile), :] = (o * (z * jax.nn.sigmoid(z))).astype(o_ref.dtype)
        return carry

    lax.fori_loop(0, n_tiles, merge, 0)


def _attn_core(proj3, pos3, inv_freq, q_gain, k_gain):
    bsz, seq, _ = proj3.shape

    def col(which, gi):
        first = (which * B_GROUPS + gi) * B_HEADS
        return pl.BlockSpec((None, seq, B_DH), lambda b, h: (b, 0, first + h))

    qkv_specs = []
    for gi in range(B_GROUPS):
        qkv_specs += [col(0, gi), col(1, gi), col(2, gi)]
    z_first = 3 * B_GROUPS * B_HEADS
    return pl.pallas_call(
        _attn_kernel,
        grid=(bsz, B_HEADS),
        in_specs=[
            pl.BlockSpec((None, seq, 1), lambda b, h: (b, 0, 0)),
            pl.BlockSpec((1, B_DH), lambda b, h: (0, 0)),
            pl.BlockSpec((B_GROUPS, B_DH), lambda b, h: (0, 0)),
            pl.BlockSpec((B_GROUPS, B_DH), lambda b, h: (0, 0)),
            *qkv_specs,
            pl.BlockSpec((None, seq, B_DH), lambda b, h: (b, 0, z_first + h)),
        ],
        out_specs=pl.BlockSpec((None, seq, B_DH), lambda b, h: (b, 0, h)),
        out_shape=jax.ShapeDtypeStruct((bsz, seq, B_W), BF16),
        scratch_shapes=[
            pltpu.VMEM((seq, B_DH), F32),
            pltpu.VMEM((seq, B_DH), F32),
            pltpu.VMEM((seq, B_DH), F32),
            pltpu.VMEM((ATTN_PAD + seq, B_DH), F32),
            pltpu.VMEM((ATTN_PAD + seq, B_DH), F32),
            pltpu.VMEM((B_GROUPS, seq, B_DH), F32),
            pltpu.VMEM((B_GROUPS, seq, B_DH), F32),
        ],
        compiler_params=pltpu.CompilerParams(
            dimension_semantics=("parallel", "arbitrary"), vmem_limit_bytes=VMEM_LIMIT_BYTES),
        name="dilated_attn",
    )(pos3, inv_freq, q_gain, k_gain, *([proj3] * 9), proj3)


def kernel(x, positions, norm_g, a_w_in, a_conv_w, a_log, a_dt_bias, a_norm_g, a_w_out,
           b_w_in, b_q_norm_g, b_k_norm_g, b_w_out):
    bsz, seq, d = x.shape
    t = bsz * seq
    x2 = x.reshape(t, d)

    w_main = a_w_in[0, :, :A_MAIN].astype(BF16)
    w_gate = a_w_in[0, :, A_MAIN:].astype(BF16)
    zeros8 = jnp.zeros((A_HEADS,), F32)
    alog16 = jnp.concatenate([zeros8, a_log[0].astype(F32)])[None, :]
    dtb16 = jnp.concatenate([zeros8, a_dt_bias[0].astype(F32)])[None, :]
    proj, gates = _norm_proj_gdn(x2, norm_g[0][None, :], w_main, w_gate, alog16, dtb16)
    o = _gdn_core(proj.reshape(bsz, seq, A_MAIN), gates.reshape(bsz, seq, 2 * A_HEADS),
                  a_conv_w[0], a_norm_g[0][None, :])
    x2 = _out_proj(o.reshape(t, A_VW), a_w_out[0].astype(BF16), x2, "gdn_out_proj")

    proj = _norm_proj(x2, norm_g[1][None, :], b_w_in[0].astype(BF16))
    inv_freq = ROPE_THETA ** (-jnp.arange(0, ROPE_DIMS, 2, dtype=F32) / ROPE_DIMS)
    inv_freq = jnp.concatenate(
        [inv_freq, inv_freq, jnp.zeros((B_DH - ROPE_DIMS,), F32)])[None, :]
    o = _attn_core(proj.reshape(bsz, seq, proj.shape[1]), positions[:, :, None], inv_freq,
                   b_q_norm_g[0], b_k_norm_g[0])
    x2 = _out_proj(o.reshape(t, B_W), b_w_out[0].astype(BF16), x2, "attn_out_proj")
    return x2.reshape(bsz, seq, d)
```

```python
import jax
import jax.numpy as jnp
import numpy as np
from jax import lax
from jax.experimental import pallas as pl
from jax.experimental.pallas import tpu as pltpu

F32 = jnp.float32
BF16 = jnp.bfloat16
EPS = 1e-6

D_MODEL = 1024
A_HEADS = 8
A_DK = 128
A_DV = 256
A_QK = A_HEADS * A_DK
A_VW = A_HEADS * A_DV
A_CONVW = 2 * A_QK + A_VW
A_MAIN = A_CONVW + A_VW
A_CONV = 4
CHUNK = 128
GROUP = 4
B_WINDOWS = (128, 512, 2048)
B_DILATIONS = (1, 4, 16)
B_GROUPS = 3
B_HEADS = 8
B_DH = 128
B_W = B_HEADS * B_DH
B_BLOCK = 128
ROPE_THETA = 500000.0
ROPE_DIMS = B_DH // 4
ROPE_HALF = ROPE_DIMS // 2
ROPE_SHIFT = B_DH // 2
ATTN_BATCH = 4
ATTN_LOOKAHEAD = 2

VMEM_LIMIT_BYTES = 56 * 1024 * 1024
ROW_TILE = 512
ATTN_ROW_TILE = 256
COL_TILE = 512
HALO = 16


def _split3(a):
    hi = a.astype(BF16)
    r = a - hi.astype(F32)
    mid = r.astype(BF16)
    lo = (r - mid.astype(F32)).astype(BF16)
    return hi, mid, lo


def _bdot(a, b):
    return jnp.einsum("cik,ckj->cij", a, b, preferred_element_type=F32)


def _normed_f32(x_ref, g_ref):
    x = x_ref[...]
    ms = jnp.mean(x * x, axis=-1, keepdims=True)
    return x * lax.rsqrt(ms + EPS) * g_ref[...]


def _resident(shape):
    return pl.BlockSpec(shape, lambda i: (0,) * len(shape), pipeline_mode=pl.Buffered(1))


def _gdn_proj_kernel(x_ref, xh_ref, g_ref, w_ref, wgt_ref, alog_ref, dtb_ref, cw_ref,
                     o_ref, gates_ref, *, tiles_per_seq):
    tm = x_ref.shape[0]
    first = (pl.program_id(0) % tiles_per_seq) == 0
    hn = _normed_f32(x_ref, g_ref).astype(BF16)
    halo = jnp.where(first, 0.0, _normed_f32(xh_ref, g_ref)).astype(BF16)
    hcat = jnp.concatenate([halo, hn], axis=0)

    for n0 in range(0, A_CONVW, COL_TILE):
        xcat = jnp.dot(hcat, w_ref[:, n0:n0 + COL_TILE], preferred_element_type=F32)
        cw = cw_ref[:, n0:n0 + COL_TILE]
        acc = xcat * cw[A_CONV - 1:A_CONV, :]
        for tap in range(A_CONV - 1):
            acc = acc + pltpu.roll(xcat, A_CONV - 1 - tap, 0) * cw[tap:tap + 1, :]
        acc = acc[HALO:, :]
        y = acc * jax.nn.sigmoid(acc)
        if n0 < 2 * A_QK:
            scale = A_DK ** -0.5 if n0 < A_QK else 1.0
            heads = []
            for j in range(COL_TILE // A_DK):
                yh = y[:, j * A_DK:(j + 1) * A_DK]
                inv = lax.rsqrt(jnp.sum(yh * yh, axis=-1, keepdims=True) + EPS)
                heads.append(yh * (inv * scale))
            y = jnp.concatenate(heads, axis=-1)
        o_ref[:, n0:n0 + COL_TILE] = y.astype(o_ref.dtype)

    for n0 in range(A_CONVW, A_MAIN, COL_TILE):
        o_ref[:, n0:n0 + COL_TILE] = jnp.dot(
            hn, w_ref[:, n0:n0 + COL_TILE], preferred_element_type=F32).astype(o_ref.dtype)

    logits = lax.dot_general(wgt_ref[...], hn, (((1,), (1,)), ((), ())), preferred_element_type=F32)
    row = lax.broadcasted_iota(jnp.int32, logits.shape, 0)
    xs = logits + dtb_ref[...]
    softplus = jnp.maximum(xs, 0.0) + jnp.log1p(jnp.exp(-jnp.abs(xs)))
    decay = -jnp.exp(alog_ref[...]) * softplus
    gates_ref[...] = jnp.where(row < A_HEADS, jax.nn.sigmoid(logits), decay)


def _gdn_proj(x2, g, w, wgt, alog16, dtb16, conv_w, seq):
    t, d = x2.shape
    tm = ROW_TILE
    kern = lambda *refs: _gdn_proj_kernel(*refs, tiles_per_seq=seq // tm)
    return pl.pallas_call(
        kern,
        grid=(t // tm,),
        in_specs=[
            pl.BlockSpec((tm, d), lambda i: (i, 0)),
            pl.BlockSpec((HALO, d), lambda i: (jnp.maximum(i * (tm // HALO) - 1, 0), 0)),
            _resident((1, d)),
            _resident((d, A_MAIN)),
            _resident((2 * A_HEADS, d)),
            _resident((2 * A_HEADS, 1)),
            _resident((2 * A_HEADS, 1)),
            _resident((A_CONV, A_CONVW)),
        ],
        out_specs=[
            pl.BlockSpec((tm, A_MAIN), lambda i: (i, 0)),
            pl.BlockSpec((2 * A_HEADS, tm), lambda i: (0, i)),
        ],
        out_shape=[
            jax.ShapeDtypeStruct((t, A_MAIN), BF16),
            jax.ShapeDtypeStruct((2 * A_HEADS, t), F32),
        ],
        compiler_params=pltpu.CompilerParams(
            dimension_semantics=("parallel",), vmem_limit_bytes=VMEM_LIMIT_BYTES),
        name="gdn_norm_proj",
    )(x2, x2, g, w, wgt, alog16, dtb16, conv_w)


def _out_proj_kernel(o_ref, w_ref, x_ref, y_ref):
    y_ref[...] = x_ref[...] + jnp.dot(o_ref[...], w_ref[...], preferred_element_type=F32)


def _out_proj(o2, w, x2, name):
    t, k = o2.shape
    d = w.shape[1]
    tm = ROW_TILE
    return pl.pallas_call(
        _out_proj_kernel,
        grid=(t // tm,),
        in_specs=[
            pl.BlockSpec((tm, k), lambda i: (i, 0)),
            _resident((k, d)),
            pl.BlockSpec((tm, d), lambda i: (i, 0)),
        ],
        out_specs=pl.BlockSpec((tm, d), lambda i: (i, 0)),
        out_shape=jax.ShapeDtypeStruct((t, d), F32),
        compiler_params=pltpu.CompilerParams(
            dimension_semantics=("parallel",), vmem_limit_bytes=VMEM_LIMIT_BYTES),
        name=name,
    )(o2, w, x2)


def _gdn_kernel(q_ref, k_ref, v_ref, z_ref, gt_ref, ng_ref, o_ref, s_ref, rf_ref):
    head = pl.program_id(1)
    rows = GROUP * CHUNK
    n_groups = q_ref.shape[0] // rows
    c = CHUNK

    ii = lax.broadcasted_iota(jnp.int32, (c, c), 0)
    jj = lax.broadcasted_iota(jnp.int32, (c, c), 1)
    eye = jnp.where(ii == jj, 1.0, 0.0).astype(F32)
    lower_incl = ii >= jj
    lower_strict = ii > jj
    cum_rhs = jnp.where(ii <= jj, 1.0, 0.0).astype(BF16)

    def same_block(size):
        return (ii // size) == (jj // size)

    def col(x):
        return jnp.swapaxes(jnp.broadcast_to(x[:, None, :], (GROUP, c, c)), 1, 2)

    s_ref[...] = jnp.zeros_like(s_ref)

    n_chunks = q_ref.shape[0] // c

    def chunk_rows(r):
        full = gt_ref[pl.ds(r, 1), :]
        return jnp.concatenate([full[:, n * c:(n + 1) * c] for n in range(n_chunks)], axis=0)

    beta_all = chunk_rows(head)
    gh, gm, gl = _split3(chunk_rows(head + A_HEADS))
    gc_all = (jnp.dot(gh, cum_rhs, preferred_element_type=F32)
              + jnp.dot(gm, cum_rhs, preferred_element_type=F32)
              + jnp.dot(gl, cum_rhs, preferred_element_type=F32))
    exp_all = jnp.exp(gc_all)
    last_all = jnp.broadcast_to(gc_all[:, c - 1:c], gc_all.shape)
    rowforms = (beta_all, gc_all, exp_all, beta_all * exp_all, jnp.exp(last_all - gc_all),
                jnp.exp(last_all))
    for idx, val in enumerate(rowforms):
        for n in range(n_groups):
            rf_ref[idx, n] = val[n * GROUP:(n + 1) * GROUP, :]

    def chunk_local(g):
        r0 = g * rows
        q3b = q_ref[pl.ds(r0, rows), :].reshape(GROUP, c, A_DK)
        k3b = k_ref[pl.ds(r0, rows), :].reshape(GROUP, c, A_DK)
        q3 = q3b.astype(F32)
        k3 = k3b.astype(F32)
        v3 = v_ref[pl.ds(r0, rows), :].astype(F32).reshape(GROUP, c, A_DV)

        gc = rf_ref[1, g]
        beta_c = col(rf_ref[0, g])
        gc_c = col(gc)
        exp_gc_c = col(rf_ref[2, g])
        beta_exp_c = col(rf_ref[3, g])
        tail_c = col(rf_ref[4, g])
        e_last = rf_ref[5, g]
        e_last = jnp.concatenate([e_last, e_last], axis=-1)
        decay = jnp.exp(jnp.where(lower_incl[None], gc_c - gc[:, None, :], -jnp.inf))

        kb3 = (k3 * beta_c).astype(BF16)
        kk = jnp.einsum("cid,cjd->cij", kb3, k3b, preferred_element_type=F32)
        qk = jnp.einsum("cid,cjd->cij", q3b, k3b, preferred_element_type=F32) * decay
        low = jnp.where(lower_strict[None], kk * decay, 0.0)

        m1 = jnp.where(same_block(8)[None], -low, 0.0)
        m1b = m1.astype(BF16)
        tinv = eye[None] + m1
        m2b = _bdot(m1b, m1b).astype(BF16)
        tinv = tinv + _bdot(tinv.astype(BF16), m2b)
        m4b = _bdot(m2b, m2b).astype(BF16)
        tinv = tinv + _bdot(tinv.astype(BF16), m4b)
        size = 8
        while size < c:
            off = jnp.where((same_block(2 * size) & jnp.logical_not(same_block(size)))[None], low, 0.0)
            tb = tinv.astype(BF16)
            tinv = tinv - _bdot(_bdot(tb, off.astype(BF16)).astype(BF16), tb)
            size *= 2

        tb = tinv.astype(BF16)
        vb = (v3 * jnp.concatenate([beta_c, beta_c], axis=-1)).astype(BF16)
        kbg = (k3 * beta_exp_c).astype(BF16)
        u = _bdot(tb, vb)
        w = _bdot(tb, kbg)

        qt = q3 * exp_gc_c
        kd = (k3 * tail_c).astype(BF16)
        ub = u.astype(BF16)
        wb = w.astype(BF16)
        qkb = qk.astype(BF16)
        kdt = jnp.swapaxes(kd, 1, 2)
        a_neg = -_bdot(kdt, wb)
        b_add = _bdot(kdt, ub)
        q_eff = qt - _bdot(qkb, wb)
        o_loc = _bdot(qkb, ub)
        p_all = jnp.concatenate([a_neg, q_eff], axis=1).astype(BF16)
        return p_all, b_add, o_loc, e_last

    def state_steps(g, p_all, b_add, o_loc, e_last):
        ng = ng_ref[...]
        for cc in range(GROUP):
            state = s_ref[...]
            r = jnp.dot(p_all[cc], state.astype(BF16), preferred_element_type=F32)
            o = r[A_DK:, :] + o_loc[cc]
            s_ref[...] = state * e_last[cc:cc + 1, :] + r[:A_DK, :] + b_add[cc]
            on = o * lax.rsqrt(jnp.mean(o * o, axis=-1, keepdims=True) + EPS) * ng
            row = g * rows + cc * c
            zc = z_ref[pl.ds(row, c), :].astype(F32)
            o_ref[pl.ds(row, c), :] = (on * (zc * jax.nn.sigmoid(zc))).astype(o_ref.dtype)

    local = chunk_local(0)
    for g in range(n_groups):
        nxt = chunk_local(g + 1) if g + 1 < n_groups else None
        state_steps(g, *local)
        local = nxt


def _gdn_core(proj3, gates_t, norm_g):
    bsz, seq, _ = proj3.shape
    kb = A_QK // A_DK
    vb = 2 * A_QK // A_DV
    zb = vb + A_VW // A_DV
    return pl.pallas_call(
        _gdn_kernel,
        grid=(bsz, A_HEADS),
        in_specs=[
            pl.BlockSpec((None, seq, A_DK), lambda b, h: (b, 0, h)),
            pl.BlockSpec((None, seq, A_DK), lambda b, h: (b, 0, kb + h)),
            pl.BlockSpec((None, seq, A_DV), lambda b, h: (b, 0, vb + h)),
            pl.BlockSpec((None, seq, A_DV), lambda b, h: (b, 0, zb + h)),
            pl.BlockSpec((2 * A_HEADS, seq), lambda b, h: (0, b)),
            pl.BlockSpec((1, A_DV), lambda b, h: (0, 0)),
        ],
        out_specs=pl.BlockSpec((None, seq, A_DV), lambda b, h: (b, 0, h)),
        out_shape=jax.ShapeDtypeStruct((bsz, seq, A_VW), BF16),
        scratch_shapes=[
            pltpu.VMEM((A_DK, A_DV), F32),
            pltpu.VMEM((6, seq // (GROUP * CHUNK), GROUP, CHUNK), F32),
        ],
        compiler_params=pltpu.CompilerParams(
            dimension_semantics=("parallel", "parallel"), vmem_limit_bytes=VMEM_LIMIT_BYTES),
        name="gdn_core",
    )(proj3, proj3, proj3, proj3, gates_t, norm_g)


def _rope_table_kernel(pos_ref, invf_ref, cos_ref, sin_ref):
    ang = pos_ref[...].astype(F32) * invf_ref[...]
    cos_ref[...] = jnp.cos(ang)
    sin_ref[...] = jnp.sin(ang)


def _rope_tables(positions):
    t = positions.size
    per_row = B_DH // ROPE_HALF
    inv_freq = ROPE_THETA ** (-jnp.arange(0, ROPE_DIMS, 2, dtype=F32) / ROPE_DIMS)
    pos_rep = jnp.repeat(positions.reshape(-1), ROPE_HALF).reshape(t // per_row, B_DH)
    invf = jnp.tile(inv_freq, per_row)[None, :]
    rows = t // per_row
    tr = min(rows, 512)
    cos, sin = pl.pallas_call(
        _rope_table_kernel,
        grid=(rows // tr,),
        in_specs=[pl.BlockSpec((tr, B_DH), lambda i: (i, 0)), pl.BlockSpec((1, B_DH), lambda i: (0, 0))],
        out_specs=[pl.BlockSpec((tr, B_DH), lambda i: (i, 0))] * 2,
        out_shape=[jax.ShapeDtypeStruct((rows, B_DH), F32)] * 2,
        name="rope_tables",
    )(pos_rep, invf)
    c = cos.reshape(t, ROPE_HALF)
    s = sin.reshape(t, ROPE_HALF)
    gap = ROPE_SHIFT - ROPE_HALF
    ones = jnp.ones((t, gap), F32)
    zeros = jnp.zeros((t, gap), F32)
    cos_full = jnp.concatenate([c, ones, c, ones], axis=-1)
    sin_signed = jnp.concatenate([-s, zeros, s, zeros], axis=-1)
    return cos_full, sin_signed


def _attn_proj_kernel(x_ref, g_ref, w0_ref, w1_ref, w2_ref, cos_ref, sin_ref, qg_ref, kg_ref,
                      p0_ref, p1_ref, p2_ref, hn_ref):
    tm = x_ref.shape[0]
    n_lane_blocks = hn_ref.shape[0]
    hn_nat = _normed_f32(x_ref, g_ref)
    for j in range(n_lane_blocks):
        hn_ref[j] = hn_nat[:, j * B_DH:(j + 1) * B_DH]

    def by_stream(ref, dil):
        if dil == 1:
            return ref[...]
        return jnp.concatenate([ref[pl.ds(r, tm // dil, stride=dil), :] for r in range(dil)], axis=0)

    def hn_by_stream(dil):
        if dil == 1:
            return hn_nat
        return jnp.concatenate([by_stream(hn_ref.at[j], dil) for j in range(n_lane_blocks)], axis=-1)

    def norm_rope(y, gain, cos, sin):
        y = y * lax.rsqrt(jnp.mean(y * y, axis=-1, keepdims=True) + EPS) * gain
        return y * cos + pltpu.roll(y, ROPE_SHIFT, 1) * sin

    w_refs = (w0_ref, w1_ref, w2_ref)
    for gi in range(B_GROUPS):
        dil = B_DILATIONS[gi]
        hn = hn_by_stream(dil).astype(BF16)
        cos = by_stream(cos_ref, dil)
        sin = by_stream(sin_ref, dil)
        w_ref = w_refs[gi]
        for n0 in range(0, w_ref.shape[1], COL_TILE):
            y = jnp.dot(hn, w_ref[:, n0:n0 + COL_TILE], preferred_element_type=F32)
            if n0 < 2 * B_W:
                gain = (qg_ref if n0 < B_W else kg_ref)[gi:gi + 1, :]
                y = jnp.concatenate(
                    [norm_rope(y[:, j * B_DH:(j + 1) * B_DH], gain, cos, sin)
                     for j in range(COL_TILE // B_DH)], axis=-1)
            y = y.astype(BF16)
            if gi == 0:
                p0_ref[:, n0:n0 + COL_TILE] = y
            else:
                out_ref = p1_ref if gi == 1 else p2_ref
                out_ref[:, :, n0:n0 + COL_TILE] = y.reshape(dil, tm // dil, COL_TILE)


def _attn_proj(x2, g, w0, w1, w2, cos, sin, q_gain, k_gain, bsz, seq):
    t, d = x2.shape
    tm = ATTN_ROW_TILE
    per_seq = seq // tm
    d1, d2 = B_DILATIONS[1], B_DILATIONS[2]
    return pl.pallas_call(
        _attn_proj_kernel,
        grid=(t // tm,),
        in_specs=[
            pl.BlockSpec((tm, d), lambda i: (i, 0)),
            _resident((1, d)),
            _resident(w0.shape),
            _resident(w1.shape),
            _resident(w2.shape),
            pl.BlockSpec((tm, B_DH), lambda i: (i, 0)),
            pl.BlockSpec((tm, B_DH), lambda i: (i, 0)),
            _resident((B_GROUPS, B_DH)),
            _resident((B_GROUPS, B_DH)),
        ],
        out_specs=[
            pl.BlockSpec((tm, w0.shape[1]), lambda i: (i, 0)),
            pl.BlockSpec((None, d1, tm // d1, w1.shape[1]), lambda i: (i // per_seq, 0, i % per_seq, 0)),
            pl.BlockSpec((None, d2, tm // d2, w2.shape[1]), lambda i: (i // per_seq, 0, i % per_seq, 0)),
        ],
        out_shape=[
            jax.ShapeDtypeStruct((t, w0.shape[1]), BF16),
            jax.ShapeDtypeStruct((bsz, d1, seq // d1, w1.shape[1]), BF16),
            jax.ShapeDtypeStruct((bsz, d2, seq // d2, w2.shape[1]), BF16),
        ],
        scratch_shapes=[pltpu.VMEM((d // B_DH, tm, B_DH), F32)],
        compiler_params=pltpu.CompilerParams(
            dimension_semantics=("parallel",), vmem_limit_bytes=VMEM_LIMIT_BYTES),
        name="attn_norm_proj",
    )(x2, g, w0, w1, w2, cos, sin, q_gain, k_gain)


def _attn_kernel(q0_ref, k0_ref, v0_ref, q1_ref, k1_ref, v1_ref, q2_ref, k2_ref, v2_ref, z_ref,
                 o_ref, og_ref, lg_ref):
    seq = q0_ref.shape[0]
    blk = B_BLOCK
    nb = ATTN_BATCH
    tile = nb * blk
    assert all(w // d == blk for w, d in zip(B_WINDOWS, B_DILATIONS))

    qi = lax.broadcasted_iota(jnp.int32, (nb, blk, 2 * blk), 1)
    kj = lax.broadcasted_iota(jnp.int32, (nb, blk, 2 * blk), 2)
    bi = lax.broadcasted_iota(jnp.int32, (nb, blk, 2 * blk), 0)
    band = ((kj < blk) & (kj >= qi)) | ((kj >= blk) & (kj - blk <= qi))
    band_first = band & ((kj >= blk) | (bi > 0))
    cur_only = (lax.broadcasted_iota(jnp.int32, (nb, blk, blk), 2)
                <= lax.broadcasted_iota(jnp.int32, (nb, blk, blk), 1))
    ones_v = jnp.ones((nb, blk, B_DH), BF16)
    zero_blk = jnp.zeros((1, blk, B_DH), BF16)

    q_refs = (q0_ref, q1_ref, q2_ref)
    k_refs = (k0_ref, k1_ref, k2_ref)
    v_refs = (v0_ref, v1_ref, v2_ref)

    def blocks(ref, first, count):
        return ref[pl.ds(first, count * blk), :].reshape(count, blk, B_DH)

    def scores(gi, first, starts_stream, has_prev):
        qb = blocks(q_refs[gi], first, nb)
        kc = blocks(k_refs[gi], first, nb)
        if not has_prev:
            s = jnp.einsum("bqd,bkd->bqk", qb, kc, preferred_element_type=F32)
            return jnp.where(cur_only, s, -jnp.inf)
        if starts_stream:
            kp = jnp.concatenate([zero_blk, kc[:nb - 1]], axis=0)
        else:
            kp = blocks(k_refs[gi], first - blk, nb)
        s = jnp.einsum("bqd,bkd->bqk", qb, jnp.concatenate([kp, kc], axis=1),
                       preferred_element_type=F32)
        return jnp.where(band_first if starts_stream else band, s, -jnp.inf)

    def finish(gi, first, starts_stream, has_prev, out_rows, s):
        m = jnp.max(s, axis=-1, keepdims=True)
        p = jnp.exp(s - m).astype(BF16)
        vc = jnp.concatenate([blocks(v_refs[gi], first, nb), ones_v], axis=-1)
        if has_prev:
            if starts_stream:
                vp = jnp.concatenate([jnp.zeros((1, blk, 2 * B_DH), BF16), vc[:nb - 1]], axis=0)
            else:
                vp = jnp.concatenate([blocks(v_refs[gi], first - blk, nb), ones_v], axis=-1)
            vc = jnp.concatenate([vp, vc], axis=1)
        acc = jnp.einsum("bqk,bkd->bqd", p, vc, preferred_element_type=F32)
        den = acc[..., B_DH:]
        o = acc[..., :B_DH] / den
        lse = m + jnp.log(den)
        b0 = 0
        for rows, count in out_rows:
            og_ref[gi, rows, :] = o[b0:b0 + count].reshape(count * blk, B_DH)
            lg_ref[gi, rows, :] = lse[b0:b0 + count].reshape(count * blk, B_DH)
            b0 += count

    batches = []
    for gi in range(B_GROUPS):
        dil = B_DILATIONS[gi]
        length = seq // dil
        per_stream = length // blk
        for first in range(0, seq, tile):
            r, m0 = divmod(first, length)
            if per_stream >= nb:
                t0 = r + m0 * dil
                rows = pl.ds(t0, tile) if dil == 1 else pl.ds(t0, tile, stride=dil)
                batches.append((gi, first, m0 == 0, True, [(rows, nb)]))
            else:
                assert per_stream == 1
                rows = [(pl.ds(r + j, blk, stride=dil), 1) for j in range(nb)]
                batches.append((gi, first, True, False, rows))

    pending = {}
    for i in range(len(batches) + ATTN_LOOKAHEAD):
        if i < len(batches):
            pending[i] = scores(*batches[i][:4])
        if i >= ATTN_LOOKAHEAD:
            j = i - ATTN_LOOKAHEAD
            finish(*batches[j], pending.pop(j))

    def merge(t, carry):
        r0 = pl.multiple_of(t * tile, tile)
        l0 = lg_ref[0, pl.ds(r0, tile), :]
        l1 = lg_ref[1, pl.ds(r0, tile), :]
        l2 = lg_ref[2, pl.ds(r0, tile), :]
        m = jnp.maximum(jnp.maximum(l0, l1), l2)
        w0 = jnp.exp(l0 - m)
        w1 = jnp.exp(l1 - m)
        w2 = jnp.exp(l2 - m)
        num = (w0 * og_ref[0, pl.ds(r0, tile), :] + w1 * og_ref[1, pl.ds(r0, tile), :]
               + w2 * og_ref[2, pl.ds(r0, tile), :])
        o = num / (w0 + w1 + w2)
        z = z_ref[pl.ds(r0, tile), :].astype(F32)
        o_ref[pl.ds(r0, tile), :] = (o * (z * jax.nn.sigmoid(z))).astype(o_ref.dtype)
        return carry

    lax.fori_loop(0, seq // tile, merge, 0)


def _attn_core(p0, p1, p2):
    bsz, seq, _ = p0.shape

    def cols(which):
        return pl.BlockSpec((None, seq, B_DH), lambda b, h: (b, 0, which * B_HEADS + h))

    qkv = [cols(0), cols(1), cols(2)]
    return pl.pallas_call(
        _attn_kernel,
        grid=(bsz, B_HEADS),
        in_specs=[*qkv, *qkv, *qkv, cols(3)],
        out_specs=pl.BlockSpec((None, seq, B_DH), lambda b, h: (b, 0, h)),
        out_shape=jax.ShapeDtypeStruct((bsz, seq, B_W), BF16),
        scratch_shapes=[
            pltpu.VMEM((B_GROUPS, seq, B_DH), F32),
            pltpu.VMEM((B_GROUPS, seq, B_DH), F32),
        ],
        compiler_params=pltpu.CompilerParams(
            dimension_semantics=("parallel", "parallel"), vmem_limit_bytes=VMEM_LIMIT_BYTES),
        name="dilated_attn",
    )(p0, p0, p0, p1, p1, p1, p2, p2, p2, p0)


def _rotary_lane_order():
    first = list(range(ROPE_HALF))
    second = list(range(ROPE_HALF, ROPE_DIMS))
    rest = list(range(ROPE_DIMS, B_DH))
    gap = ROPE_SHIFT - ROPE_HALF
    return np.array(first + rest[:gap] + second + rest[gap:], dtype=np.int32)


def kernel(x, positions, norm_g, a_w_in, a_conv_w, a_log, a_dt_bias, a_norm_g, a_w_out,
           b_w_in, b_q_norm_g, b_k_norm_g, b_w_out):
    bsz, seq, d = x.shape
    t = bsz * seq
    x2 = x.reshape(t, d)

    w_main = a_w_in[0, :, :A_MAIN].astype(BF16)
    w_gate_t = a_w_in[0, :, A_MAIN:].T.astype(BF16)
    zeros8 = jnp.zeros((A_HEADS,), F32)
    alog16 = jnp.concatenate([zeros8, a_log[0].astype(F32)])[:, None]
    dtb16 = jnp.concatenate([zeros8, a_dt_bias[0].astype(F32)])[:, None]
    proj, gates_t = _gdn_proj(x2, norm_g[0][None, :], w_main, w_gate_t, alog16, dtb16,
                              a_conv_w[0], seq)
    o = _gdn_core(proj.reshape(bsz, seq, A_MAIN), gates_t, a_norm_g[0][None, :])
    x2 = _out_proj(o.reshape(t, A_VW), a_w_out[0].astype(BF16), x2, "gdn_out_proj")

    order = _rotary_lane_order()
    wb = b_w_in[0]
    n_qkv = 3 * B_GROUPS * B_W
    w5 = wb[:, :n_qkv].reshape(d, 3, B_GROUPS, B_HEADS, B_DH)
    w5 = jnp.concatenate([w5[:, :2][..., order], w5[:, 2:]], axis=1)
    w_groups = [w5[:, :, gi].reshape(d, 3 * B_W) for gi in range(B_GROUPS)]
    w0 = jnp.concatenate([w_groups[0], wb[:, n_qkv:]], axis=1).astype(BF16)
    w1 = w_groups[1].astype(BF16)
    w2 = w_groups[2].astype(BF16)
    q_gain = b_q_norm_g[0][:, order] * (B_DH ** -0.5)
    k_gain = b_k_norm_g[0][:, order]
    cos, sin = _rope_tables(positions)
    p0, p1, p2 = _attn_proj(x2, norm_g[1][None, :], w0, w1, w2, cos, sin, q_gain, k_gain, bsz, seq)
    o = _attn_core(p0.reshape(bsz, seq, -1), p1.reshape(bsz, seq, -1), p2.reshape(bsz, seq, -1))
    x2 = _out_proj(o.reshape(t, B_W), b_w_out[0].astype(BF16), x2, "attn_out_proj")
    return x2.reshape(bsz, seq, d)
```

```python
import jax
import jax.numpy as jnp
from jax import lax
from jax.experimental import pallas as pl
from jax.experimental.pallas import tpu as pltpu

F32 = jnp.float32
BF16 = jnp.bfloat16
EPS = 1e-6

D_MODEL = 1024
A_HEADS = 8
A_DK = 128
A_DV = 256
A_QK = A_HEADS * A_DK
A_VW = A_HEADS * A_DV
A_CONVW = 2 * A_QK + A_VW
A_MAIN = A_CONVW + A_VW
A_CONV = 4
CHUNK = 128
GROUP = 16
B_WINDOWS = (128, 512, 2048)
B_DILATIONS = (1, 4, 16)
B_GROUPS = 3
B_HEADS = 8
B_DH = 128
B_W = B_HEADS * B_DH
B_BLOCK = 128
ROPE_THETA = 500000.0
ROPE_DIMS = B_DH // 4
ROPE_HALF = ROPE_DIMS // 2
ROPE_SHIFT = B_DH // 2
ATTN_BATCH = 4
ATTN_LOOKAHEAD = 2

VMEM_LIMIT_BYTES = 56 * 1024 * 1024
ROW_TILE = 512
ATTN_ROW_TILE = 256
COL_TILE = 512
HALO = 16


def _split3(a):
    hi = a.astype(BF16)
    r = a - hi.astype(F32)
    mid = r.astype(BF16)
    lo = (r - mid.astype(F32)).astype(BF16)
    return hi, mid, lo


def _bdot(a, b):
    return jnp.einsum("cik,ckj->cij", a, b, preferred_element_type=F32)


def _normed_f32(x_ref, g_ref):
    x = x_ref[...]
    ms = jnp.mean(x * x, axis=-1, keepdims=True)
    return x * lax.rsqrt(ms + EPS) * g_ref[...]


def _resident(shape):
    return pl.BlockSpec(shape, lambda i: (0,) * len(shape), pipeline_mode=pl.Buffered(1))


def _gdn_proj_kernel(x_ref, xh_ref, g_ref, w_ref, wgt_ref, alog_ref, dtb_ref, cw_ref,
                     o_ref, gates_ref, *, tiles_per_seq):
    tm = x_ref.shape[0]
    first = (pl.program_id(0) % tiles_per_seq) == 0
    hn = _normed_f32(x_ref, g_ref).astype(BF16)
    halo = jnp.where(first, 0.0, _normed_f32(xh_ref, g_ref)).astype(BF16)
    hcat = jnp.concatenate([halo, hn], axis=0)

    def project(n0):
        lhs = hcat if n0 < A_CONVW else hn
        return jnp.dot(lhs, w_ref[:, n0:n0 + COL_TILE], preferred_element_type=F32)

    def epilogue(n0, y):
        if n0 < A_CONVW:
            cw = cw_ref[:, n0:n0 + COL_TILE]
            acc = y * cw[A_CONV - 1:A_CONV, :]
            for tap in range(A_CONV - 1):
                acc = acc + pltpu.roll(y, A_CONV - 1 - tap, 0) * cw[tap:tap + 1, :]
            y = acc[HALO:, :]
        o_ref[:, n0:n0 + COL_TILE] = y.astype(o_ref.dtype)

    starts = list(range(0, A_MAIN, COL_TILE))
    pending = project(starts[0])
    for idx, n0 in enumerate(starts):
        nxt = project(starts[idx + 1]) if idx + 1 < len(starts) else None
        epilogue(n0, pending)
        pending = nxt

    logits = lax.dot_general(wgt_ref[...], hn, (((1,), (1,)), ((), ())), preferred_element_type=F32)
    row = lax.broadcasted_iota(jnp.int32, logits.shape, 0)
    xs = logits + dtb_ref[...]
    softplus = jnp.maximum(xs, 0.0) + jnp.log1p(jnp.exp(-jnp.abs(xs)))
    decay = -jnp.exp(alog_ref[...]) * softplus
    gates_ref[...] = jnp.where(row < A_HEADS, jax.nn.sigmoid(logits), decay)


def _gdn_proj(x2, g, w, wgt, alog16, dtb16, conv_w, seq):
    t, d = x2.shape
    tm = ROW_TILE
    kern = lambda *refs: _gdn_proj_kernel(*refs, tiles_per_seq=seq // tm)
    return pl.pallas_call(
        kern,
        grid=(t // tm,),
        in_specs=[
            pl.BlockSpec((tm, d), lambda i: (i, 0)),
            pl.BlockSpec((HALO, d), lambda i: (jnp.maximum(i * (tm // HALO) - 1, 0), 0)),
            _resident((1, d)),
            _resident((d, A_MAIN)),
            _resident((2 * A_HEADS, d)),
            _resident((2 * A_HEADS, 1)),
            _resident((2 * A_HEADS, 1)),
            _resident((A_CONV, A_CONVW)),
        ],
        out_specs=[
            pl.BlockSpec((tm, A_MAIN), lambda i: (i, 0)),
            pl.BlockSpec((2 * A_HEADS, tm), lambda i: (0, i)),
        ],
        out_shape=[
            jax.ShapeDtypeStruct((t, A_MAIN), BF16),
            jax.ShapeDtypeStruct((2 * A_HEADS, t), F32),
        ],
        compiler_params=pltpu.CompilerParams(
            dimension_semantics=("parallel",), vmem_limit_bytes=VMEM_LIMIT_BYTES),
        name="gdn_norm_proj",
    )(x2, x2, g, w, wgt, alog16, dtb16, conv_w)


def _out_proj_kernel(o_ref, w_ref, x_ref, y_ref):
    y_ref[...] = x_ref[...] + jnp.dot(o_ref[...], w_ref[...], preferred_element_type=F32)


def _out_proj(o2, w, x2, name):
    t, k = o2.shape
    d = w.shape[1]
    tm = ROW_TILE
    return pl.pallas_call(
        _out_proj_kernel,
        grid=(t // tm,),
        in_specs=[
            pl.BlockSpec((tm, k), lambda i: (i, 0)),
            _resident((k, d)),
            pl.BlockSpec((tm, d), lambda i: (i, 0)),
        ],
        out_specs=pl.BlockSpec((tm, d), lambda i: (i, 0)),
        out_shape=jax.ShapeDtypeStruct((t, d), F32),
        compiler_params=pltpu.CompilerParams(
            dimension_semantics=("parallel",), vmem_limit_bytes=VMEM_LIMIT_BYTES),
        name=name,
    )(o2, w, x2)


def _gdn_kernel(q_ref, k_ref, v_ref, z_ref, gt_ref, ng_ref, o_ref, s_ref, rf_ref):
    head = pl.program_id(1)
    rows = GROUP * CHUNK
    n_groups = q_ref.shape[0] // rows
    c = CHUNK

    ii = lax.broadcasted_iota(jnp.int32, (c, c), 0)
    jj = lax.broadcasted_iota(jnp.int32, (c, c), 1)
    eye = jnp.where(ii == jj, 1.0, 0.0).astype(F32)
    lower_incl = ii >= jj
    lower_strict = ii > jj
    cum_rhs = jnp.where(ii <= jj, 1.0, 0.0).astype(BF16)

    def same_block(size):
        return (ii // size) == (jj // size)

    def col(x):
        return jnp.swapaxes(jnp.broadcast_to(x[:, None, :], (GROUP, c, c)), 1, 2)

    s_ref[...] = jnp.zeros_like(s_ref)

    n_chunks = q_ref.shape[0] // c

    def chunk_rows(r):
        full = gt_ref[pl.ds(r, 1), :]
        return jnp.concatenate([full[:, n * c:(n + 1) * c] for n in range(n_chunks)], axis=0)

    beta_all = chunk_rows(head)
    gh, gm, gl = _split3(chunk_rows(head + A_HEADS))
    gc_all = (jnp.dot(gh, cum_rhs, preferred_element_type=F32)
              + jnp.dot(gm, cum_rhs, preferred_element_type=F32)
              + jnp.dot(gl, cum_rhs, preferred_element_type=F32))
    exp_all = jnp.exp(gc_all)
    last_all = jnp.broadcast_to(gc_all[:, c - 1:c], gc_all.shape)
    rowforms = (beta_all, gc_all, exp_all, beta_all * exp_all, jnp.exp(last_all - gc_all),
                jnp.exp(last_all))
    for idx, val in enumerate(rowforms):
        for n in range(n_groups):
            rf_ref[idx, n] = val[n * GROUP:(n + 1) * GROUP, :]

    def chunk_local(g):
        r0 = g * rows
        def silu_rows(ref):
            a = ref[pl.ds(r0, rows), :].astype(F32)
            return a * jax.nn.sigmoid(a)

        def l2n(a):
            return a * lax.rsqrt(jnp.sum(a * a, axis=-1, keepdims=True) + EPS)

        q3 = (l2n(silu_rows(q_ref)) * (A_DK ** -0.5)).reshape(GROUP, c, A_DK)
        k3 = l2n(silu_rows(k_ref)).reshape(GROUP, c, A_DK)
        v3 = silu_rows(v_ref).reshape(GROUP, c, A_DV)
        q3b = q3.astype(BF16)
        k3b = k3.astype(BF16)

        gc = rf_ref[1, g]
        beta_c = col(rf_ref[0, g])
        gc_c = col(gc)
        exp_gc_c = col(rf_ref[2, g])
        beta_exp_c = col(rf_ref[3, g])
        tail_c = col(rf_ref[4, g])
        e_last = rf_ref[5, g]
        e_last = jnp.concatenate([e_last, e_last], axis=-1)
        decay = jnp.exp(jnp.where(lower_incl[None], gc_c - gc[:, None, :], -jnp.inf))

        kb3 = (k3 * beta_c).astype(BF16)
        kk = jnp.einsum("cid,cjd->cij", kb3, k3b, preferred_element_type=F32)
        qk = jnp.einsum("cid,cjd->cij", q3b, k3b, preferred_element_type=F32) * decay
        low = jnp.where(lower_strict[None], kk * decay, 0.0)

        m1 = jnp.where(same_block(8)[None], -low, 0.0)
        m1b = m1.astype(BF16)
        tinv = eye[None] + m1
        m2b = _bdot(m1b, m1b).astype(BF16)
        tinv = tinv + _bdot(tinv.astype(BF16), m2b)
        m4b = _bdot(m2b, m2b).astype(BF16)
        tinv = tinv + _bdot(tinv.astype(BF16), m4b)
        size = 8
        while size < c:
            off = jnp.where((same_block(2 * size) & jnp.logical_not(same_block(size)))[None], low, 0.0)
            tb = tinv.astype(BF16)
            tinv = tinv - _bdot(_bdot(tb, off.astype(BF16)).astype(BF16), tb)
            size *= 2

        tb = tinv.astype(BF16)
        vb = (v3 * jnp.concatenate([beta_c, beta_c], axis=-1)).astype(BF16)
        kbg = (k3 * beta_exp_c).astype(BF16)
        u = _bdot(tb, vb)
        w = _bdot(tb, kbg)

        qt = q3 * exp_gc_c
        kd = (k3 * tail_c).astype(BF16)
        ub = u.astype(BF16)
        wb = w.astype(BF16)
        qkb = qk.astype(BF16)
        kdt = jnp.swapaxes(kd, 1, 2)
        a_neg = -_bdot(kdt, wb)
        b_add = _bdot(kdt, ub)
        q_eff = qt - _bdot(qkb, wb)
        o_loc = _bdot(qkb, ub)
        p_all = jnp.concatenate([a_neg, q_eff], axis=1).astype(BF16)
        return p_all, b_add, o_loc, e_last

    def state_steps(g, p_all, b_add, o_loc, e_last):
        ng = ng_ref[...]
        for cc in range(GROUP):
            state = s_ref[...]
            r = jnp.dot(p_all[cc], state.astype(BF16), preferred_element_type=F32)
            o = r[A_DK:, :] + o_loc[cc]
            s_ref[...] = state * e_last[cc:cc + 1, :] + r[:A_DK, :] + b_add[cc]
            on = o * lax.rsqrt(jnp.mean(o * o, axis=-1, keepdims=True) + EPS) * ng
            row = g * rows + cc * c
            zc = z_ref[pl.ds(row, c), :].astype(F32)
            o_ref[pl.ds(row, c), :] = (on * (zc * jax.nn.sigmoid(zc))).astype(o_ref.dtype)

    local = chunk_local(0)
    for g in range(n_groups):
        nxt = chunk_local(g + 1) if g + 1 < n_groups else None
        state_steps(g, *local)
        local = nxt


def _gdn_core(proj3, gates_t, norm_g):
    bsz, seq, _ = proj3.shape
    kb = A_QK // A_DK
    vb = 2 * A_QK // A_DV
    zb = vb + A_VW // A_DV
    return pl.pallas_call(
        _gdn_kernel,
        grid=(bsz, A_HEADS),
        in_specs=[
            pl.BlockSpec((None, seq, A_DK), lambda b, h: (b, 0, h)),
            pl.BlockSpec((None, seq, A_DK), lambda b, h: (b, 0, kb + h)),
            pl.BlockSpec((None, seq, A_DV), lambda b, h: (b, 0, vb + h)),
            pl.BlockSpec((None, seq, A_DV), lambda b, h: (b, 0, zb + h)),
            pl.BlockSpec((2 * A_HEADS, seq), lambda b, h: (0, b)),
            pl.BlockSpec((1, A_DV), lambda b, h: (0, 0)),
        ],
        out_specs=pl.BlockSpec((None, seq, A_DV), lambda b, h: (b, 0, h)),
        out_shape=jax.ShapeDtypeStruct((bsz, seq, A_VW), BF16),
        scratch_shapes=[
            pltpu.VMEM((A_DK, A_DV), F32),
            pltpu.VMEM((6, seq // (GROUP * CHUNK), GROUP, CHUNK), F32),
        ],
        compiler_params=pltpu.CompilerParams(
            dimension_semantics=("parallel", "parallel"), vmem_limit_bytes=VMEM_LIMIT_BYTES),
        name="gdn_core",
    )(proj3, proj3, proj3, proj3, gates_t, norm_g)


def _rope_table_kernel(pos_ref, invf_ref, cos_ref, sin_ref):
    tr, per_row = pos_ref.shape
    nf = ROPE_HALF

    def dot_split(a, b, terms):
        parts = _split3(a)[:terms]
        out = jnp.dot(parts[0], b, preferred_element_type=F32)
        for part in parts[1:]:
            out = out + jnp.dot(part, b, preferred_element_type=F32)
        return out

    rk = lax.broadcasted_iota(jnp.int32, (per_row, B_DH), 0)
    rl = lax.broadcasted_iota(jnp.int32, (per_row, B_DH), 1)
    rep = jnp.where(rl // nf == rk, 1.0, 0.0).astype(BF16)
    ang = dot_split(pos_ref[...].astype(F32), rep, 3) * invf_ref[...]
    c = jnp.cos(ang)
    s = jnp.sin(ang)

    ek = lax.broadcasted_iota(jnp.int32, (B_DH, B_DH), 0)
    el = lax.broadcasted_iota(jnp.int32, (B_DH, B_DH), 1)
    lane = lax.broadcasted_iota(jnp.int32, (1, B_DH), 1)
    rotary = (lane < nf) | ((lane >= ROPE_SHIFT) & (lane < ROPE_SHIFT + nf))
    base = jnp.where(rotary, 0.0, 1.0)
    for p in range(per_row):
        mine = ek // nf == p
        first = mine & (el == ek % nf)
        second = mine & (el == ek % nf + ROPE_SHIFT)
        e_cos = jnp.where(first | second, 1.0, 0.0).astype(BF16)
        e_sin = (jnp.where(second, 1.0, 0.0) - jnp.where(first, 1.0, 0.0)).astype(BF16)
        rows = pl.ds(p, tr, stride=per_row)
        cos_ref[rows, :] = dot_split(c, e_cos, 2) + base
        sin_ref[rows, :] = dot_split(s, e_sin, 2)


def _rope_tables(positions):
    t = positions.size
    per_row = B_DH // ROPE_HALF
    inv_freq = ROPE_THETA ** (-jnp.arange(0, ROPE_DIMS, 2, dtype=F32) / ROPE_DIMS)
    invf = jnp.tile(inv_freq, per_row)[None, :]
    rows = t // per_row
    tr = min(rows, 512)
    return pl.pallas_call(
        _rope_table_kernel,
        grid=(rows // tr,),
        in_specs=[pl.BlockSpec((tr, per_row), lambda i: (i, 0)), pl.BlockSpec((1, B_DH), lambda i: (0, 0))],
        out_specs=[pl.BlockSpec((tr * per_row, B_DH), lambda i: (i, 0))] * 2,
        out_shape=[jax.ShapeDtypeStruct((t, B_DH), F32)] * 2,
        name="rope_tables",
    )(positions.reshape(rows, per_row), invf)


def _attn_proj_kernel(x_ref, g_ref, w0_ref, w1_ref, w2_ref, cos_ref, sin_ref, qg_ref, kg_ref,
                      p0_ref, p1_ref, p2_ref, hn_ref):
    tm = x_ref.shape[0]
    n_lane_blocks = hn_ref.shape[0]
    hn_nat = _normed_f32(x_ref, g_ref)
    for j in range(n_lane_blocks):
        hn_ref[j] = hn_nat[:, j * B_DH:(j + 1) * B_DH]

    def by_stream(ref, dil):
        if dil == 1:
            return ref[...]
        return jnp.concatenate([ref[pl.ds(r, tm // dil, stride=dil), :] for r in range(dil)], axis=0)

    def hn_by_stream(dil):
        if dil == 1:
            return hn_nat
        return jnp.concatenate([by_stream(hn_ref.at[j], dil) for j in range(n_lane_blocks)], axis=-1)

    def norm_rope(y, gain, cos, sin):
        y = y * lax.rsqrt(jnp.mean(y * y, axis=-1, keepdims=True) + EPS) * gain
        return y * cos + pltpu.roll(y, ROPE_SHIFT, 1) * sin

    w_refs = (w0_ref, w1_ref, w2_ref)
    lhs = [hn_by_stream(dil).astype(BF16) for dil in B_DILATIONS]
    tables = [(by_stream(cos_ref, dil), by_stream(sin_ref, dil)) for dil in B_DILATIONS]

    def project(gi, n0):
        return jnp.dot(lhs[gi], w_refs[gi][:, n0:n0 + COL_TILE], preferred_element_type=F32)

    def epilogue(gi, n0, y):
        dil = B_DILATIONS[gi]
        if n0 < 2 * B_W:
            gain = (qg_ref if n0 < B_W else kg_ref)[gi:gi + 1, :]
            cos, sin = tables[gi]
            y = jnp.concatenate(
                [norm_rope(y[:, j * B_DH:(j + 1) * B_DH], gain, cos, sin)
                 for j in range(COL_TILE // B_DH)], axis=-1)
        y = y.astype(BF16)
        if gi == 0:
            p0_ref[:, n0:n0 + COL_TILE] = y
        else:
            out_ref = p1_ref if gi == 1 else p2_ref
            out_ref[:, :, n0:n0 + COL_TILE] = y.reshape(dil, tm // dil, COL_TILE)

    tiles = [(gi, n0) for gi in range(B_GROUPS) for n0 in range(0, w_refs[gi].shape[1], COL_TILE)]
    pending = project(*tiles[0])
    for idx, tile_id in enumerate(tiles):
        nxt = project(*tiles[idx + 1]) if idx + 1 < len(tiles) else None
        epilogue(*tile_id, pending)
        pending = nxt


def _attn_proj(x2, g, w0, w1, w2, cos, sin, q_gain, k_gain, bsz, seq):
    t, d = x2.shape
    tm = ATTN_ROW_TILE
    per_seq = seq // tm
    d1, d2 = B_DILATIONS[1], B_DILATIONS[2]
    return pl.pallas_call(
        _attn_proj_kernel,
        grid=(t // tm,),
        in_specs=[
            pl.BlockSpec((tm, d), lambda i: (i, 0)),
            _resident((1, d)),
            _resident(w0.shape),
            _resident(w1.shape),
            _resident(w2.shape),
            pl.BlockSpec((tm, B_DH), lambda i: (i, 0)),
            pl.BlockSpec((tm, B_DH), lambda i: (i, 0)),
            _resident((B_GROUPS, B_DH)),
            _resident((B_GROUPS, B_DH)),
        ],
        out_specs=[
            pl.BlockSpec((tm, w0.shape[1]), lambda i: (i, 0)),
            pl.BlockSpec((None, d1, tm // d1, w1.shape[1]), lambda i: (i // per_seq, 0, i % per_seq, 0)),
            pl.BlockSpec((None, d2, tm // d2, w2.shape[1]), lambda i: (i // per_seq, 0, i % per_seq, 0)),
        ],
        out_shape=[
            jax.ShapeDtypeStruct((t, w0.shape[1]), BF16),
            jax.ShapeDtypeStruct((bsz, d1, seq // d1, w1.shape[1]), BF16),
            jax.ShapeDtypeStruct((bsz, d2, seq // d2, w2.shape[1]), BF16),
        ],
        scratch_shapes=[pltpu.VMEM((d // B_DH, tm, B_DH), F32)],
        compiler_params=pltpu.CompilerParams(
            dimension_semantics=("parallel",), vmem_limit_bytes=VMEM_LIMIT_BYTES),
        name="attn_norm_proj",
    )(x2, g, w0, w1, w2, cos, sin, q_gain, k_gain)


def _attn_kernel(q0_ref, k0_ref, v0_ref, q1_ref, k1_ref, v1_ref, q2_ref, k2_ref, v2_ref, z_ref,
                 o_ref, og_ref, lg_ref):
    seq = q0_ref.shape[0]
    blk = B_BLOCK
    nb = ATTN_BATCH
    tile = nb * blk
    assert all(w // d == blk for w, d in zip(B_WINDOWS, B_DILATIONS))

    qi = lax.broadcasted_iota(jnp.int32, (nb, blk, 2 * blk), 1)
    kj = lax.broadcasted_iota(jnp.int32, (nb, blk, 2 * blk), 2)
    bi = lax.broadcasted_iota(jnp.int32, (nb, blk, 2 * blk), 0)
    band = ((kj < blk) & (kj >= qi)) | ((kj >= blk) & (kj - blk <= qi))
    band_first = band & ((kj >= blk) | (bi > 0))
    cur_only = (lax.broadcasted_iota(jnp.int32, (nb, blk, blk), 2)
                <= lax.broadcasted_iota(jnp.int32, (nb, blk, blk), 1))
    ones_v = jnp.ones((nb, blk, B_DH), BF16)
    zero_blk = jnp.zeros((1, blk, B_DH), BF16)

    q_refs = (q0_ref, q1_ref, q2_ref)
    k_refs = (k0_ref, k1_ref, k2_ref)
    v_refs = (v0_ref, v1_ref, v2_ref)

    def blocks(ref, first, count):
        return ref[pl.ds(first, count * blk), :].reshape(count, blk, B_DH)

    def scores(gi, first, starts_stream, has_prev):
        qb = blocks(q_refs[gi], first, nb)
        kc = blocks(k_refs[gi], first, nb)
        if not has_prev:
            s = jnp.einsum("bqd,bkd->bqk", qb, kc, preferred_element_type=F32)
            return jnp.where(cur_only, s, -jnp.inf)
        if starts_stream:
            kp = jnp.concatenate([zero_blk, kc[:nb - 1]], axis=0)
        else:
            kp = blocks(k_refs[gi], first - blk, nb)
        s = jnp.einsum("bqd,bkd->bqk", qb, jnp.concatenate([kp, kc], axis=1),
                       preferred_element_type=F32)
        return jnp.where(band_first if starts_stream else band, s, -jnp.inf)

    def finish(gi, first, starts_stream, has_prev, out_rows, s):
        m = jnp.max(s, axis=-1, keepdims=True)
        p = jnp.exp(s - m).astype(BF16)
        vc = jnp.concatenate([blocks(v_refs[gi], first, nb), ones_v], axis=-1)
        if has_prev:
            if starts_stream:
                vp = jnp.concatenate([jnp.zeros((1, blk, 2 * B_DH), BF16), vc[:nb - 1]], axis=0)
            else:
                vp = jnp.concatenate([blocks(v_refs[gi], first - blk, nb), ones_v], axis=-1)
            vc = jnp.concatenate([vp, vc], axis=1)
        acc = jnp.einsum("bqk,bkd->bqd", p, vc, preferred_element_type=F32)
        den = acc[..., B_DH:]
        o = acc[..., :B_DH] / den
        lse = m + jnp.log(den)
        b0 = 0
        for rows, count in out_rows:
            og_ref[gi, rows, :] = o[b0:b0 + count].reshape(count * blk, B_DH)
            lg_ref[gi, rows, :] = lse[b0:b0 + count].reshape(count * blk, B_DH)
            b0 += count

    batches = []
    for gi in range(B_GROUPS):
        dil = B_DILATIONS[gi]
        length = seq // dil
        per_stream = length // blk
        for first in range(0, seq, tile):
            r, m0 = divmod(first, length)
            if per_stream >= nb:
                t0 = r + m0 * dil
                rows = pl.ds(t0, tile) if dil == 1 else pl.ds(t0, tile, stride=dil)
                batches.append((gi, first, m0 == 0, True, [(rows, nb)]))
            else:
                assert per_stream == 1
                rows = [(pl.ds(r + j, blk, stride=dil), 1) for j in range(nb)]
                batches.append((gi, first, True, False, rows))

    pending = {}
    for i in range(len(batches) + ATTN_LOOKAHEAD):
        if i < len(batches):
            pending[i] = scores(*batches[i][:4])
        if i >= ATTN_LOOKAHEAD:
            j = i - ATTN_LOOKAHEAD
            finish(*batches[j], pending.pop(j))

    def merge(t, carry):
        r0 = pl.multiple_of(t * tile, tile)
        l0 = lg_ref[0, pl.ds(r0, tile), :]
        l1 = lg_ref[1, pl.ds(r0, tile), :]
        l2 = lg_ref[2, pl.ds(r0, tile), :]
        m = jnp.maximum(jnp.maximum(l0, l1), l2)
        w0 = jnp.exp(l0 - m)
        w1 = jnp.exp(l1 - m)
        w2 = jnp.exp(l2 - m)
        num = (w0 * og_ref[0, pl.ds(r0, tile), :] + w1 * og_ref[1, pl.ds(r0, tile), :]
               + w2 * og_ref[2, pl.ds(r0, tile), :])
        o = num / (w0 + w1 + w2)
        z = z_ref[pl.ds(r0, tile), :].astype(F32)
        o_ref[pl.ds(r0, tile), :] = (o * (z * jax.nn.sigmoid(z))).astype(o_ref.dtype)
        return carry

    lax.fori_loop(0, seq // tile, merge, 0)


def _attn_core(p0, p1, p2):
    bsz, seq, _ = p0.shape

    def cols(which):
        return pl.BlockSpec((None, seq, B_DH), lambda b, h: (b, 0, which * B_HEADS + h))

    qkv = [cols(0), cols(1), cols(2)]
    return pl.pallas_call(
        _attn_kernel,
        grid=(bsz, B_HEADS),
        in_specs=[*qkv, *qkv, *qkv, cols(3)],
        out_specs=pl.BlockSpec((None, seq, B_DH), lambda b, h: (b, 0, h)),
        out_shape=jax.ShapeDtypeStruct((bsz, seq, B_W), BF16),
        scratch_shapes=[
            pltpu.VMEM((B_GROUPS, seq, B_DH), F32),
            pltpu.VMEM((B_GROUPS, seq, B_DH), F32),
        ],
        compiler_params=pltpu.CompilerParams(
            dimension_semantics=("parallel", "parallel"), vmem_limit_bytes=VMEM_LIMIT_BYTES),
        name="dilated_attn",
    )(p0, p0, p0, p1, p1, p1, p2, p2, p2, p0)


def _rotary_lane_order(a):
    gap = ROPE_SHIFT - ROPE_HALF
    return jnp.concatenate(
        [a[..., :ROPE_HALF], a[..., ROPE_DIMS:ROPE_DIMS + gap], a[..., ROPE_HALF:ROPE_DIMS],
         a[..., ROPE_DIMS + gap:]], axis=-1)


def kernel(x, positions, norm_g, a_w_in, a_conv_w, a_log, a_dt_bias, a_norm_g, a_w_out,
           b_w_in, b_q_norm_g, b_k_norm_g, b_w_out):
    bsz, seq, d = x.shape
    t = bsz * seq
    x2 = x.reshape(t, d)

    w_main = a_w_in[0, :, :A_MAIN].astype(BF16)
    w_gate_t = a_w_in[0, :, A_MAIN:].T.astype(BF16)
    zeros8 = jnp.zeros((A_HEADS,), F32)
    alog16 = jnp.concatenate([zeros8, a_log[0].astype(F32)])[:, None]
    dtb16 = jnp.concatenate([zeros8, a_dt_bias[0].astype(F32)])[:, None]
    proj, gates_t = _gdn_proj(x2, norm_g[0][None, :], w_main, w_gate_t, alog16, dtb16,
                              a_conv_w[0], seq)
    o = _gdn_core(proj.reshape(bsz, seq, A_MAIN), gates_t, a_norm_g[0][None, :])
    x2 = _out_proj(o.reshape(t, A_VW), a_w_out[0].astype(BF16), x2, "gdn_out_proj")

    wb = b_w_in[0]
    n_qkv = 3 * B_GROUPS * B_W
    w5 = wb[:, :n_qkv].reshape(d, 3, B_GROUPS, B_HEADS, B_DH)

    def group_weights(gi):
        parts = [_rotary_lane_order(w5[:, 0, gi]), _rotary_lane_order(w5[:, 1, gi]), w5[:, 2, gi]]
        return jnp.concatenate([p.reshape(d, B_W) for p in parts], axis=1)

    w0 = jnp.concatenate([group_weights(0), wb[:, n_qkv:]], axis=1).astype(BF16)
    w1 = group_weights(1).astype(BF16)
    w2 = group_weights(2).astype(BF16)
    q_gain = _rotary_lane_order(b_q_norm_g[0]) * (B_DH ** -0.5)
    k_gain = _rotary_lane_order(b_k_norm_g[0])
    cos, sin = _rope_tables(positions)
    p0, p1, p2 = _attn_proj(x2, norm_g[1][None, :], w0, w1, w2, cos, sin, q_gain, k_gain, bsz, seq)
    o = _attn_core(p0.reshape(bsz, seq, -1), p1.reshape(bsz, seq, -1), p2.reshape(bsz, seq, -1))
    x2 = _out_proj(o.reshape(t, B_W), b_w_out[0].astype(BF16), x2, "attn_out_proj")
    return x2.reshape(bsz, seq, d)
```

```python
import jax
import jax.numpy as jnp
from jax import lax
from jax.experimental import pallas as pl
from jax.experimental.pallas import tpu as pltpu

F32 = jnp.float32
BF16 = jnp.bfloat16
EPS = 1e-6

D_MODEL = 1024
A_HEADS = 8
A_DK = 128
A_DV = 256
A_QK = A_HEADS * A_DK
A_VW = A_HEADS * A_DV
A_CONVW = 2 * A_QK + A_VW
A_MAIN = A_CONVW + A_VW
A_CONV = 4
CHUNK = 128
GROUP = 16
OUT_BATCH = 4
B_WINDOWS = (128, 512, 2048)
B_DILATIONS = (1, 4, 16)
B_GROUPS = 3
B_HEADS = 8
B_DH = 128
B_W = B_HEADS * B_DH
B_BLOCK = 128
ROPE_THETA = 500000.0
ROPE_DIMS = B_DH // 4
ROPE_HALF = ROPE_DIMS // 2
ROPE_SHIFT = B_DH // 2
ATTN_BATCH = 4
ATTN_LOOKAHEAD = 2

VMEM_LIMIT_BYTES = 56 * 1024 * 1024
ROW_TILE = 512
ATTN_ROW_TILE = 256
COL_TILE = 512
HALO = 16


def _split3(a):
    hi = a.astype(BF16)
    r = a - hi.astype(F32)
    mid = r.astype(BF16)
    lo = (r - mid.astype(F32)).astype(BF16)
    return hi, mid, lo


def _bdot(a, b):
    return jnp.einsum("cik,ckj->cij", a, b, preferred_element_type=F32)


def _normed_f32(x_ref, g_ref):
    x = x_ref[...]
    ms = jnp.mean(x * x, axis=-1, keepdims=True)
    return x * lax.rsqrt(ms + EPS) * g_ref[...]


def _resident(shape):
    return pl.BlockSpec(shape, lambda i: (0,) * len(shape), pipeline_mode=pl.Buffered(1))


def _gdn_proj_kernel(x_ref, xh_ref, g_ref, w_ref, wgt_ref, alog_ref, dtb_ref, cw_ref,
                     o_ref, gates_ref, *, tiles_per_seq):
    tm = x_ref.shape[0]
    first = (pl.program_id(0) % tiles_per_seq) == 0
    hn = _normed_f32(x_ref, g_ref).astype(BF16)
    halo = jnp.where(first, 0.0, _normed_f32(xh_ref, g_ref)).astype(BF16)
    hcat = jnp.concatenate([halo, hn], axis=0)

    def project(n0):
        lhs = hcat if n0 < A_CONVW else hn
        return jnp.dot(lhs, w_ref[:, n0:n0 + COL_TILE], preferred_element_type=F32)

    def epilogue(n0, y):
        if n0 < A_CONVW:
            cw = cw_ref[:, n0:n0 + COL_TILE]
            acc = y * cw[A_CONV - 1:A_CONV, :]
            for tap in range(A_CONV - 1):
                acc = acc + pltpu.roll(y, A_CONV - 1 - tap, 0) * cw[tap:tap + 1, :]
            y = acc[HALO:, :]
        o_ref[:, n0:n0 + COL_TILE] = y.astype(o_ref.dtype)

    starts = list(range(0, A_MAIN, COL_TILE))
    pending = project(starts[0])
    for idx, n0 in enumerate(starts):
        nxt = project(starts[idx + 1]) if idx + 1 < len(starts) else None
        epilogue(n0, pending)
        pending = nxt

    logits = lax.dot_general(wgt_ref[...], hn, (((1,), (1,)), ((), ())), preferred_element_type=F32)
    row = lax.broadcasted_iota(jnp.int32, logits.shape, 0)
    xs = logits + dtb_ref[...]
    softplus = jnp.maximum(xs, 0.0) + jnp.log1p(jnp.exp(-jnp.abs(xs)))
    decay = -jnp.exp(alog_ref[...]) * softplus
    gates_ref[...] = jnp.where(row < A_HEADS, jax.nn.sigmoid(logits), decay)


def _gdn_proj(x2, g, w, wgt, alog16, dtb16, conv_w, seq):
    t, d = x2.shape
    tm = ROW_TILE
    kern = lambda *refs: _gdn_proj_kernel(*refs, tiles_per_seq=seq // tm)
    return pl.pallas_call(
        kern,
        grid=(t // tm,),
        in_specs=[
            pl.BlockSpec((tm, d), lambda i: (i, 0)),
            pl.BlockSpec((HALO, d), lambda i: (jnp.maximum(i * (tm // HALO) - 1, 0), 0)),
            _resident((1, d)),
            _resident((d, A_MAIN)),
            _resident((2 * A_HEADS, d)),
            _resident((2 * A_HEADS, 1)),
            _resident((2 * A_HEADS, 1)),
            _resident((A_CONV, A_CONVW)),
        ],
        out_specs=[
            pl.BlockSpec((tm, A_MAIN), lambda i: (i, 0)),
            pl.BlockSpec((2 * A_HEADS, tm), lambda i: (0, i)),
        ],
        out_shape=[
            jax.ShapeDtypeStruct((t, A_MAIN), BF16),
            jax.ShapeDtypeStruct((2 * A_HEADS, t), F32),
        ],
        compiler_params=pltpu.CompilerParams(
            dimension_semantics=("parallel",), vmem_limit_bytes=VMEM_LIMIT_BYTES),
        name="gdn_norm_proj",
    )(x2, x2, g, w, wgt, alog16, dtb16, conv_w)


def _out_proj_kernel(o_ref, w_ref, x_ref, y_ref):
    y_ref[...] = x_ref[...] + jnp.dot(o_ref[...], w_ref[...], preferred_element_type=F32)


def _out_proj(o2, w, x2, name):
    t, k = o2.shape
    d = w.shape[1]
    tm = ROW_TILE
    return pl.pallas_call(
        _out_proj_kernel,
        grid=(t // tm,),
        in_specs=[
            pl.BlockSpec((tm, k), lambda i: (i, 0)),
            _resident((k, d)),
            pl.BlockSpec((tm, d), lambda i: (i, 0)),
        ],
        out_specs=pl.BlockSpec((tm, d), lambda i: (i, 0)),
        out_shape=jax.ShapeDtypeStruct((t, d), F32),
        compiler_params=pltpu.CompilerParams(
            dimension_semantics=("parallel",), vmem_limit_bytes=VMEM_LIMIT_BYTES),
        name=name,
    )(o2, w, x2)


def _gdn_kernel(q_ref, k_ref, v_ref, z_ref, gt_ref, ng_ref, wo_ref, x_ref, y_ref, s_ref, rf_ref):
    head = pl.program_id(1)
    rows = GROUP * CHUNK
    n_groups = q_ref.shape[0] // rows
    c = CHUNK

    ii = lax.broadcasted_iota(jnp.int32, (c, c), 0)
    jj = lax.broadcasted_iota(jnp.int32, (c, c), 1)
    eye = jnp.where(ii == jj, 1.0, 0.0).astype(F32)
    lower_incl = ii >= jj
    lower_strict = ii > jj
    cum_rhs = jnp.where(ii <= jj, 1.0, 0.0).astype(BF16)

    def same_block(size):
        return (ii // size) == (jj // size)

    def col(x):
        return jnp.swapaxes(jnp.broadcast_to(x[:, None, :], (GROUP, c, c)), 1, 2)

    s_ref[...] = jnp.zeros_like(s_ref)

    n_chunks = q_ref.shape[0] // c

    def chunk_rows(r):
        full = gt_ref[pl.ds(r, 1), :]
        return jnp.concatenate([full[:, n * c:(n + 1) * c] for n in range(n_chunks)], axis=0)

    beta_all = chunk_rows(head)
    gh, gm, gl = _split3(chunk_rows(head + A_HEADS))
    gc_all = (jnp.dot(gh, cum_rhs, preferred_element_type=F32)
              + jnp.dot(gm, cum_rhs, preferred_element_type=F32)
              + jnp.dot(gl, cum_rhs, preferred_element_type=F32))
    exp_all = jnp.exp(gc_all)
    last_all = jnp.broadcast_to(gc_all[:, c - 1:c], gc_all.shape)
    rowforms = (beta_all, gc_all, exp_all, beta_all * exp_all, jnp.exp(last_all - gc_all),
                jnp.exp(last_all))
    for idx, val in enumerate(rowforms):
        for n in range(n_groups):
            rf_ref[idx, n] = val[n * GROUP:(n + 1) * GROUP, :]

    def chunk_local(g):
        r0 = g * rows
        def silu_rows(ref):
            a = ref[pl.ds(r0, rows), :].astype(F32)
            return a * jax.nn.sigmoid(a)

        def l2n(a):
            return a * lax.rsqrt(jnp.sum(a * a, axis=-1, keepdims=True) + EPS)

        q3 = (l2n(silu_rows(q_ref)) * (A_DK ** -0.5)).reshape(GROUP, c, A_DK)
        k3 = l2n(silu_rows(k_ref)).reshape(GROUP, c, A_DK)
        v3 = silu_rows(v_ref).reshape(GROUP, c, A_DV)
        q3b = q3.astype(BF16)
        k3b = k3.astype(BF16)

        gc = rf_ref[1, g]
        beta_c = col(rf_ref[0, g])
        gc_c = col(gc)
        exp_gc_c = col(rf_ref[2, g])
        beta_exp_c = col(rf_ref[3, g])
        tail_c = col(rf_ref[4, g])
        e_last = rf_ref[5, g]
        e_last = jnp.concatenate([e_last, e_last], axis=-1)
        decay = jnp.exp(jnp.where(lower_incl[None], gc_c - gc[:, None, :], -jnp.inf))

        kb3 = (k3 * beta_c).astype(BF16)
        kk = jnp.einsum("cid,cjd->cij", kb3, k3b, preferred_element_type=F32)
        qk = jnp.einsum("cid,cjd->cij", q3b, k3b, preferred_element_type=F32) * decay
        low = jnp.where(lower_strict[None], kk * decay, 0.0)

        m1 = jnp.where(same_block(8)[None], -low, 0.0)
        m1b = m1.astype(BF16)
        tinv = eye[None] + m1
        m2b = _bdot(m1b, m1b).astype(BF16)
        tinv = tinv + _bdot(tinv.astype(BF16), m2b)
        m4b = _bdot(m2b, m2b).astype(BF16)
        tinv = tinv + _bdot(tinv.astype(BF16), m4b)
        size = 8
        while size < c:
            off = jnp.where((same_block(2 * size) & jnp.logical_not(same_block(size)))[None], low, 0.0)
            tb = tinv.astype(BF16)
            tinv = tinv - _bdot(_bdot(tb, off.astype(BF16)).astype(BF16), tb)
            size *= 2

        tb = tinv.astype(BF16)
        vb = (v3 * jnp.concatenate([beta_c, beta_c], axis=-1)).astype(BF16)
        kbg = (k3 * beta_exp_c).astype(BF16)
        u = _bdot(tb, vb)
        w = _bdot(tb, kbg)

        qt = q3 * exp_gc_c
        kd = (k3 * tail_c).astype(BF16)
        ub = u.astype(BF16)
        wb = w.astype(BF16)
        qkb = qk.astype(BF16)
        kdt = jnp.swapaxes(kd, 1, 2)
        a_neg = -_bdot(kdt, wb)
        b_add = _bdot(kdt, ub)
        q_eff = qt - _bdot(qkb, wb)
        o_loc = _bdot(qkb, ub)
        p_all = jnp.concatenate([a_neg, q_eff], axis=1).astype(BF16)
        return p_all, b_add, o_loc, e_last

    def project_out(row, gated):
        lhs = jnp.concatenate(gated, axis=0)
        y_ref[pl.ds(row, lhs.shape[0]), :] += jnp.dot(lhs, wo_ref[...], preferred_element_type=F32)

    def state_steps(g, p_all, b_add, o_loc, e_last):
        ng = ng_ref[...]
        ready = None
        gated = []
        for cc in range(GROUP):
            state = s_ref[...]
            r = jnp.dot(p_all[cc], state.astype(BF16), preferred_element_type=F32)
            o = r[A_DK:, :] + o_loc[cc]
            s_ref[...] = state * e_last[cc:cc + 1, :] + r[:A_DK, :] + b_add[cc]
            if ready is not None:
                project_out(*ready)
                ready = None
            on = o * lax.rsqrt(jnp.mean(o * o, axis=-1, keepdims=True) + EPS) * ng
            row = g * rows + cc * c
            zc = z_ref[pl.ds(row, c), :].astype(F32)
            gated.append((on * (zc * jax.nn.sigmoid(zc))).astype(BF16))
            if len(gated) == OUT_BATCH:
                ready = (row - (OUT_BATCH - 1) * c, gated)
                gated = []
        project_out(*ready)

    @pl.when(head == 0)
    def _():
        y_ref[...] = x_ref[...]

    local = chunk_local(0)
    for g in range(n_groups):
        nxt = chunk_local(g + 1) if g + 1 < n_groups else None
        state_steps(g, *local)
        local = nxt


def _gdn_core(proj3, gates_t, norm_g, w_out, x3):
    bsz, seq, _ = proj3.shape
    d = x3.shape[-1]
    kb = A_QK // A_DK
    vb = 2 * A_QK // A_DV
    zb = vb + A_VW // A_DV
    return pl.pallas_call(
        _gdn_kernel,
        grid=(bsz, A_HEADS),
        in_specs=[
            pl.BlockSpec((None, seq, A_DK), lambda b, h: (b, 0, h)),
            pl.BlockSpec((None, seq, A_DK), lambda b, h: (b, 0, kb + h)),
            pl.BlockSpec((None, seq, A_DV), lambda b, h: (b, 0, vb + h)),
            pl.BlockSpec((None, seq, A_DV), lambda b, h: (b, 0, zb + h)),
            pl.BlockSpec((2 * A_HEADS, seq), lambda b, h: (0, b)),
            pl.BlockSpec((1, A_DV), lambda b, h: (0, 0)),
            pl.BlockSpec((A_DV, d), lambda b, h: (h, 0)),
            pl.BlockSpec((None, seq, d), lambda b, h: (b, 0, 0)),
        ],
        out_specs=pl.BlockSpec((None, seq, d), lambda b, h: (b, 0, 0)),
        out_shape=jax.ShapeDtypeStruct((bsz, seq, d), F32),
        scratch_shapes=[
            pltpu.VMEM((A_DK, A_DV), F32),
            pltpu.VMEM((6, seq // (GROUP * CHUNK), GROUP, CHUNK), F32),
        ],
        compiler_params=pltpu.CompilerParams(
            dimension_semantics=("parallel", "arbitrary"), vmem_limit_bytes=VMEM_LIMIT_BYTES),
        name="gdn_core",
    )(proj3, proj3, proj3, proj3, gates_t, norm_g, w_out, x3)


def _rope_table_kernel(pos_ref, invf_ref, cos_ref, sin_ref):
    tr, per_row = pos_ref.shape
    nf = ROPE_HALF

    def dot_split(a, b, terms):
        parts = _split3(a)[:terms]
        out = jnp.dot(parts[0], b, preferred_element_type=F32)
        for part in parts[1:]:
            out = out + jnp.dot(part, b, preferred_element_type=F32)
        return out

    rk = lax.broadcasted_iota(jnp.int32, (per_row, B_DH), 0)
    rl = lax.broadcasted_iota(jnp.int32, (per_row, B_DH), 1)
    rep = jnp.where(rl // nf == rk, 1.0, 0.0).astype(BF16)
    ang = dot_split(pos_ref[...].astype(F32), rep, 3) * invf_ref[...]
    c = jnp.cos(ang)
    s = jnp.sin(ang)

    ek = lax.broadcasted_iota(jnp.int32, (B_DH, B_DH), 0)
    el = lax.broadcasted_iota(jnp.int32, (B_DH, B_DH), 1)
    lane = lax.broadcasted_iota(jnp.int32, (1, B_DH), 1)
    rotary = (lane < nf) | ((lane >= ROPE_SHIFT) & (lane < ROPE_SHIFT + nf))
    base = jnp.where(rotary, 0.0, 1.0)
    for p in range(per_row):
        mine = ek // nf == p
        first = mine & (el == ek % nf)
        second = mine & (el == ek % nf + ROPE_SHIFT)
        e_cos = jnp.where(first | second, 1.0, 0.0).astype(BF16)
        e_sin = (jnp.where(second, 1.0, 0.0) - jnp.where(first, 1.0, 0.0)).astype(BF16)
        rows = pl.ds(p, tr, stride=per_row)
        cos_ref[rows, :] = dot_split(c, e_cos, 2) + base
        sin_ref[rows, :] = dot_split(s, e_sin, 2)


def _rope_tables(positions):
    t = positions.size
    per_row = B_DH // ROPE_HALF
    inv_freq = ROPE_THETA ** (-jnp.arange(0, ROPE_DIMS, 2, dtype=F32) / ROPE_DIMS)
    invf = jnp.tile(inv_freq, per_row)[None, :]
    rows = t // per_row
    tr = min(rows, 512)
    return pl.pallas_call(
        _rope_table_kernel,
        grid=(rows // tr,),
        in_specs=[pl.BlockSpec((tr, per_row), lambda i: (i, 0)), pl.BlockSpec((1, B_DH), lambda i: (0, 0))],
        out_specs=[pl.BlockSpec((tr * per_row, B_DH), lambda i: (i, 0))] * 2,
        out_shape=[jax.ShapeDtypeStruct((t, B_DH), F32)] * 2,
        name="rope_tables",
    )(positions.reshape(rows, per_row), invf)


def _attn_proj_kernel(x_ref, g_ref, w0_ref, w1_ref, w2_ref, cos_ref, sin_ref, qg_ref, kg_ref,
                      p0_ref, p1_ref, p2_ref, hn_ref):
    tm = x_ref.shape[0]
    n_lane_blocks = hn_ref.shape[0]
    hn_nat = _normed_f32(x_ref, g_ref)
    for j in range(n_lane_blocks):
        hn_ref[j] = hn_nat[:, j * B_DH:(j + 1) * B_DH]

    def by_stream(ref, dil):
        if dil == 1:
            return ref[...]
        return jnp.concatenate([ref[pl.ds(r, tm // dil, stride=dil), :] for r in range(dil)], axis=0)

    def hn_by_stream(dil):
        if dil == 1:
            return hn_nat
        return jnp.concatenate([by_stream(hn_ref.at[j], dil) for j in range(n_lane_blocks)], axis=-1)

    def norm_rope(y, gain, cos, sin):
        y = y * lax.rsqrt(jnp.mean(y * y, axis=-1, keepdims=True) + EPS) * gain
        return y * cos + pltpu.roll(y, ROPE_SHIFT, 1) * sin

    w_refs = (w0_ref, w1_ref, w2_ref)
    lhs = [hn_by_stream(dil).astype(BF16) for dil in B_DILATIONS]
    tables = [(by_stream(cos_ref, dil), by_stream(sin_ref, dil)) for dil in B_DILATIONS]

    def project(gi, n0):
        return jnp.dot(lhs[gi], w_refs[gi][:, n0:n0 + COL_TILE], preferred_element_type=F32)

    def epilogue(gi, n0, y):
        dil = B_DILATIONS[gi]
        if n0 < 2 * B_W:
            gain = (qg_ref if n0 < B_W else kg_ref)[gi:gi + 1, :]
            cos, sin = tables[gi]
            y = jnp.concatenate(
                [norm_rope(y[:, j * B_DH:(j + 1) * B_DH], gain, cos, sin)
                 for j in range(COL_TILE // B_DH)], axis=-1)
        y = y.astype(BF16)
        if gi == 0:
            p0_ref[:, n0:n0 + COL_TILE] = y
        else:
            out_ref = p1_ref if gi == 1 else p2_ref
            out_ref[:, :, n0:n0 + COL_TILE] = y.reshape(dil, tm // dil, COL_TILE)

    tiles = [(gi, n0) for gi in range(B_GROUPS) for n0 in range(0, w_refs[gi].shape[1], COL_TILE)]
    pending = project(*tiles[0])
    for idx, tile_id in enumerate(tiles):
        nxt = project(*tiles[idx + 1]) if idx + 1 < len(tiles) else None
        epilogue(*tile_id, pending)
        pending = nxt


def _attn_proj(x2, g, w0, w1, w2, cos, sin, q_gain, k_gain, bsz, seq):
    t, d = x2.shape
    tm = ATTN_ROW_TILE
    per_seq = seq // tm
    d1, d2 = B_DILATIONS[1], B_DILATIONS[2]
    return pl.pallas_call(
        _attn_proj_kernel,
        grid=(t // tm,),
        in_specs=[
            pl.BlockSpec((tm, d), lambda i: (i, 0)),
            _resident((1, d)),
            _resident(w0.shape),
            _resident(w1.shape),
            _resident(w2.shape),
            pl.BlockSpec((tm, B_DH), lambda i: (i, 0)),
            pl.BlockSpec((tm, B_DH), lambda i: (i, 0)),
            _resident((B_GROUPS, B_DH)),
            _resident((B_GROUPS, B_DH)),
        ],
        out_specs=[
            pl.BlockSpec((tm, w0.shape[1]), lambda i: (i, 0)),
            pl.BlockSpec((None, d1, tm // d1, w1.shape[1]), lambda i: (i // per_seq, 0, i % per_seq, 0)),
            pl.BlockSpec((None, d2, tm // d2, w2.shape[1]), lambda i: (i // per_seq, 0, i % per_seq, 0)),
        ],
        out_shape=[
            jax.ShapeDtypeStruct((t, w0.shape[1]), BF16),
            jax.ShapeDtypeStruct((bsz, d1, seq // d1, w1.shape[1]), BF16),
            jax.ShapeDtypeStruct((bsz, d2, seq // d2, w2.shape[1]), BF16),
        ],
        scratch_shapes=[pltpu.VMEM((d // B_DH, tm, B_DH), F32)],
        compiler_params=pltpu.CompilerParams(
            dimension_semantics=("parallel",), vmem_limit_bytes=VMEM_LIMIT_BYTES),
        name="attn_norm_proj",
    )(x2, g, w0, w1, w2, cos, sin, q_gain, k_gain)


def _attn_kernel(q0_ref, k0_ref, v0_ref, q1_ref, k1_ref, v1_ref, q2_ref, k2_ref, v2_ref, z_ref,
                 o_ref, og_ref, lg_ref):
    seq = q0_ref.shape[0]
    blk = B_BLOCK
    nb = ATTN_BATCH
    tile = nb * blk
    assert all(w // d == blk for w, d in zip(B_WINDOWS, B_DILATIONS))

    qi = lax.broadcasted_iota(jnp.int32, (nb, blk, 2 * blk), 1)
    kj = lax.broadcasted_iota(jnp.int32, (nb, blk, 2 * blk), 2)
    bi = lax.broadcasted_iota(jnp.int32, (nb, blk, 2 * blk), 0)
    band = ((kj < blk) & (kj >= qi)) | ((kj >= blk) & (kj - blk <= qi))
    band_first = band & ((kj >= blk) | (bi > 0))
    cur_only = (lax.broadcasted_iota(jnp.int32, (nb, blk, blk), 2)
                <= lax.broadcasted_iota(jnp.int32, (nb, blk, blk), 1))
    ones_v = jnp.ones((nb, blk, B_DH), BF16)
    zero_blk = jnp.zeros((1, blk, B_DH), BF16)

    q_refs = (q0_ref, q1_ref, q2_ref)
    k_refs = (k0_ref, k1_ref, k2_ref)
    v_refs = (v0_ref, v1_ref, v2_ref)

    def blocks(ref, first, count):
        return ref[pl.ds(first, count * blk), :].reshape(count, blk, B_DH)

    def scores(gi, first, starts_stream, has_prev):
        qb = blocks(q_refs[gi], first, nb)
        kc = blocks(k_refs[gi], first, nb)
        if not has_prev:
            s = jnp.einsum("bqd,bkd->bqk", qb, kc, preferred_element_type=F32)
            return jnp.where(cur_only, s, -jnp.inf)
        if starts_stream:
            kp = jnp.concatenate([zero_blk, kc[:nb - 1]], axis=0)
        else:
            kp = blocks(k_refs[gi], first - blk, nb)
        s = jnp.einsum("bqd,bkd->bqk", qb, jnp.concatenate([kp, kc], axis=1),
                       preferred_element_type=F32)
        return jnp.where(band_first if starts_stream else band, s, -jnp.inf)

    def finish(gi, first, starts_stream, has_prev, out_rows, s):
        m = jnp.max(s, axis=-1, keepdims=True)
        p = jnp.exp(s - m).astype(BF16)
        vc = jnp.concatenate([blocks(v_refs[gi], first, nb), ones_v], axis=-1)
        if has_prev:
            if starts_stream:
                vp = jnp.concatenate([jnp.zeros((1, blk, 2 * B_DH), BF16), vc[:nb - 1]], axis=0)
            else:
                vp = jnp.concatenate([blocks(v_refs[gi], first - blk, nb), ones_v], axis=-1)
            vc = jnp.concatenate([vp, vc], axis=1)
        acc = jnp.einsum("bqk,bkd->bqd", p, vc, preferred_element_type=F32)
        den = acc[..., B_DH:]
        o = acc[..., :B_DH] / den
        lse = m + jnp.log(den)
        b0 = 0
        for rows, count in out_rows:
            og_ref[gi, rows, :] = o[b0:b0 + count].reshape(count * blk, B_DH)
            lg_ref[gi, rows, :] = lse[b0:b0 + count].reshape(count * blk, B_DH)
            b0 += count

    batches = []
    for gi in range(B_GROUPS):
        dil = B_DILATIONS[gi]
        length = seq // dil
        per_stream = length // blk
        for first in range(0, seq, tile):
            r, m0 = divmod(first, length)
            if per_stream >= nb:
                t0 = r + m0 * dil
                rows = pl.ds(t0, tile) if dil == 1 else pl.ds(t0, tile, stride=dil)
                batches.append((gi, first, m0 == 0, True, [(rows, nb)]))
            else:
                assert per_stream == 1
                rows = [(pl.ds(r + j, blk, stride=dil), 1) for j in range(nb)]
                batches.append((gi, first, True, False, rows))

    pending = {}
    for i in range(len(batches) + ATTN_LOOKAHEAD):
        if i < len(batches):
            pending[i] = scores(*batches[i][:4])
        if i >= ATTN_LOOKAHEAD:
            j = i - ATTN_LOOKAHEAD
            finish(*batches[j], pending.pop(j))

    def merge(t, carry):
        r0 = pl.multiple_of(t * tile, tile)
        l0 = lg_ref[0, pl.ds(r0, tile), :]
        l1 = lg_ref[1, pl.ds(r0, tile), :]
        l2 = lg_ref[2, pl.ds(r0, tile), :]
        m = jnp.maximum(jnp.maximum(l0, l1), l2)
        w0 = jnp.exp(l0 - m)
        w1 = jnp.exp(l1 - m)
        w2 = jnp.exp(l2 - m)
        num = (w0 * og_ref[0, pl.ds(r0, tile), :] + w1 * og_ref[1, pl.ds(r0, tile), :]
               + w2 * og_ref[2, pl.ds(r0, tile), :])
        o = num / (w0 + w1 + w2)
        z = z_ref[pl.ds(r0, tile), :].astype(F32)
        o_ref[pl.ds(r0, tile), :] = (o * (z * jax.nn.sigmoid(z))).astype(o_ref.dtype)
        return carry

    lax.fori_loop(0, seq // tile, merge, 0)


def _attn_core(p0, p1, p2):
    bsz, seq, _ = p0.shape

    def cols(which):
        return pl.BlockSpec((None, seq, B_DH), lambda b, h: (b, 0, which * B_HEADS + h))

    qkv = [cols(0), cols(1), cols(2)]
    return pl.pallas_call(
        _attn_kernel,
        grid=(bsz, B_HEADS),
        in_specs=[*qkv, *qkv, *qkv, cols(3)],
        out_specs=pl.BlockSpec((None, seq, B_DH), lambda b, h: (b, 0, h)),
        out_shape=jax.ShapeDtypeStruct((bsz, seq, B_W), BF16),
        scratch_shapes=[
            pltpu.VMEM((B_GROUPS, seq, B_DH), F32),
            pltpu.VMEM((B_GROUPS, seq, B_DH), F32),
        ],
        compiler_params=pltpu.CompilerParams(
            dimension_semantics=("parallel", "parallel"), vmem_limit_bytes=VMEM_LIMIT_BYTES),
        name="dilated_attn",
    )(p0, p0, p0, p1, p1, p1, p2, p2, p2, p0)


def _rotary_lane_order(a):
    gap = ROPE_SHIFT - ROPE_HALF
    return jnp.concatenate(
        [a[..., :ROPE_HALF], a[..., ROPE_DIMS:ROPE_DIMS + gap], a[..., ROPE_HALF:ROPE_DIMS],
         a[..., ROPE_DIMS + gap:]], axis=-1)


def kernel(x, positions, norm_g, a_w_in, a_conv_w, a_log, a_dt_bias, a_norm_g, a_w_out,
           b_w_in, b_q_norm_g, b_k_norm_g, b_w_out):
    bsz, seq, d = x.shape
    t = bsz * seq
    x2 = x.reshape(t, d)

    w_main = a_w_in[0, :, :A_MAIN].astype(BF16)
    w_gate_t = a_w_in[0, :, A_MAIN:].T.astype(BF16)
    zeros8 = jnp.zeros((A_HEADS,), F32)
    alog16 = jnp.concatenate([zeros8, a_log[0].astype(F32)])[:, None]
    dtb16 = jnp.concatenate([zeros8, a_dt_bias[0].astype(F32)])[:, None]
    proj, gates_t = _gdn_proj(x2, norm_g[0][None, :], w_main, w_gate_t, alog16, dtb16,
                              a_conv_w[0], seq)
    x3 = _gdn_core(proj.reshape(bsz, seq, A_MAIN), gates_t, a_norm_g[0][None, :],
                   a_w_out[0].astype(BF16), x)
    x2 = x3.reshape(t, d)

    wb = b_w_in[0]
    n_qkv = 3 * B_GROUPS * B_W
    w5 = wb[:, :n_qkv].reshape(d, 3, B_GROUPS, B_HEADS, B_DH)

    def group_weights(gi):
        parts = [_rotary_lane_order(w5[:, 0, gi]), _rotary_lane_order(w5[:, 1, gi]), w5[:, 2, gi]]
        return jnp.concatenate([p.reshape(d, B_W) for p in parts], axis=1)

    w0 = jnp.concatenate([group_weights(0), wb[:, n_qkv:]], axis=1).astype(BF16)
    w1 = group_weights(1).astype(BF16)
    w2 = group_weights(2).astype(BF16)
    q_gain = _rotary_lane_order(b_q_norm_g[0]) * (B_DH ** -0.5)
    k_gain = _rotary_lane_order(b_k_norm_g[0])
    cos, sin = _rope_tables(positions)
    p0, p1, p2 = _attn_proj(x2, norm_g[1][None, :], w0, w1, w2, cos, sin, q_gain, k_gain, bsz, seq)
    o = _attn_core(p0.reshape(bsz, seq, -1), p1.reshape(bsz, seq, -1), p2.reshape(bsz, seq, -1))
    x2 = _out_proj(o.reshape(t, B_W), b_w_out[0].astype(BF16), x2, "attn_out_proj")
    return x2.reshape(bsz, seq, d)
```

```python
import functools

import jax
import jax.numpy as jnp
from jax import lax
from jax.experimental import pallas as pl
from jax.experimental.pallas import tpu as pltpu

F32 = jnp.float32
BF16 = jnp.bfloat16
EPS = 1e-6

D_MODEL = 1024
A_HEADS = 8
A_DK = 128
A_DV = 256
A_QK = A_HEADS * A_DK
A_VW = A_HEADS * A_DV
A_CONVW = 2 * A_QK + A_VW
A_MAIN = A_CONVW + A_VW
A_CONV = 4
CHUNK = 128
GROUP = 16
OUT_BATCH = 2
B_WINDOWS = (128, 512, 2048)
B_DILATIONS = (1, 4, 16)
B_GROUPS = 3
B_HEADS = 8
B_DH = 128
B_W = B_HEADS * B_DH
B_BLOCK = 128
ROPE_THETA = 500000.0
ROPE_DIMS = B_DH // 4
ROPE_HALF = ROPE_DIMS // 2
ROPE_SHIFT = B_DH // 2
ATTN_BATCH = 4
ATTN_LOOKAHEAD = 2

VMEM_LIMIT_BYTES = 56 * 1024 * 1024
OUT_ROW_TILE = 512
GDN_ROW_TILE = 256
ATTN_ROW_TILE = 256
COL_TILE = 512
HALO = 16


def _split3(a):
    hi = a.astype(BF16)
    r = a - hi.astype(F32)
    mid = r.astype(BF16)
    lo = (r - mid.astype(F32)).astype(BF16)
    return hi, mid, lo


def _bdot(a, b):
    return jnp.einsum("cik,ckj->cij", a, b, preferred_element_type=F32)


def _normed_f32(x_ref, g_ref):
    x = x_ref[...]
    ms = jnp.mean(x * x, axis=-1, keepdims=True)
    return x * lax.rsqrt(ms + EPS) * g_ref[...]


def _resident(shape):
    return pl.BlockSpec(shape, lambda i: (0,) * len(shape), pipeline_mode=pl.Buffered(1))


def _gdn_proj_kernel(x_ref, xh_ref, g_ref, w_ref, wgt_ref, alog_ref, dtb_ref, cw_ref,
                     o_ref, gates_ref, *, tiles_per_seq):
    tm = x_ref.shape[0]
    first = (pl.program_id(0) % tiles_per_seq) == 0
    hn = _normed_f32(x_ref, g_ref).astype(BF16)
    halo = jnp.where(first, 0.0, _normed_f32(xh_ref, g_ref)).astype(BF16)
    hcat = jnp.concatenate([halo, hn], axis=0)

    def project(n0):
        lhs = hcat if n0 < A_CONVW else hn
        return jnp.dot(lhs, w_ref[:, n0:n0 + COL_TILE], preferred_element_type=F32)

    def epilogue(n0, y):
        if n0 < A_CONVW:
            cw = cw_ref[:, n0:n0 + COL_TILE]
            acc = y * cw[A_CONV - 1:A_CONV, :]
            for tap in range(A_CONV - 1):
                acc = acc + pltpu.roll(y, A_CONV - 1 - tap, 0) * cw[tap:tap + 1, :]
            y = acc[HALO:, :]
        o_ref[:, n0:n0 + COL_TILE] = y.astype(o_ref.dtype)

    starts = list(range(0, A_MAIN, COL_TILE))
    pending = project(starts[0])
    for idx, n0 in enumerate(starts):
        nxt = project(starts[idx + 1]) if idx + 1 < len(starts) else None
        epilogue(n0, pending)
        pending = nxt

    logits = lax.dot_general(wgt_ref[...], hn, (((1,), (1,)), ((), ())), preferred_element_type=F32)
    row = lax.broadcasted_iota(jnp.int32, logits.shape, 0)
    xs = logits + dtb_ref[...]
    softplus = jnp.maximum(xs, 0.0) + jnp.log1p(jnp.exp(-jnp.abs(xs)))
    decay = -jnp.exp(alog_ref[...]) * softplus
    gates_ref[...] = jnp.where(row < A_HEADS, jax.nn.sigmoid(logits), decay)


def _gdn_proj(x2, g, w, wgt, alog16, dtb16, conv_w, seq):
    t, d = x2.shape
    tm = GDN_ROW_TILE
    kern = functools.partial(_gdn_proj_kernel, tiles_per_seq=seq // tm)
    return pl.pallas_call(
        kern,
        grid=(t // tm,),
        in_specs=[
            pl.BlockSpec((tm, d), lambda i: (i, 0)),
            pl.BlockSpec((HALO, d), lambda i: (jnp.maximum(i * (tm // HALO) - 1, 0), 0)),
            _resident((1, d)),
            _resident((d, A_MAIN)),
            _resident((2 * A_HEADS, d)),
            _resident((2 * A_HEADS, 1)),
            _resident((2 * A_HEADS, 1)),
            _resident((A_CONV, A_CONVW)),
        ],
        out_specs=[
            pl.BlockSpec((tm, A_MAIN), lambda i: (i, 0)),
            pl.BlockSpec((2 * A_HEADS, tm), lambda i: (0, i)),
        ],
        out_shape=[
            jax.ShapeDtypeStruct((t, A_MAIN), BF16),
            jax.ShapeDtypeStruct((2 * A_HEADS, t), F32),
        ],
        compiler_params=pltpu.CompilerParams(
            dimension_semantics=("parallel",), vmem_limit_bytes=VMEM_LIMIT_BYTES),
        name="gdn_norm_proj",
    )(x2, x2, g, w, wgt, alog16, dtb16, conv_w)


def _out_proj_kernel(o_ref, w_ref, x_ref, y_ref):
    y_ref[...] = x_ref[...] + jnp.dot(o_ref[...], w_ref[...], preferred_element_type=F32)


def _out_proj(o2, w, x2, name):
    t, k = o2.shape
    d = w.shape[1]
    tm = OUT_ROW_TILE
    return pl.pallas_call(
        _out_proj_kernel,
        grid=(t // tm,),
        in_specs=[
            pl.BlockSpec((tm, k), lambda i: (i, 0)),
            _resident((k, d)),
            pl.BlockSpec((tm, d), lambda i: (i, 0)),
        ],
        out_specs=pl.BlockSpec((tm, d), lambda i: (i, 0)),
        out_shape=jax.ShapeDtypeStruct((t, d), F32),
        compiler_params=pltpu.CompilerParams(
            dimension_semantics=("parallel",), vmem_limit_bytes=VMEM_LIMIT_BYTES),
        name=name,
    )(o2, w, x2)


def _gdn_kernel(q_ref, k_ref, v_ref, z_ref, gt_ref, ng_ref, wo_ref, x_ref, y_ref, s_ref, rf_ref):
    head = pl.program_id(1)
    rows = GROUP * CHUNK
    n_groups = q_ref.shape[0] // rows
    c = CHUNK

    ii = lax.broadcasted_iota(jnp.int32, (c, c), 0)
    jj = lax.broadcasted_iota(jnp.int32, (c, c), 1)
    eye = jnp.where(ii == jj, 1.0, 0.0).astype(F32)
    lower_incl = ii >= jj
    lower_strict = ii > jj
    cum_rhs = jnp.where(ii <= jj, 1.0, 0.0).astype(BF16)

    def same_block(size):
        return (ii // size) == (jj // size)

    def col(x):
        return jnp.swapaxes(jnp.broadcast_to(x[:, None, :], (GROUP, c, c)), 1, 2)

    s_ref[...] = jnp.zeros_like(s_ref)

    n_chunks = q_ref.shape[0] // c

    def chunk_rows(r):
        full = gt_ref[pl.ds(r, 1), :]
        return jnp.concatenate([full[:, n * c:(n + 1) * c] for n in range(n_chunks)], axis=0)

    beta_all = chunk_rows(head)
    gh, gm, gl = _split3(chunk_rows(head + A_HEADS))
    gc_all = (jnp.dot(gh, cum_rhs, preferred_element_type=F32)
              + jnp.dot(gm, cum_rhs, preferred_element_type=F32)
              + jnp.dot(gl, cum_rhs, preferred_element_type=F32))
    exp_all = jnp.exp(gc_all)
    last_all = jnp.broadcast_to(gc_all[:, c - 1:c], gc_all.shape)
    rowforms = (beta_all, gc_all, exp_all, beta_all * exp_all, jnp.exp(last_all - gc_all),
                jnp.exp(last_all))
    for idx, val in enumerate(rowforms):
        for n in range(n_groups):
            rf_ref[idx, n] = val[n * GROUP:(n + 1) * GROUP, :]

    def chunk_local(g):
        r0 = g * rows
        def silu_rows(ref):
            a = ref[pl.ds(r0, rows), :].astype(F32)
            return a * jax.nn.sigmoid(a)

        def l2n(a):
            return a * lax.rsqrt(jnp.sum(a * a, axis=-1, keepdims=True) + EPS)

        q3 = (l2n(silu_rows(q_ref)) * (A_DK ** -0.5)).reshape(GROUP, c, A_DK)
        k3 = l2n(silu_rows(k_ref)).reshape(GROUP, c, A_DK)
        v3 = silu_rows(v_ref).reshape(GROUP, c, A_DV)
        q3b = q3.astype(BF16)
        k3b = k3.astype(BF16)

        gc = rf_ref[1, g]
        beta_c = col(rf_ref[0, g])
        gc_c = col(gc)
        exp_gc_c = col(rf_ref[2, g])
        beta_exp_c = col(rf_ref[3, g])
        tail_c = col(rf_ref[4, g])
        e_last = rf_ref[5, g]
        e_last = jnp.concatenate([e_last, e_last], axis=-1)
        decay = jnp.exp(jnp.where(lower_incl[None], gc_c - gc[:, None, :], -jnp.inf))

        kb3 = (k3 * beta_c).astype(BF16)
        kk = jnp.einsum("cid,cjd->cij", kb3, k3b, preferred_element_type=F32)
        qk = jnp.einsum("cid,cjd->cij", q3b, k3b, preferred_element_type=F32) * decay
        low = jnp.where(lower_strict[None], kk * decay, 0.0)

        m1 = jnp.where(same_block(8)[None], -low, 0.0)
        m1b = m1.astype(BF16)
        tinv = eye[None] + m1
        m2b = _bdot(m1b, m1b).astype(BF16)
        tinv = tinv + _bdot(tinv.astype(BF16), m2b)
        m4b = _bdot(m2b, m2b).astype(BF16)
        tinv = tinv + _bdot(tinv.astype(BF16), m4b)
        size = 8
        while size < c:
            off = jnp.where((same_block(2 * size) & jnp.logical_not(same_block(size)))[None], low, 0.0)
            tb = tinv.astype(BF16)
            tinv = tinv - _bdot(_bdot(tb, off.astype(BF16)).astype(BF16), tb)
            size *= 2

        tb = tinv.astype(BF16)
        vb = (v3 * jnp.concatenate([beta_c, beta_c], axis=-1)).astype(BF16)
        kbg = (k3 * beta_exp_c).astype(BF16)
        u = _bdot(tb, vb)
        w = _bdot(tb, kbg)

        qt = q3 * exp_gc_c
        kd = (k3 * tail_c).astype(BF16)
        ub = u.astype(BF16)
        wb = w.astype(BF16)
        qkb = qk.astype(BF16)
        kdt = jnp.swapaxes(kd, 1, 2)
        a_neg = -_bdot(kdt, wb)
        b_add = _bdot(kdt, ub)
        q_eff = qt - _bdot(qkb, wb)
        o_loc = _bdot(qkb, ub)
        p_all = jnp.concatenate([a_neg, q_eff], axis=1).astype(BF16)
        return p_all, b_add, o_loc, e_last

    def project_out(row, gated):
        lhs = jnp.concatenate(gated, axis=0)
        y_ref[pl.ds(row, lhs.shape[0]), :] += jnp.dot(lhs, wo_ref[...], preferred_element_type=F32)

    def state_steps(g, p_all, b_add, o_loc, e_last):
        ng = ng_ref[...]
        ready = None
        gated = []
        for cc in range(GROUP):
            state = s_ref[...]
            r = jnp.dot(p_all[cc], state.astype(BF16), preferred_element_type=F32)
            o = r[A_DK:, :] + o_loc[cc]
            s_ref[...] = state * e_last[cc:cc + 1, :] + r[:A_DK, :] + b_add[cc]
            if ready is not None:
                project_out(*ready)
                ready = None
            on = o * lax.rsqrt(jnp.mean(o * o, axis=-1, keepdims=True) + EPS) * ng
            row = g * rows + cc * c
            zc = z_ref[pl.ds(row, c), :].astype(F32)
            gated.append((on * (zc * jax.nn.sigmoid(zc))).astype(BF16))
            if len(gated) == OUT_BATCH:
                ready = (row - (OUT_BATCH - 1) * c, gated)
                gated = []
        project_out(*ready)

    @pl.when(head == 0)
    def _():
        y_ref[...] = x_ref[...]

    local = chunk_local(0)
    for g in range(n_groups):
        nxt = chunk_local(g + 1) if g + 1 < n_groups else None
        state_steps(g, *local)
        local = nxt


def _gdn_core(proj3, gates_t, norm_g, w_out, x3):
    bsz, seq, _ = proj3.shape
    d = x3.shape[-1]
    kb = A_QK // A_DK
    vb = 2 * A_QK // A_DV
    zb = vb + A_VW // A_DV
    return pl.pallas_call(
        _gdn_kernel,
        grid=(bsz, A_HEADS),
        in_specs=[
            pl.BlockSpec((None, seq, A_DK), lambda b, h: (b, 0, h)),
            pl.BlockSpec((None, seq, A_DK), lambda b, h: (b, 0, kb + h)),
            pl.BlockSpec((None, seq, A_DV), lambda b, h: (b, 0, vb + h)),
            pl.BlockSpec((None, seq, A_DV), lambda b, h: (b, 0, zb + h)),
            pl.BlockSpec((2 * A_HEADS, seq), lambda b, h: (0, b)),
            pl.BlockSpec((1, A_DV), lambda b, h: (0, 0)),
            pl.BlockSpec((A_DV, d), lambda b, h: (h, 0)),
            pl.BlockSpec((None, seq, d), lambda b, h: (b, 0, 0)),
        ],
        out_specs=pl.BlockSpec((None, seq, d), lambda b, h: (b, 0, 0)),
        out_shape=jax.ShapeDtypeStruct((bsz, seq, d), F32),
        scratch_shapes=[
            pltpu.VMEM((A_DK, A_DV), F32),
            pltpu.VMEM((6, seq // (GROUP * CHUNK), GROUP, CHUNK), F32),
        ],
        compiler_params=pltpu.CompilerParams(
            dimension_semantics=("parallel", "arbitrary"), vmem_limit_bytes=VMEM_LIMIT_BYTES),
        name="gdn_core",
    )(proj3, proj3, proj3, proj3, gates_t, norm_g, w_out, x3)


def _rope_table_kernel(pos_ref, invf_ref, cos_ref, sin_ref):
    tr, per_row = pos_ref.shape
    nf = ROPE_HALF

    def dot_split(a, b, terms):
        parts = _split3(a)[:terms]
        out = jnp.dot(parts[0], b, preferred_element_type=F32)
        for part in parts[1:]:
            out = out + jnp.dot(part, b, preferred_element_type=F32)
        return out

    rk = lax.broadcasted_iota(jnp.int32, (per_row, B_DH), 0)
    rl = lax.broadcasted_iota(jnp.int32, (per_row, B_DH), 1)
    rep = jnp.where(rl // nf == rk, 1.0, 0.0).astype(BF16)
    ang = dot_split(pos_ref[...].astype(F32), rep, 3) * invf_ref[...]
    c = jnp.cos(ang)
    s = jnp.sin(ang)

    ek = lax.broadcasted_iota(jnp.int32, (B_DH, B_DH), 0)
    el = lax.broadcasted_iota(jnp.int32, (B_DH, B_DH), 1)
    lane = lax.broadcasted_iota(jnp.int32, (1, B_DH), 1)
    rotary = (lane < nf) | ((lane >= ROPE_SHIFT) & (lane < ROPE_SHIFT + nf))
    base = jnp.where(rotary, 0.0, 1.0)
    for p in range(per_row):
        mine = ek // nf == p
        first = mine & (el == ek % nf)
        second = mine & (el == ek % nf + ROPE_SHIFT)
        e_cos = jnp.where(first | second, 1.0, 0.0).astype(BF16)
        e_sin = (jnp.where(second, 1.0, 0.0) - jnp.where(first, 1.0, 0.0)).astype(BF16)
        rows = pl.ds(p, tr, stride=per_row)
        cos_ref[rows, :] = dot_split(c, e_cos, 2) + base
        sin_ref[rows, :] = dot_split(s, e_sin, 2)


def _rope_tables(positions):
    t = positions.size
    per_row = B_DH // ROPE_HALF
    inv_freq = ROPE_THETA ** (-jnp.arange(0, ROPE_DIMS, 2, dtype=F32) / ROPE_DIMS)
    invf = jnp.tile(inv_freq, per_row)[None, :]
    rows = t // per_row
    tr = min(rows, 512)
    return pl.pallas_call(
        _rope_table_kernel,
        grid=(rows // tr,),
        in_specs=[pl.BlockSpec((tr, per_row), lambda i: (i, 0)), pl.BlockSpec((1, B_DH), lambda i: (0, 0))],
        out_specs=[pl.BlockSpec((tr * per_row, B_DH), lambda i: (i, 0))] * 2,
        out_shape=[jax.ShapeDtypeStruct((t, B_DH), F32)] * 2,
        name="rope_tables",
    )(positions.reshape(rows, per_row), invf)


def _attn_proj_kernel(x_ref, g_ref, w0_ref, w1_ref, w2_ref, cos_ref, sin_ref, qg_ref, kg_ref,
                      p0_ref, p1_ref, p2_ref, hn_ref):
    tm = x_ref.shape[0]
    n_lane_blocks = hn_ref.shape[0]
    hn_nat = _normed_f32(x_ref, g_ref)
    for j in range(n_lane_blocks):
        hn_ref[j] = hn_nat[:, j * B_DH:(j + 1) * B_DH]

    def by_stream(ref, dil):
        if dil == 1:
            return ref[...]
        return jnp.concatenate([ref[pl.ds(r, tm // dil, stride=dil), :] for r in range(dil)], axis=0)

    def hn_by_stream(dil):
        if dil == 1:
            return hn_nat
        return jnp.concatenate([by_stream(hn_ref.at[j], dil) for j in range(n_lane_blocks)], axis=-1)

    def norm_rope(y, gain, cos, sin):
        y = y * lax.rsqrt(jnp.mean(y * y, axis=-1, keepdims=True) + EPS) * gain
        return y * cos + pltpu.roll(y, ROPE_SHIFT, 1) * sin

    w_refs = (w0_ref, w1_ref, w2_ref)
    lhs = [hn_by_stream(dil).astype(BF16) for dil in B_DILATIONS]
    tables = [(by_stream(cos_ref, dil), by_stream(sin_ref, dil)) for dil in B_DILATIONS]

    def project(gi, n0):
        return jnp.dot(lhs[gi], w_refs[gi][:, n0:n0 + COL_TILE], preferred_element_type=F32)

    def epilogue(gi, n0, y):
        dil = B_DILATIONS[gi]
        if n0 < 2 * B_W:
            gain = (qg_ref if n0 < B_W else kg_ref)[gi:gi + 1, :]
            cos, sin = tables[gi]
            y = jnp.concatenate(
                [norm_rope(y[:, j * B_DH:(j + 1) * B_DH], gain, cos, sin)
                 for j in range(COL_TILE // B_DH)], axis=-1)
        y = y.astype(BF16)
        if gi == 0:
            p0_ref[:, n0:n0 + COL_TILE] = y
        else:
            out_ref = p1_ref if gi == 1 else p2_ref
            out_ref[:, :, n0:n0 + COL_TILE] = y.reshape(dil, tm // dil, COL_TILE)

    tiles = [(gi, n0) for gi in range(B_GROUPS) for n0 in range(0, w_refs[gi].shape[1], COL_TILE)]
    pending = project(*tiles[0])
    for idx, tile_id in enumerate(tiles):
        nxt = project(*tiles[idx + 1]) if idx + 1 < len(tiles) else None
        epilogue(*tile_id, pending)
        pending = nxt


def _attn_proj(x2, g, w0, w1, w2, cos, sin, q_gain, k_gain, bsz, seq):
    t, d = x2.shape
    tm = ATTN_ROW_TILE
    per_seq = seq // tm
    d1, d2 = B_DILATIONS[1], B_DILATIONS[2]
    return pl.pallas_call(
        _attn_proj_kernel,
        grid=(t // tm,),
        in_specs=[
            pl.BlockSpec((tm, d), lambda i: (i, 0)),
            _resident((1, d)),
            _resident(w0.shape),
            _resident(w1.shape),
            _resident(w2.shape),
            pl.BlockSpec((tm, B_DH), lambda i: (i, 0)),
            pl.BlockSpec((tm, B_DH), lambda i: (i, 0)),
            _resident((B_GROUPS, B_DH)),
            _resident((B_GROUPS, B_DH)),
        ],
        out_specs=[
            pl.BlockSpec((tm, w0.shape[1]), lambda i: (i, 0)),
            pl.BlockSpec((None, d1, tm // d1, w1.shape[1]), lambda i: (i // per_seq, 0, i % per_seq, 0)),
            pl.BlockSpec((None, d2, tm // d2, w2.shape[1]), lambda i: (i // per_seq, 0, i % per_seq, 0)),
        ],
        out_shape=[
            jax.ShapeDtypeStruct((t, w0.shape[1]), BF16),
            jax.ShapeDtypeStruct((bsz, d1, seq // d1, w1.shape[1]), BF16),
            jax.ShapeDtypeStruct((bsz, d2, seq // d2, w2.shape[1]), BF16),
        ],
        scratch_shapes=[pltpu.VMEM((d // B_DH, tm, B_DH), F32)],
        compiler_params=pltpu.CompilerParams(
            dimension_semantics=("parallel",), vmem_limit_bytes=VMEM_LIMIT_BYTES),
        name="attn_norm_proj",
    )(x2, g, w0, w1, w2, cos, sin, q_gain, k_gain)


def _attn_kernel(q0_ref, k0_ref, v0_ref, q1_ref, k1_ref, v1_ref, q2_ref, k2_ref, v2_ref, z_ref,
                 o_ref, og_ref, lg_ref):
    seq = q0_ref.shape[0]
    blk = B_BLOCK
    nb = ATTN_BATCH
    tile = nb * blk
    assert all(w // d == blk for w, d in zip(B_WINDOWS, B_DILATIONS))

    qi = lax.broadcasted_iota(jnp.int32, (nb, blk, 2 * blk), 1)
    kj = lax.broadcasted_iota(jnp.int32, (nb, blk, 2 * blk), 2)
    bi = lax.broadcasted_iota(jnp.int32, (nb, blk, 2 * blk), 0)
    band = ((kj < blk) & (kj >= qi)) | ((kj >= blk) & (kj - blk <= qi))
    band_first = band & ((kj >= blk) | (bi > 0))
    cur_only = (lax.broadcasted_iota(jnp.int32, (nb, blk, blk), 2)
                <= lax.broadcasted_iota(jnp.int32, (nb, blk, blk), 1))
    ones_v = jnp.ones((nb, blk, B_DH), BF16)
    zero_blk = jnp.zeros((1, blk, B_DH), BF16)

    q_refs = (q0_ref, q1_ref, q2_ref)
    k_refs = (k0_ref, k1_ref, k2_ref)
    v_refs = (v0_ref, v1_ref, v2_ref)

    def blocks(ref, first, count):
        return ref[pl.ds(first, count * blk), :].reshape(count, blk, B_DH)

    def scores(gi, first, starts_stream, has_prev):
        qb = blocks(q_refs[gi], first, nb)
        kc = blocks(k_refs[gi], first, nb)
        if not has_prev:
            s = jnp.einsum("bqd,bkd->bqk", qb, kc, preferred_element_type=F32)
            return jnp.where(cur_only, s, -jnp.inf)
        if starts_stream:
            kp = jnp.concatenate([zero_blk, kc[:nb - 1]], axis=0)
        else:
            kp = blocks(k_refs[gi], first - blk, nb)
        s = jnp.einsum("bqd,bkd->bqk", qb, jnp.concatenate([kp, kc], axis=1),
                       preferred_element_type=F32)
        return jnp.where(band_first if starts_stream else band, s, -jnp.inf)

    def finish(gi, first, starts_stream, has_prev, out_rows, s):
        m = jnp.max(s, axis=-1, keepdims=True)
        p = jnp.exp(s - m).astype(BF16)
        vc = jnp.concatenate([blocks(v_refs[gi], first, nb), ones_v], axis=-1)
        if has_prev:
            if starts_stream:
                vp = jnp.concatenate([jnp.zeros((1, blk, 2 * B_DH), BF16), vc[:nb - 1]], axis=0)
            else:
                vp = jnp.concatenate([blocks(v_refs[gi], first - blk, nb), ones_v], axis=-1)
            vc = jnp.concatenate([vp, vc], axis=1)
        acc = jnp.einsum("bqk,bkd->bqd", p, vc, preferred_element_type=F32)
        den = acc[..., B_DH:]
        o = acc[..., :B_DH] / den
        lse = m + jnp.log(den)
        b0 = 0
        for rows, count in out_rows:
            og_ref[gi, rows, :] = o[b0:b0 + count].reshape(count * blk, B_DH)
            lg_ref[gi, rows, :] = lse[b0:b0 + count].reshape(count * blk, B_DH)
            b0 += count

    batches = []
    for gi in range(B_GROUPS):
        dil = B_DILATIONS[gi]
        length = seq // dil
        per_stream = length // blk
        for first in range(0, seq, tile):
            r, m0 = divmod(first, length)
            if per_stream >= nb:
                t0 = r + m0 * dil
                rows = pl.ds(t0, tile) if dil == 1 else pl.ds(t0, tile, stride=dil)
                batches.append((gi, first, m0 == 0, True, [(rows, nb)]))
            else:
                assert per_stream == 1
                rows = [(pl.ds(r + j, blk, stride=dil), 1) for j in range(nb)]
                batches.append((gi, first, True, False, rows))

    pending = {}
    for i in range(len(batches) + ATTN_LOOKAHEAD):
        if i < len(batches):
            pending[i] = scores(*batches[i][:4])
        if i >= ATTN_LOOKAHEAD:
            j = i - ATTN_LOOKAHEAD
            finish(*batches[j], pending.pop(j))

    def merge(t, carry):
        r0 = pl.multiple_of(t * tile, tile)
        l0 = lg_ref[0, pl.ds(r0, tile), :]
        l1 = lg_ref[1, pl.ds(r0, tile), :]
        l2 = lg_ref[2, pl.ds(r0, tile), :]
        m = jnp.maximum(jnp.maximum(l0, l1), l2)
        w0 = jnp.exp(l0 - m)
        w1 = jnp.exp(l1 - m)
        w2 = jnp.exp(l2 - m)
        num = (w0 * og_ref[0, pl.ds(r0, tile), :] + w1 * og_ref[1, pl.ds(r0, tile), :]
               + w2 * og_ref[2, pl.ds(r0, tile), :])
        o = num / (w0 + w1 + w2)
        z = z_ref[pl.ds(r0, tile), :].astype(F32)
        o_ref[pl.ds(r0, tile), :] = (o * (z * jax.nn.sigmoid(z))).astype(o_ref.dtype)
        return carry

    lax.fori_loop(0, seq // tile, merge, 0)


def _attn_core(p0, p1, p2):
    bsz, seq, _ = p0.shape

    def cols(which):
        return pl.BlockSpec((None, seq, B_DH), lambda b, h: (b, 0, which * B_HEADS + h))

    qkv = [cols(0), cols(1), cols(2)]
    return pl.pallas_call(
        _attn_kernel,
        grid=(bsz, B_HEADS),
        in_specs=[*qkv, *qkv, *qkv, cols(3)],
        out_specs=pl.BlockSpec((None, seq, B_DH), lambda b, h: (b, 0, h)),
        out_shape=jax.ShapeDtypeStruct((bsz, seq, B_W), BF16),
        scratch_shapes=[
            pltpu.VMEM((B_GROUPS, seq, B_DH), F32),
            pltpu.VMEM((B_GROUPS, seq, B_DH), F32),
        ],
        compiler_params=pltpu.CompilerParams(
            dimension_semantics=("parallel", "parallel"), vmem_limit_bytes=VMEM_LIMIT_BYTES),
        name="dilated_attn",
    )(p0, p0, p0, p1, p1, p1, p2, p2, p2, p0)


def _rotary_lane_order(a):
    gap = ROPE_SHIFT - ROPE_HALF
    return jnp.concatenate(
        [a[..., :ROPE_HALF], a[..., ROPE_DIMS:ROPE_DIMS + gap], a[..., ROPE_HALF:ROPE_DIMS],
         a[..., ROPE_DIMS + gap:]], axis=-1)


def kernel(x, positions, norm_g, a_w_in, a_conv_w, a_log, a_dt_bias, a_norm_g, a_w_out,
           b_w_in, b_q_norm_g, b_k_norm_g, b_w_out):
    bsz, seq, d = x.shape
    t = bsz * seq
    x2 = x.reshape(t, d)

    w_main = a_w_in[0, :, :A_MAIN].astype(BF16)
    w_gate_t = a_w_in[0, :, A_MAIN:].T.astype(BF16)
    zeros8 = jnp.zeros((A_HEADS,), F32)
    alog16 = jnp.concatenate([zeros8, a_log[0].astype(F32)])[:, None]
    dtb16 = jnp.concatenate([zeros8, a_dt_bias[0].astype(F32)])[:, None]
    proj, gates_t = _gdn_proj(x2, norm_g[0][None, :], w_main, w_gate_t, alog16, dtb16,
                              a_conv_w[0], seq)
    x3 = _gdn_core(proj.reshape(bsz, seq, A_MAIN), gates_t, a_norm_g[0][None, :],
                   a_w_out[0].astype(BF16), x)
    x2 = x3.reshape(t, d)

    wb = b_w_in[0]
    n_qkv = 3 * B_GROUPS * B_W
    w5 = wb[:, :n_qkv].reshape(d, 3, B_GROUPS, B_HEADS, B_DH)

    def group_weights(gi):
        parts = [_rotary_lane_order(w5[:, 0, gi]), _rotary_lane_order(w5[:, 1, gi]), w5[:, 2, gi]]
        return jnp.concatenate([p.reshape(d, B_W) for p in parts], axis=1)

    w0 = jnp.concatenate([group_weights(0), wb[:, n_qkv:]], axis=1).astype(BF16)
    w1 = group_weights(1).astype(BF16)
    w2 = group_weights(2).astype(BF16)
    q_gain = _rotary_lane_order(b_q_norm_g[0]) * (B_DH ** -0.5)
    k_gain = _rotary_lane_order(b_k_norm_g[0])
    cos, sin = _rope_tables(positions)
    p0, p1, p2 = _attn_proj(x2, norm_g[1][None, :], w0, w1, w2, cos, sin, q_gain, k_gain, bsz, seq)
    o = _attn_core(p0.reshape(bsz, seq, -1), p1.reshape(bsz, seq, -1), p2.reshape(bsz, seq, -1))
    x2 = _out_proj(o.reshape(t, B_W), b_w_out[0].astype(BF16), x2, "attn_out_proj")
    return x2.reshape(bsz, seq, d)
```

```python
import functools

import jax
import jax.numpy as jnp
from jax import lax
from jax.experimental import pallas as pl
from jax.experimental.pallas import tpu as pltpu

F32 = jnp.float32
BF16 = jnp.bfloat16
EPS = 1e-6

D_MODEL = 1024
A_HEADS = 8
A_DK = 128
A_DV = 256
A_QK = A_HEADS * A_DK
A_VW = A_HEADS * A_DV
A_CONVW = 2 * A_QK + A_VW
A_MAIN = A_CONVW + A_VW
A_CONV = 4
CHUNK = 128
GROUP = 16
OUT_BATCH = 4
NEUMANN_BLOCK = 8
B_WINDOWS = (128, 512, 2048)
B_DILATIONS = (1, 4, 16)
B_GROUPS = 3
B_HEADS = 8
B_DH = 128
B_W = B_HEADS * B_DH
B_BLOCK = 128
ROPE_THETA = 500000.0
ROPE_DIMS = B_DH // 4
ROPE_HALF = ROPE_DIMS // 2
ROPE_SHIFT = B_DH // 2
ATTN_BATCH = 4
ATTN_LOOKAHEAD = 2

VMEM_LIMIT_BYTES = 56 * 1024 * 1024
OUT_ROW_TILE = 1024
GDN_ROW_TILE = 256
ATTN_ROW_TILE = 256
COL_TILE = 512
HALO = 16


def _split3(a):
    hi = a.astype(BF16)
    r = a - hi.astype(F32)
    mid = r.astype(BF16)
    lo = (r - mid.astype(F32)).astype(BF16)
    return hi, mid, lo


def _bdot(a, b):
    return jnp.einsum("cik,ckj->cij", a, b, preferred_element_type=F32)


def _normed_f32(x_ref, g_ref):
    x = x_ref[...]
    ms = jnp.mean(x * x, axis=-1, keepdims=True)
    return x * lax.rsqrt(ms + EPS) * g_ref[...]


def _resident(shape):
    return pl.BlockSpec(shape, lambda i: (0,) * len(shape), pipeline_mode=pl.Buffered(1))


def _gdn_proj_kernel(x_ref, xh_ref, g_ref, w_ref, wgt_ref, alog_ref, dtb_ref, cw_ref,
                     o_ref, gates_ref, *, tiles_per_seq):
    tm = x_ref.shape[0]
    first = (pl.program_id(0) % tiles_per_seq) == 0
    hn = _normed_f32(x_ref, g_ref).astype(BF16)
    halo = jnp.where(first, 0.0, _normed_f32(xh_ref, g_ref)).astype(BF16)
    hcat = jnp.concatenate([halo, hn], axis=0)

    for n0 in range(0, A_CONVW, COL_TILE):
        y = jnp.dot(hcat, w_ref[:, n0:n0 + COL_TILE], preferred_element_type=F32)
        cw = cw_ref[:, n0:n0 + COL_TILE]
        acc = y * cw[A_CONV - 1:A_CONV, :]
        for tap in range(A_CONV - 1):
            acc = acc + pltpu.roll(y, A_CONV - 1 - tap, 0) * cw[tap:tap + 1, :]
        o_ref[:, n0:n0 + COL_TILE] = acc[HALO:, :].astype(o_ref.dtype)

    for n0 in range(A_CONVW, A_MAIN, COL_TILE):
        o_ref[:, n0:n0 + COL_TILE] = jnp.dot(
            hn, w_ref[:, n0:n0 + COL_TILE], preferred_element_type=F32).astype(o_ref.dtype)

    logits = lax.dot_general(wgt_ref[...], hn, (((1,), (1,)), ((), ())), preferred_element_type=F32)
    row = lax.broadcasted_iota(jnp.int32, logits.shape, 0)
    xs = logits + dtb_ref[...]
    softplus = jnp.maximum(xs, 0.0) + jnp.log1p(jnp.exp(-jnp.abs(xs)))
    decay = -jnp.exp(alog_ref[...]) * softplus
    gates_ref[...] = jnp.where(row < A_HEADS, jax.nn.sigmoid(logits), decay)


def _gdn_proj(x2, g, w, wgt, alog16, dtb16, conv_w, seq):
    t, d = x2.shape
    tm = GDN_ROW_TILE
    kern = functools.partial(_gdn_proj_kernel, tiles_per_seq=seq // tm)
    return pl.pallas_call(
        kern,
        grid=(t // tm,),
        in_specs=[
            pl.BlockSpec((tm, d), lambda i: (i, 0)),
            pl.BlockSpec((HALO, d), lambda i: (jnp.maximum(i * (tm // HALO) - 1, 0), 0)),
            _resident((1, d)),
            _resident((d, A_MAIN)),
            _resident((2 * A_HEADS, d)),
            _resident((2 * A_HEADS, 1)),
            _resident((2 * A_HEADS, 1)),
            _resident((A_CONV, A_CONVW)),
        ],
        out_specs=[
            pl.BlockSpec((tm, A_MAIN), lambda i: (i, 0)),
            pl.BlockSpec((2 * A_HEADS, tm), lambda i: (0, i)),
        ],
        out_shape=[
            jax.ShapeDtypeStruct((t, A_MAIN), BF16),
            jax.ShapeDtypeStruct((2 * A_HEADS, t), F32),
        ],
        compiler_params=pltpu.CompilerParams(
            dimension_semantics=("parallel",), vmem_limit_bytes=VMEM_LIMIT_BYTES),
        name="gdn_norm_proj",
    )(x2, x2, g, w, wgt, alog16, dtb16, conv_w)


def _out_proj_kernel(o_ref, w_ref, x_ref, y_ref):
    y_ref[...] = x_ref[...] + jnp.dot(o_ref[...], w_ref[...], preferred_element_type=F32)


def _out_proj(o2, w, x2, name):
    t, k = o2.shape
    d = w.shape[1]
    tm = OUT_ROW_TILE
    return pl.pallas_call(
        _out_proj_kernel,
        grid=(t // tm,),
        in_specs=[
            pl.BlockSpec((tm, k), lambda i: (i, 0)),
            _resident((k, d)),
            pl.BlockSpec((tm, d), lambda i: (i, 0)),
        ],
        out_specs=pl.BlockSpec((tm, d), lambda i: (i, 0)),
        out_shape=jax.ShapeDtypeStruct((t, d), F32),
        compiler_params=pltpu.CompilerParams(
            dimension_semantics=("parallel",), vmem_limit_bytes=VMEM_LIMIT_BYTES),
        name=name,
    )(o2, w, x2)


def _gdn_kernel(q_ref, k_ref, v_ref, z_ref, gt_ref, ng_ref, wo_ref, x_ref, y_ref, s_ref, rf_ref):
    head = pl.program_id(1)
    rows = GROUP * CHUNK
    n_groups = q_ref.shape[0] // rows
    c = CHUNK

    ii = lax.broadcasted_iota(jnp.int32, (c, c), 0)
    jj = lax.broadcasted_iota(jnp.int32, (c, c), 1)
    eye = jnp.where(ii == jj, 1.0, 0.0).astype(F32)
    lower_incl = ii >= jj
    lower_strict = ii > jj
    cum_rhs = jnp.where(ii <= jj, 1.0, 0.0).astype(BF16)

    def same_block(size):
        return (ii // size) == (jj // size)

    def col(x):
        return jnp.swapaxes(jnp.broadcast_to(x[:, None, :], (GROUP, c, c)), 1, 2)

    s_ref[...] = jnp.zeros_like(s_ref)

    n_chunks = q_ref.shape[0] // c

    def chunk_rows(r):
        full = gt_ref[pl.ds(r, 1), :]
        return jnp.concatenate([full[:, n * c:(n + 1) * c] for n in range(n_chunks)], axis=0)

    beta_all = chunk_rows(head)
    gh, gm, gl = _split3(chunk_rows(head + A_HEADS))
    gc_all = (jnp.dot(gh, cum_rhs, preferred_element_type=F32)
              + jnp.dot(gm, cum_rhs, preferred_element_type=F32)
              + jnp.dot(gl, cum_rhs, preferred_element_type=F32))
    exp_all = jnp.exp(gc_all)
    last_all = jnp.broadcast_to(gc_all[:, c - 1:c], gc_all.shape)
    rowforms = (beta_all, gc_all, exp_all, beta_all * exp_all, jnp.exp(last_all - gc_all),
                jnp.exp(last_all))
    for idx, val in enumerate(rowforms):
        for n in range(n_groups):
            rf_ref[idx, n] = val[n * GROUP:(n + 1) * GROUP, :]

    def chunk_local(g):
        r0 = g * rows
        def silu_rows(ref):
            a = ref[pl.ds(r0, rows), :].astype(F32)
            return a * jax.nn.sigmoid(a)

        def l2n(a):
            return a * lax.rsqrt(jnp.sum(a * a, axis=-1, keepdims=True) + EPS)

        q3 = (l2n(silu_rows(q_ref)) * (A_DK ** -0.5)).reshape(GROUP, c, A_DK)
        k3 = l2n(silu_rows(k_ref)).reshape(GROUP, c, A_DK)
        v3 = silu_rows(v_ref).reshape(GROUP, c, A_DV)
        q3b = q3.astype(BF16)
        k3b = k3.astype(BF16)

        gc = rf_ref[1, g]
        beta_c = col(rf_ref[0, g])
        gc_c = col(gc)
        exp_gc_c = col(rf_ref[2, g])
        beta_exp_c = col(rf_ref[3, g])
        tail_c = col(rf_ref[4, g])
        e_last = rf_ref[5, g]
        e_last = jnp.concatenate([e_last, e_last], axis=-1)
        decay = jnp.exp(jnp.where(lower_incl[None], gc_c - gc[:, None, :], -jnp.inf))

        kb3 = (k3 * beta_c).astype(BF16)
        kk = jnp.einsum("cid,cjd->cij", kb3, k3b, preferred_element_type=F32)
        qk = jnp.einsum("cid,cjd->cij", q3b, k3b, preferred_element_type=F32) * decay
        low = jnp.where(lower_strict[None], kk * decay, 0.0)

        assert NEUMANN_BLOCK == 8
        m1 = jnp.where(same_block(NEUMANN_BLOCK)[None], -low, 0.0)
        m1b = m1.astype(BF16)
        tinv = eye[None] + m1
        m2b = _bdot(m1b, m1b).astype(BF16)
        tinv = tinv + _bdot(tinv.astype(BF16), m2b)
        m4b = _bdot(m2b, m2b).astype(BF16)
        tinv = tinv + _bdot(tinv.astype(BF16), m4b)
        size = NEUMANN_BLOCK
        while size < c:
            off = jnp.where((same_block(2 * size) & jnp.logical_not(same_block(size)))[None], low, 0.0)
            tb = tinv.astype(BF16)
            tinv = tinv - _bdot(_bdot(tb, off.astype(BF16)).astype(BF16), tb)
            size *= 2

        tb = tinv.astype(BF16)
        vb = (v3 * jnp.concatenate([beta_c, beta_c], axis=-1)).astype(BF16)
        kbg = (k3 * beta_exp_c).astype(BF16)
        u = _bdot(tb, vb)
        w = _bdot(tb, kbg)

        qt = q3 * exp_gc_c
        kd = (k3 * tail_c).astype(BF16)
        ub = u.astype(BF16)
        wb = w.astype(BF16)
        qkb = qk.astype(BF16)
        kdt = jnp.swapaxes(kd, 1, 2)
        a_neg = -_bdot(kdt, wb)
        b_add = _bdot(kdt, ub)
        q_eff = qt - _bdot(qkb, wb)
        o_loc = _bdot(qkb, ub)
        p_all = jnp.concatenate([a_neg, q_eff], axis=1).astype(BF16)
        return p_all, b_add, o_loc, e_last

    def project_out(row, gated):
        lhs = jnp.concatenate(gated, axis=0)
        y_ref[pl.ds(row, lhs.shape[0]), :] += jnp.dot(lhs, wo_ref[...], preferred_element_type=F32)

    def state_steps(g, p_all, b_add, o_loc, e_last):
        ng = ng_ref[...]
        ready = None
        gated = []
        for cc in range(GROUP):
            state = s_ref[...]
            r = jnp.dot(p_all[cc], state.astype(BF16), preferred_element_type=F32)
            o = r[A_DK:, :] + o_loc[cc]
            s_ref[...] = state * e_last[cc:cc + 1, :] + r[:A_DK, :] + b_add[cc]
            if ready is not None:
                project_out(*ready)
                ready = None
            on = o * lax.rsqrt(jnp.mean(o * o, axis=-1, keepdims=True) + EPS) * ng
            row = g * rows + cc * c
            zc = z_ref[pl.ds(row, c), :].astype(F32)
            gated.append((on * (zc * jax.nn.sigmoid(zc))).astype(BF16))
            if len(gated) == OUT_BATCH:
                ready = (row - (OUT_BATCH - 1) * c, gated)
                gated = []
        project_out(*ready)

    @pl.when(head == 0)
    def _():
        y_ref[...] = x_ref[...]

    local = chunk_local(0)
    for g in range(n_groups):
        nxt = chunk_local(g + 1) if g + 1 < n_groups else None
        state_steps(g, *local)
        local = nxt


def _gdn_core(proj3, gates_t, norm_g, w_out, x3):
    bsz, seq, _ = proj3.shape
    d = x3.shape[-1]
    kb = A_QK // A_DK
    vb = 2 * A_QK // A_DV
    zb = vb + A_VW // A_DV
    return pl.pallas_call(
        _gdn_kernel,
        grid=(bsz, A_HEADS),
        in_specs=[
            pl.BlockSpec((None, seq, A_DK), lambda b, h: (b, 0, h)),
            pl.BlockSpec((None, seq, A_DK), lambda b, h: (b, 0, kb + h)),
            pl.BlockSpec((None, seq, A_DV), lambda b, h: (b, 0, vb + h)),
            pl.BlockSpec((None, seq, A_DV), lambda b, h: (b, 0, zb + h)),
            pl.BlockSpec((2 * A_HEADS, seq), lambda b, h: (0, b)),
            pl.BlockSpec((1, A_DV), lambda b, h: (0, 0)),
            pl.BlockSpec((A_DV, d), lambda b, h: (h, 0)),
            pl.BlockSpec((None, seq, d), lambda b, h: (b, 0, 0)),
        ],
        out_specs=pl.BlockSpec((None, seq, d), lambda b, h: (b, 0, 0)),
        out_shape=jax.ShapeDtypeStruct((bsz, seq, d), F32),
        scratch_shapes=[
            pltpu.VMEM((A_DK, A_DV), F32),
            pltpu.VMEM((6, seq // (GROUP * CHUNK), GROUP, CHUNK), F32),
        ],
        compiler_params=pltpu.CompilerParams(
            dimension_semantics=("parallel", "arbitrary"), vmem_limit_bytes=VMEM_LIMIT_BYTES),
        name="gdn_core",
    )(proj3, proj3, proj3, proj3, gates_t, norm_g, w_out, x3)


def _rope_table_kernel(pos_ref, invf_ref, cos_ref, sin_ref):
    tr, per_row = pos_ref.shape
    nf = ROPE_HALF

    def dot_split(a, b, terms):
        parts = _split3(a)[:terms]
        out = jnp.dot(parts[0], b, preferred_element_type=F32)
        for part in parts[1:]:
            out = out + jnp.dot(part, b, preferred_element_type=F32)
        return out

    rk = lax.broadcasted_iota(jnp.int32, (per_row, B_DH), 0)
    rl = lax.broadcasted_iota(jnp.int32, (per_row, B_DH), 1)
    rep = jnp.where(rl // nf == rk, 1.0, 0.0).astype(BF16)
    ang = dot_split(pos_ref[...].astype(F32), rep, 3) * invf_ref[...]
    c = jnp.cos(ang)
    s = jnp.sin(ang)

    ek = lax.broadcasted_iota(jnp.int32, (B_DH, B_DH), 0)
    el = lax.broadcasted_iota(jnp.int32, (B_DH, B_DH), 1)
    lane = lax.broadcasted_iota(jnp.int32, (1, B_DH), 1)
    rotary = (lane < nf) | ((lane >= ROPE_SHIFT) & (lane < ROPE_SHIFT + nf))
    base = jnp.where(rotary, 0.0, 1.0)
    for p in range(per_row):
        mine = ek // nf == p
        first = mine & (el == ek % nf)
        second = mine & (el == ek % nf + ROPE_SHIFT)
        e_cos = jnp.where(first | second, 1.0, 0.0).astype(BF16)
        e_sin = (jnp.where(second, 1.0, 0.0) - jnp.where(first, 1.0, 0.0)).astype(BF16)
        rows = pl.ds(p, tr, stride=per_row)
        cos_ref[rows, :] = dot_split(c, e_cos, 2) + base
        sin_ref[rows, :] = dot_split(s, e_sin, 2)


def _rope_tables(positions):
    t = positions.size
    per_row = B_DH // ROPE_HALF
    inv_freq = ROPE_THETA ** (-jnp.arange(0, ROPE_DIMS, 2, dtype=F32) / ROPE_DIMS)
    invf = jnp.tile(inv_freq, per_row)[None, :]
    rows = t // per_row
    tr = min(rows, 512)
    return pl.pallas_call(
        _rope_table_kernel,
        grid=(rows // tr,),
        in_specs=[pl.BlockSpec((tr, per_row), lambda i: (i, 0)), pl.BlockSpec((1, B_DH), lambda i: (0, 0))],
        out_specs=[pl.BlockSpec((tr * per_row, B_DH), lambda i: (i, 0))] * 2,
        out_shape=[jax.ShapeDtypeStruct((t, B_DH), F32)] * 2,
        name="rope_tables",
    )(positions.reshape(rows, per_row), invf)


def _attn_proj_kernel(x_ref, g_ref, w0_ref, w1_ref, w2_ref, cos_ref, sin_ref, qg_ref, kg_ref,
                      p0_ref, p1_ref, p2_ref, hn_ref):
    tm = x_ref.shape[0]
    n_lane_blocks = hn_ref.shape[0]
    hn_nat = _normed_f32(x_ref, g_ref)
    for j in range(n_lane_blocks):
        hn_ref[j] = hn_nat[:, j * B_DH:(j + 1) * B_DH]

    def by_stream(ref, dil):
        if dil == 1:
            return ref[...]
        return jnp.concatenate([ref[pl.ds(r, tm // dil, stride=dil), :] for r in range(dil)], axis=0)

    def hn_by_stream(dil):
        if dil == 1:
            return hn_nat
        return jnp.concatenate([by_stream(hn_ref.at[j], dil) for j in range(n_lane_blocks)], axis=-1)

    def norm_rope(y, gain, cos, sin):
        y = y * lax.rsqrt(jnp.mean(y * y, axis=-1, keepdims=True) + EPS) * gain
        return y * cos + pltpu.roll(y, ROPE_SHIFT, 1) * sin

    w_refs = (w0_ref, w1_ref, w2_ref)
    for gi in range(B_GROUPS):
        dil = B_DILATIONS[gi]
        hn = hn_by_stream(dil).astype(BF16)
        cos = by_stream(cos_ref, dil)
        sin = by_stream(sin_ref, dil)
        w_ref = w_refs[gi]
        for n0 in range(0, w_ref.shape[1], COL_TILE):
            y = jnp.dot(hn, w_ref[:, n0:n0 + COL_TILE], preferred_element_type=F32)
            if n0 < 2 * B_W:
                gain = (qg_ref if n0 < B_W else kg_ref)[gi:gi + 1, :]
                y = jnp.concatenate(
                    [norm_rope(y[:, j * B_DH:(j + 1) * B_DH], gain, cos, sin)
                     for j in range(COL_TILE // B_DH)], axis=-1)
            y = y.astype(BF16)
            if gi == 0:
                p0_ref[:, n0:n0 + COL_TILE] = y
            else:
                out_ref = p1_ref if gi == 1 else p2_ref
                out_ref[:, :, n0:n0 + COL_TILE] = y.reshape(dil, tm // dil, COL_TILE)


def _attn_proj(x2, g, w0, w1, w2, cos, sin, q_gain, k_gain, bsz, seq):
    t, d = x2.shape
    tm = ATTN_ROW_TILE
    per_seq = seq // tm
    d1, d2 = B_DILATIONS[1], B_DILATIONS[2]
    return pl.pallas_call(
        _attn_proj_kernel,
        grid=(t // tm,),
        in_specs=[
            pl.BlockSpec((tm, d), lambda i: (i, 0)),
            _resident((1, d)),
            _resident(w0.shape),
            _resident(w1.shape),
            _resident(w2.shape),
            pl.BlockSpec((tm, B_DH), lambda i: (i, 0)),
            pl.BlockSpec((tm, B_DH), lambda i: (i, 0)),
            _resident((B_GROUPS, B_DH)),
            _resident((B_GROUPS, B_DH)),
        ],
        out_specs=[
            pl.BlockSpec((tm, w0.shape[1]), lambda i: (i, 0)),
            pl.BlockSpec((None, d1, tm // d1, w1.shape[1]), lambda i: (i // per_seq, 0, i % per_seq, 0)),
            pl.BlockSpec((None, d2, tm // d2, w2.shape[1]), lambda i: (i // per_seq, 0, i % per_seq, 0)),
        ],
        out_shape=[
            jax.ShapeDtypeStruct((t, w0.shape[1]), BF16),
            jax.ShapeDtypeStruct((bsz, d1, seq // d1, w1.shape[1]), BF16),
            jax.ShapeDtypeStruct((bsz, d2, seq // d2, w2.shape[1]), BF16),
        ],
        scratch_shapes=[pltpu.VMEM((d // B_DH, tm, B_DH), F32)],
        compiler_params=pltpu.CompilerParams(
            dimension_semantics=("parallel",), vmem_limit_bytes=VMEM_LIMIT_BYTES),
        name="attn_norm_proj",
    )(x2, g, w0, w1, w2, cos, sin, q_gain, k_gain)


def _attn_kernel(q0_ref, k0_ref, v0_ref, q1_ref, k1_ref, v1_ref, q2_ref, k2_ref, v2_ref, z_ref,
                 o_ref, og_ref, lg_ref):
    seq = q0_ref.shape[0]
    blk = B_BLOCK
    nb = ATTN_BATCH
    tile = nb * blk
    assert all(w // d == blk for w, d in zip(B_WINDOWS, B_DILATIONS))

    qi = lax.broadcasted_iota(jnp.int32, (nb, blk, 2 * blk), 1)
    kj = lax.broadcasted_iota(jnp.int32, (nb, blk, 2 * blk), 2)
    bi = lax.broadcasted_iota(jnp.int32, (nb, blk, 2 * blk), 0)
    band = ((kj < blk) & (kj >= qi)) | ((kj >= blk) & (kj - blk <= qi))
    band_first = band & ((kj >= blk) | (bi > 0))
    cur_only = (lax.broadcasted_iota(jnp.int32, (nb, blk, blk), 2)
                <= lax.broadcasted_iota(jnp.int32, (nb, blk, blk), 1))
    ones_v = jnp.ones((nb, blk, B_DH), BF16)
    zero_blk = jnp.zeros((1, blk, B_DH), BF16)

    q_refs = (q0_ref, q1_ref, q2_ref)
    k_refs = (k0_ref, k1_ref, k2_ref)
    v_refs = (v0_ref, v1_ref, v2_ref)

    def blocks(ref, first, count):
        return ref[pl.ds(first, count * blk), :].reshape(count, blk, B_DH)

    def scores(gi, first, starts_stream, has_prev):
        qb = blocks(q_refs[gi], first, nb)
        kc = blocks(k_refs[gi], first, nb)
        if not has_prev:
            s = jnp.einsum("bqd,bkd->bqk", qb, kc, preferred_element_type=F32)
            return jnp.where(cur_only, s, -jnp.inf)
        if starts_stream:
            kp = jnp.concatenate([zero_blk, kc[:nb - 1]], axis=0)
        else:
            kp = blocks(k_refs[gi], first - blk, nb)
        s = jnp.einsum("bqd,bkd->bqk", qb, jnp.concatenate([kp, kc], axis=1),
                       preferred_element_type=F32)
        return jnp.where(band_first if starts_stream else band, s, -jnp.inf)

    def finish(gi, first, starts_stream, has_prev, out_rows, s):
        m = jnp.max(s, axis=-1, keepdims=True)
        p = jnp.exp(s - m).astype(BF16)
        vc = jnp.concatenate([blocks(v_refs[gi], first, nb), ones_v], axis=-1)
        if has_prev:
            if starts_stream:
                vp = jnp.concatenate([jnp.zeros((1, blk, 2 * B_DH), BF16), vc[:nb - 1]], axis=0)
            else:
                vp = jnp.concatenate([blocks(v_refs[gi], first - blk, nb), ones_v], axis=-1)
            vc = jnp.concatenate([vp, vc], axis=1)
        acc = jnp.einsum("bqk,bkd->bqd", p, vc, preferred_element_type=F32)
        den = acc[..., B_DH:]
        o = acc[..., :B_DH] / den
        lse = m + jnp.log(den)
        b0 = 0
        for rows, count in out_rows:
            og_ref[gi, rows, :] = o[b0:b0 + count].reshape(count * blk, B_DH)
            lg_ref[gi, rows, :] = lse[b0:b0 + count].reshape(count * blk, B_DH)
            b0 += count

    batches = []
    for gi in range(B_GROUPS):
        dil = B_DILATIONS[gi]
        length = seq // dil
        per_stream = length // blk
        for first in range(0, seq, tile):
            r, m0 = divmod(first, length)
            if per_stream >= nb:
                t0 = r + m0 * dil
                rows = pl.ds(t0, tile) if dil == 1 else pl.ds(t0, tile, stride=dil)
                batches.append((gi, first, m0 == 0, True, [(rows, nb)]))
            else:
                assert per_stream == 1
                rows = [(pl.ds(r + j, blk, stride=dil), 1) for j in range(nb)]
                batches.append((gi, first, True, False, rows))

    pending = {}
    for i in range(len(batches) + ATTN_LOOKAHEAD):
        if i < len(batches):
            pending[i] = scores(*batches[i][:4])
        if i >= ATTN_LOOKAHEAD:
            j = i - ATTN_LOOKAHEAD
            finish(*batches[j], pending.pop(j))

    def merge(t, carry):
        r0 = pl.multiple_of(t * tile, tile)
        l0 = lg_ref[0, pl.ds(r0, tile), :]
        l1 = lg_ref[1, pl.ds(r0, tile), :]
        l2 = lg_ref[2, pl.ds(r0, tile), :]
        m = jnp.maximum(jnp.maximum(l0, l1), l2)
        w0 = jnp.exp(l0 - m)
        w1 = jnp.exp(l1 - m)
        w2 = jnp.exp(l2 - m)
        num = (w0 * og_ref[0, pl.ds(r0, tile), :] + w1 * og_ref[1, pl.ds(r0, tile), :]
               + w2 * og_ref[2, pl.ds(r0, tile), :])
        o = num / (w0 + w1 + w2)
        z = z_ref[pl.ds(r0, tile), :].astype(F32)
        o_ref[pl.ds(r0, tile), :] = (o * (z * jax.nn.sigmoid(z))).astype(o_ref.dtype)
        return carry

    lax.fori_loop(0, seq // tile, merge, 0)


def _attn_core(p0, p1, p2):
    bsz, seq, _ = p0.shape

    def cols(which):
        return pl.BlockSpec((None, seq, B_DH), lambda b, h: (b, 0, which * B_HEADS + h))

    qkv = [cols(0), cols(1), cols(2)]
    return pl.pallas_call(
        _attn_kernel,
        grid=(bsz, B_HEADS),
        in_specs=[*qkv, *qkv, *qkv, cols(3)],
        out_specs=pl.BlockSpec((None, seq, B_DH), lambda b, h: (b, 0, h)),
        out_shape=jax.ShapeDtypeStruct((bsz, seq, B_W), BF16),
        scratch_shapes=[
            pltpu.VMEM((B_GROUPS, seq, B_DH), F32),
            pltpu.VMEM((B_GROUPS, seq, B_DH), F32),
        ],
        compiler_params=pltpu.CompilerParams(
            dimension_semantics=("parallel", "parallel"), vmem_limit_bytes=VMEM_LIMIT_BYTES),
        name="dilated_attn",
    )(p0, p0, p0, p1, p1, p1, p2, p2, p2, p0)


def _rotary_lane_order(a):
    gap = ROPE_SHIFT - ROPE_HALF
    return jnp.concatenate(
        [a[..., :ROPE_HALF], a[..., ROPE_DIMS:ROPE_DIMS + gap], a[..., ROPE_HALF:ROPE_DIMS],
         a[..., ROPE_DIMS + gap:]], axis=-1)


def kernel(x, positions, norm_g, a_w_in, a_conv_w, a_log, a_dt_bias, a_norm_g, a_w_out,
           b_w_in, b_q_norm_g, b_k_norm_g, b_w_out):
    bsz, seq, d = x.shape
    t = bsz * seq
    x2 = x.reshape(t, d)

    w_main = a_w_in[0, :, :A_MAIN].astype(BF16)
    w_gate_t = a_w_in[0, :, A_MAIN:].T.astype(BF16)
    zeros8 = jnp.zeros((A_HEADS,), F32)
    alog16 = jnp.concatenate([zeros8, a_log[0].astype(F32)])[:, None]
    dtb16 = jnp.concatenate([zeros8, a_dt_bias[0].astype(F32)])[:, None]
    proj, gates_t = _gdn_proj(x2, norm_g[0][None, :], w_main, w_gate_t, alog16, dtb16,
                              a_conv_w[0], seq)
    x3 = _gdn_core(proj.reshape(bsz, seq, A_MAIN), gates_t, a_norm_g[0][None, :],
                   a_w_out[0].astype(BF16), x)
    x2 = x3.reshape(t, d)

    wb = b_w_in[0]
    n_qkv = 3 * B_GROUPS * B_W
    w5 = wb[:, :n_qkv].reshape(d, 3, B_GROUPS, B_HEADS, B_DH)
    w5 = jnp.concatenate([_rotary_lane_order(w5[:, :2]), w5[:, 2:]], axis=1)

    def group_weights(gi):
        return w5[:, :, gi].reshape(d, 3 * B_W)

    w0 = jnp.concatenate([group_weights(0), wb[:, n_qkv:]], axis=1).astype(BF16)
    w1 = group_weights(1).astype(BF16)
    w2 = group_weights(2).astype(BF16)
    q_gain = _rotary_lane_order(b_q_norm_g[0]) * (B_DH ** -0.5)
    k_gain = _rotary_lane_order(b_k_norm_g[0])
    cos, sin = _rope_tables(positions)
    p0, p1, p2 = _attn_proj(x2, norm_g[1][None, :], w0, w1, w2, cos, sin, q_gain, k_gain, bsz, seq)
    o = _attn_core(p0.reshape(bsz, seq, -1), p1.reshape(bsz, seq, -1), p2.reshape(bsz, seq, -1))
    x2 = _out_proj(o.reshape(t, B_W), b_w_out[0].astype(BF16), x2, "attn_out_proj")
    return x2.reshape(bsz, seq, d)
```

```python
import functools

import jax
import jax.numpy as jnp
from jax import lax
from jax.experimental import pallas as pl
from jax.experimental.pallas import tpu as pltpu

F32 = jnp.float32
BF16 = jnp.bfloat16
EPS = 1e-6

D_MODEL = 1024
A_HEADS = 8
A_DK = 128
A_DV = 256
A_QK = A_HEADS * A_DK
A_VW = A_HEADS * A_DV
A_CONVW = 2 * A_QK + A_VW
A_MAIN = A_CONVW + A_VW
A_CONV = 4
CHUNK = 128
GROUP = 16
OUT_BATCH = 4
NEUMANN_BLOCK = 8
B_WINDOWS = (128, 512, 2048)
B_DILATIONS = (1, 4, 16)
B_GROUPS = 3
B_HEADS = 8
B_DH = 128
B_W = B_HEADS * B_DH
B_BLOCK = 128
ROPE_THETA = 500000.0
ROPE_DIMS = B_DH // 4
ROPE_HALF = ROPE_DIMS // 2
ROPE_SHIFT = B_DH // 2
ATTN_BATCH = 4
ATTN_LOOKAHEAD = 2

VMEM_LIMIT_BYTES = 56 * 1024 * 1024
OUT_ROW_TILE = 1024
GDN_ROW_TILE = 256
ATTN_ROW_TILE = 256
COL_TILE = 512
HALO = 16


def _split3(a):
    hi = a.astype(BF16)
    r = a - hi.astype(F32)
    mid = r.astype(BF16)
    lo = (r - mid.astype(F32)).astype(BF16)
    return hi, mid, lo


def _bdot(a, b):
    return jnp.einsum("cik,ckj->cij", a, b, preferred_element_type=F32)


def _normed_f32(x_ref, g_ref):
    x = x_ref[...]
    ms = jnp.mean(x * x, axis=-1, keepdims=True)
    return x * lax.rsqrt(ms + EPS) * g_ref[...]


def _resident(shape):
    return pl.BlockSpec(shape, lambda i: (0,) * len(shape), pipeline_mode=pl.Buffered(1))


def _gdn_proj_kernel(x_ref, xh_ref, g_ref, w_ref, wgt_ref, alog_ref, dtb_ref, cw_ref,
                     o_ref, gates_ref, *, tiles_per_seq):
    tm = x_ref.shape[0]
    first = (pl.program_id(0) % tiles_per_seq) == 0
    hn = _normed_f32(x_ref, g_ref).astype(BF16)
    halo = jnp.where(first, 0.0, _normed_f32(xh_ref, g_ref)).astype(BF16)
    hcat = jnp.concatenate([halo, hn], axis=0)

    for n0 in range(0, A_CONVW, COL_TILE):
        y = jnp.dot(hcat, w_ref[:, n0:n0 + COL_TILE], preferred_element_type=F32)
        cw = cw_ref[:, n0:n0 + COL_TILE]
        acc = y * cw[A_CONV - 1:A_CONV, :]
        for tap in range(A_CONV - 1):
            acc = acc + pltpu.roll(y, A_CONV - 1 - tap, 0) * cw[tap:tap + 1, :]
        o_ref[:, n0:n0 + COL_TILE] = acc[HALO:, :].astype(o_ref.dtype)

    for n0 in range(A_CONVW, A_MAIN, COL_TILE):
        o_ref[:, n0:n0 + COL_TILE] = jnp.dot(
            hn, w_ref[:, n0:n0 + COL_TILE], preferred_element_type=F32).astype(o_ref.dtype)

    logits = lax.dot_general(wgt_ref[...], hn, (((1,), (1,)), ((), ())), preferred_element_type=F32)
    row = lax.broadcasted_iota(jnp.int32, logits.shape, 0)
    xs = logits + dtb_ref[...]
    softplus = jnp.maximum(xs, 0.0) + jnp.log1p(jnp.exp(-jnp.abs(xs)))
    decay = -jnp.exp(alog_ref[...]) * softplus
    gates_ref[...] = jnp.where(row < A_HEADS, jax.nn.sigmoid(logits), decay)


def _gdn_proj(x2, g, w, wgt, alog16, dtb16, conv_w, seq):
    t, d = x2.shape
    tm = GDN_ROW_TILE
    kern = functools.partial(_gdn_proj_kernel, tiles_per_seq=seq // tm)
    return pl.pallas_call(
        kern,
        grid=(t // tm,),
        in_specs=[
            pl.BlockSpec((tm, d), lambda i: (i, 0)),
            pl.BlockSpec((HALO, d), lambda i: (jnp.maximum(i * (tm // HALO) - 1, 0), 0)),
            _resident((1, d)),
            _resident((d, A_MAIN)),
            _resident((2 * A_HEADS, d)),
            _resident((2 * A_HEADS, 1)),
            _resident((2 * A_HEADS, 1)),
            _resident((A_CONV, A_CONVW)),
        ],
        out_specs=[
            pl.BlockSpec((tm, A_MAIN), lambda i: (i, 0)),
            pl.BlockSpec((2 * A_HEADS, tm), lambda i: (0, i)),
        ],
        out_shape=[
            jax.ShapeDtypeStruct((t, A_MAIN), BF16),
            jax.ShapeDtypeStruct((2 * A_HEADS, t), F32),
        ],
        compiler_params=pltpu.CompilerParams(
            dimension_semantics=("parallel",), vmem_limit_bytes=VMEM_LIMIT_BYTES),
        name="gdn_norm_proj",
    )(x2, x2, g, w, wgt, alog16, dtb16, conv_w)


def _out_proj_kernel(o_ref, w_ref, x_ref, y_ref):
    y_ref[...] = x_ref[...] + jnp.dot(o_ref[...], w_ref[...], preferred_element_type=F32)


def _out_proj(o2, w, x2, name):
    t, k = o2.shape
    d = w.shape[1]
    tm = OUT_ROW_TILE
    return pl.pallas_call(
        _out_proj_kernel,
        grid=(t // tm,),
        in_specs=[
            pl.BlockSpec((tm, k), lambda i: (i, 0)),
            _resident((k, d)),
            pl.BlockSpec((tm, d), lambda i: (i, 0)),
        ],
        out_specs=pl.BlockSpec((tm, d), lambda i: (i, 0)),
        out_shape=jax.ShapeDtypeStruct((t, d), F32),
        compiler_params=pltpu.CompilerParams(
            dimension_semantics=("parallel",), vmem_limit_bytes=VMEM_LIMIT_BYTES),
        name=name,
    )(o2, w, x2)


def _gdn_kernel(q_ref, k_ref, v_ref, z_ref, gt_ref, ng_ref, wo_ref, x_ref, y_ref, s_ref, rf_ref):
    head = pl.program_id(1)
    rows = GROUP * CHUNK
    n_groups = q_ref.shape[0] // rows
    c = CHUNK

    ii = lax.broadcasted_iota(jnp.int32, (c, c), 0)
    jj = lax.broadcasted_iota(jnp.int32, (c, c), 1)
    eye = jnp.where(ii == jj, 1.0, 0.0).astype(F32)
    lower_incl = ii >= jj
    lower_strict = ii > jj
    cum_rhs = jnp.where(ii <= jj, 1.0, 0.0).astype(BF16)

    def same_block(size):
        return (ii // size) == (jj // size)

    def col(x):
        return jnp.swapaxes(jnp.broadcast_to(x[:, None, :], (GROUP, c, c)), 1, 2)

    s_ref[...] = jnp.zeros_like(s_ref)

    n_chunks = q_ref.shape[0] // c

    def chunk_rows(r):
        full = gt_ref[pl.ds(r, 1), :]
        return jnp.concatenate([full[:, n * c:(n + 1) * c] for n in range(n_chunks)], axis=0)

    beta_all = chunk_rows(head)
    gh, gm, gl = _split3(chunk_rows(head + A_HEADS))
    gc_all = (jnp.dot(gh, cum_rhs, preferred_element_type=F32)
              + jnp.dot(gm, cum_rhs, preferred_element_type=F32)
              + jnp.dot(gl, cum_rhs, preferred_element_type=F32))
    exp_all = jnp.exp(gc_all)
    last_all = jnp.broadcast_to(gc_all[:, c - 1:c], gc_all.shape)
    rowforms = (beta_all, gc_all, exp_all, beta_all * exp_all, jnp.exp(last_all - gc_all),
                jnp.exp(last_all))
    for idx, val in enumerate(rowforms):
        for n in range(n_groups):
            rf_ref[idx, n] = val[n * GROUP:(n + 1) * GROUP, :]

    def chunk_local(g):
        r0 = g * rows
        def silu_rows(ref):
            a = ref[pl.ds(r0, rows), :].astype(F32)
            return a * jax.nn.sigmoid(a)

        def l2n(a):
            return a * lax.rsqrt(jnp.sum(a * a, axis=-1, keepdims=True) + EPS)

        q3 = (l2n(silu_rows(q_ref)) * (A_DK ** -0.5)).reshape(GROUP, c, A_DK)
        k3 = l2n(silu_rows(k_ref)).reshape(GROUP, c, A_DK)
        v3 = silu_rows(v_ref).reshape(GROUP, c, A_DV)
        q3b = q3.astype(BF16)
        k3b = k3.astype(BF16)

        gc = rf_ref[1, g]
        beta_c = col(rf_ref[0, g])
        gc_c = col(gc)
        exp_gc_c = col(rf_ref[2, g])
        beta_exp_c = col(rf_ref[3, g])
        tail_c = col(rf_ref[4, g])
        e_last = rf_ref[5, g]
        e_last = jnp.concatenate([e_last, e_last], axis=-1)
        decay = jnp.exp(jnp.where(lower_incl[None], gc_c - gc[:, None, :], -jnp.inf))

        kb3 = (k3 * beta_c).astype(BF16)
        kk = jnp.einsum("cid,cjd->cij", kb3, k3b, preferred_element_type=F32)
        qk = jnp.einsum("cid,cjd->cij", q3b, k3b, preferred_element_type=F32) * decay
        low = jnp.where(lower_strict[None], kk * decay, 0.0)

        assert NEUMANN_BLOCK == 8
        m1 = jnp.where(same_block(NEUMANN_BLOCK)[None], -low, 0.0)
        m1b = m1.astype(BF16)
        tinv = eye[None] + m1
        m2b = _bdot(m1b, m1b).astype(BF16)
        tinv = tinv + _bdot(tinv.astype(BF16), m2b)
        m4b = _bdot(m2b, m2b).astype(BF16)
        tinv = tinv + _bdot(tinv.astype(BF16), m4b)
        size = NEUMANN_BLOCK
        while size < c:
            off = jnp.where((same_block(2 * size) & jnp.logical_not(same_block(size)))[None], low, 0.0)
            tb = tinv.astype(BF16)
            tinv = tinv - _bdot(_bdot(tb, off.astype(BF16)).astype(BF16), tb)
            size *= 2

        tb = tinv.astype(BF16)
        vb = (v3 * jnp.concatenate([beta_c, beta_c], axis=-1)).astype(BF16)
        kbg = (k3 * beta_exp_c).astype(BF16)
        u = _bdot(tb, vb)
        w = _bdot(tb, kbg)

        qt = q3 * exp_gc_c
        kd = (k3 * tail_c).astype(BF16)
        ub = u.astype(BF16)
        wb = w.astype(BF16)
        qkb = qk.astype(BF16)
        kdt = jnp.swapaxes(kd, 1, 2)
        a_neg = -_bdot(kdt, wb)
        b_add = _bdot(kdt, ub)
        q_eff = qt - _bdot(qkb, wb)
        o_loc = _bdot(qkb, ub)
        p_all = jnp.concatenate([a_neg, q_eff], axis=1).astype(BF16)
        return p_all, b_add, o_loc, e_last

    def project_out(row, gated):
        lhs = jnp.concatenate(gated, axis=0)
        y_ref[pl.ds(row, lhs.shape[0]), :] += jnp.dot(lhs, wo_ref[...], preferred_element_type=F32)

    def state_steps(g, p_all, b_add, o_loc, e_last):
        ng = ng_ref[...]
        ready = None
        gated = []
        for cc in range(GROUP):
            state = s_ref[...]
            r = jnp.dot(p_all[cc], state.astype(BF16), preferred_element_type=F32)
            o = r[A_DK:, :] + o_loc[cc]
            s_ref[...] = state * e_last[cc:cc + 1, :] + r[:A_DK, :] + b_add[cc]
            if ready is not None:
                project_out(*ready)
                ready = None
            on = o * lax.rsqrt(jnp.mean(o * o, axis=-1, keepdims=True) + EPS) * ng
            row = g * rows + cc * c
            zc = z_ref[pl.ds(row, c), :].astype(F32)
            gated.append((on * (zc * jax.nn.sigmoid(zc))).astype(BF16))
            if len(gated) == OUT_BATCH:
                ready = (row - (OUT_BATCH - 1) * c, gated)
                gated = []
        project_out(*ready)

    @pl.when(head == 0)
    def _():
        y_ref[...] = x_ref[...]

    local = chunk_local(0)
    for g in range(n_groups):
        nxt = chunk_local(g + 1) if g + 1 < n_groups else None
        state_steps(g, *local)
        local = nxt


def _gdn_core(proj3, gates_t, norm_g, w_out, x3):
    bsz, seq, _ = proj3.shape
    d = x3.shape[-1]
    kb = A_QK // A_DK
    vb = 2 * A_QK // A_DV
    zb = vb + A_VW // A_DV
    return pl.pallas_call(
        _gdn_kernel,
        grid=(bsz, A_HEADS),
        in_specs=[
            pl.BlockSpec((None, seq, A_DK), lambda b, h: (b, 0, h)),
            pl.BlockSpec((None, seq, A_DK), lambda b, h: (b, 0, kb + h)),
            pl.BlockSpec((None, seq, A_DV), lambda b, h: (b, 0, vb + h)),
            pl.BlockSpec((None, seq, A_DV), lambda b, h: (b, 0, zb + h)),
            pl.BlockSpec((2 * A_HEADS, seq), lambda b, h: (0, b)),
            pl.BlockSpec((1, A_DV), lambda b, h: (0, 0)),
            pl.BlockSpec((A_DV, d), lambda b, h: (h, 0)),
            pl.BlockSpec((None, seq, d), lambda b, h: (b, 0, 0)),
        ],
        out_specs=pl.BlockSpec((None, seq, d), lambda b, h: (b, 0, 0)),
        out_shape=jax.ShapeDtypeStruct((bsz, seq, d), F32),
        scratch_shapes=[
            pltpu.VMEM((A_DK, A_DV), F32),
            pltpu.VMEM((6, seq // (GROUP * CHUNK), GROUP, CHUNK), F32),
        ],
        compiler_params=pltpu.CompilerParams(
            dimension_semantics=("parallel", "arbitrary"), vmem_limit_bytes=VMEM_LIMIT_BYTES),
        name="gdn_core",
    )(proj3, proj3, proj3, proj3, gates_t, norm_g, w_out, x3)


def _rope_table_kernel(pos_ref, invf_ref, cos_ref, sin_ref):
    tr, per_row = pos_ref.shape
    nf = ROPE_HALF

    def dot_split(a, b, terms):
        parts = _split3(a)[:terms]
        out = jnp.dot(parts[0], b, preferred_element_type=F32)
        for part in parts[1:]:
            out = out + jnp.dot(part, b, preferred_element_type=F32)
        return out

    rk = lax.broadcasted_iota(jnp.int32, (per_row, B_DH), 0)
    rl = lax.broadcasted_iota(jnp.int32, (per_row, B_DH), 1)
    rep = jnp.where(rl // nf == rk, 1.0, 0.0).astype(BF16)
    ang = dot_split(pos_ref[...].astype(F32), rep, 3) * invf_ref[...]
    c = jnp.cos(ang)
    s = jnp.sin(ang)

    ek = lax.broadcasted_iota(jnp.int32, (B_DH, B_DH), 0)
    el = lax.broadcasted_iota(jnp.int32, (B_DH, B_DH), 1)
    lane = lax.broadcasted_iota(jnp.int32, (1, B_DH), 1)
    rotary = (lane < nf) | ((lane >= ROPE_SHIFT) & (lane < ROPE_SHIFT + nf))
    base = jnp.where(rotary, 0.0, 1.0)
    for p in range(per_row):
        mine = ek // nf == p
        first = mine & (el == ek % nf)
        second = mine & (el == ek % nf + ROPE_SHIFT)
        e_cos = jnp.where(first | second, 1.0, 0.0).astype(BF16)
        e_sin = (jnp.where(second, 1.0, 0.0) - jnp.where(first, 1.0, 0.0)).astype(BF16)
        rows = pl.ds(p, tr, stride=per_row)
        cos_ref[rows, :] = dot_split(c, e_cos, 2) + base
        sin_ref[rows, :] = dot_split(s, e_sin, 2)


def _rope_tables(positions):
    t = positions.size
    per_row = B_DH // ROPE_HALF
    inv_freq = ROPE_THETA ** (-jnp.arange(0, ROPE_DIMS, 2, dtype=F32) / ROPE_DIMS)
    invf = jnp.tile(inv_freq, per_row)[None, :]
    rows = t // per_row
    tr = min(rows, 512)
    return pl.pallas_call(
        _rope_table_kernel,
        grid=(rows // tr,),
        in_specs=[pl.BlockSpec((tr, per_row), lambda i: (i, 0)), pl.BlockSpec((1, B_DH), lambda i: (0, 0))],
        out_specs=[pl.BlockSpec((tr * per_row, B_DH), lambda i: (i, 0))] * 2,
        out_shape=[jax.ShapeDtypeStruct((t, B_DH), F32)] * 2,
        name="rope_tables",
    )(positions.reshape(rows, per_row), invf)


def _attn_proj_kernel(x_ref, g_ref, w0_ref, w1_ref, w2_ref, cos_ref, sin_ref, qg_ref, kg_ref,
                      p0_ref, p1_ref, p2_ref, hn_ref):
    tm = x_ref.shape[0]
    n_lane_blocks = hn_ref.shape[0]
    hn_nat = _normed_f32(x_ref, g_ref)
    for j in range(n_lane_blocks):
        hn_ref[j] = hn_nat[:, j * B_DH:(j + 1) * B_DH]

    def by_stream(ref, dil):
        if dil == 1:
            return ref[...]
        return jnp.concatenate([ref[pl.ds(r, tm // dil, stride=dil), :] for r in range(dil)], axis=0)

    def hn_by_stream(dil):
        if dil == 1:
            return hn_nat
        return jnp.concatenate([by_stream(hn_ref.at[j], dil) for j in range(n_lane_blocks)], axis=-1)

    def norm_rope(y, gain, cos, sin):
        y = y * lax.rsqrt(jnp.mean(y * y, axis=-1, keepdims=True) + EPS) * gain
        return y * cos + pltpu.roll(y, ROPE_SHIFT, 1) * sin

    w_refs = (w0_ref, w1_ref, w2_ref)
    for gi in range(B_GROUPS):
        dil = B_DILATIONS[gi]
        hn = hn_by_stream(dil).astype(BF16)
        cos = by_stream(cos_ref, dil)
        sin = by_stream(sin_ref, dil)
        w_ref = w_refs[gi]
        for n0 in range(0, w_ref.shape[1], COL_TILE):
            y = jnp.dot(hn, w_ref[:, n0:n0 + COL_TILE], preferred_element_type=F32)
            if n0 < 2 * B_W:
                gain = (qg_ref if n0 < B_W else kg_ref)[gi:gi + 1, :]
                y = jnp.concatenate(
                    [norm_rope(y[:, j * B_DH:(j + 1) * B_DH], gain, cos, sin)
                     for j in range(COL_TILE // B_DH)], axis=-1)
            y = y.astype(BF16)
            if gi == 0:
                p0_ref[:, n0:n0 + COL_TILE] = y
            else:
                out_ref = p1_ref if gi == 1 else p2_ref
                out_ref[:, :, n0:n0 + COL_TILE] = y.reshape(dil, tm // dil, COL_TILE)


def _attn_proj(x2, g, w0, w1, w2, cos, sin, q_gain, k_gain, bsz, seq):
    t, d = x2.shape
    tm = ATTN_ROW_TILE
    per_seq = seq // tm
    d1, d2 = B_DILATIONS[1], B_DILATIONS[2]
    return pl.pallas_call(
        _attn_proj_kernel,
        grid=(t // tm,),
        in_specs=[
            pl.BlockSpec((tm, d), lambda i: (i, 0)),
            _resident((1, d)),
            _resident(w0.shape),
            _resident(w1.shape),
            _resident(w2.shape),
            pl.BlockSpec((tm, B_DH), lambda i: (i, 0)),
            pl.BlockSpec((tm, B_DH), lambda i: (i, 0)),
            _resident((B_GROUPS, B_DH)),
            _resident((B_GROUPS, B_DH)),
        ],
        out_specs=[
            pl.BlockSpec((tm, w0.shape[1]), lambda i: (i, 0)),
            pl.BlockSpec((None, d1, tm // d1, w1.shape[1]), lambda i: (i // per_seq, 0, i % per_seq, 0)),
            pl.BlockSpec((None, d2, tm // d2, w2.shape[1]), lambda i: (i // per_seq, 0, i % per_seq, 0)),
        ],
        out_shape=[
            jax.ShapeDtypeStruct((t, w0.shape[1]), BF16),
            jax.ShapeDtypeStruct((bsz, d1, seq // d1, w1.shape[1]), BF16),
            jax.ShapeDtypeStruct((bsz, d2, seq // d2, w2.shape[1]), BF16),
        ],
        scratch_shapes=[pltpu.VMEM((d // B_DH, tm, B_DH), F32)],
        compiler_params=pltpu.CompilerParams(
            dimension_semantics=("parallel",), vmem_limit_bytes=VMEM_LIMIT_BYTES),
        name="attn_norm_proj",
    )(x2, g, w0, w1, w2, cos, sin, q_gain, k_gain)


def _attn_kernel(q0_ref, k0_ref, v0_ref, q1_ref, k1_ref, v1_ref, q2_ref, k2_ref, v2_ref, z_ref,
                 o_ref, og_ref, lg_ref):
    seq = q0_ref.shape[0]
    blk = B_BLOCK
    nb = ATTN_BATCH
    tile = nb * blk
    assert all(w // d == blk for w, d in zip(B_WINDOWS, B_DILATIONS))

    qi = lax.broadcasted_iota(jnp.int32, (nb, blk, 2 * blk), 1)
    kj = lax.broadcasted_iota(jnp.int32, (nb, blk, 2 * blk), 2)
    bi = lax.broadcasted_iota(jnp.int32, (nb, blk, 2 * blk), 0)
    band = ((kj < blk) & (kj >= qi)) | ((kj >= blk) & (kj - blk <= qi))
    band_first = band & ((kj >= blk) | (bi > 0))
    cur_only = (lax.broadcasted_iota(jnp.int32, (nb, blk, blk), 2)
                <= lax.broadcasted_iota(jnp.int32, (nb, blk, blk), 1))
    ones_v = jnp.ones((nb, blk, B_DH), BF16)
    zero_blk = jnp.zeros((1, blk, B_DH), BF16)

    q_refs = (q0_ref, q1_ref, q2_ref)
    k_refs = (k0_ref, k1_ref, k2_ref)
    v_refs = (v0_ref, v1_ref, v2_ref)

    def blocks(ref, first, count):
        return ref[pl.ds(first, count * blk), :].reshape(count, blk, B_DH)

    def scores(gi, first, starts_stream, has_prev):
        qb = blocks(q_refs[gi], first, nb)
        kc = blocks(k_refs[gi], first, nb)
        if not has_prev:
            s = jnp.einsum("bqd,bkd->bqk", qb, kc, preferred_element_type=F32)
            return jnp.where(cur_only, s, -jnp.inf)
        if starts_stream:
            kp = jnp.concatenate([zero_blk, kc[:nb - 1]], axis=0)
        else:
            kp = blocks(k_refs[gi], first - blk, nb)
        s = jnp.einsum("bqd,bkd->bqk", qb, jnp.concatenate([kp, kc], axis=1),
                       preferred_element_type=F32)
        return jnp.where(band_first if starts_stream else band, s, -jnp.inf)

    def finish(gi, first, starts_stream, has_prev, out_rows, s):
        m = jnp.max(s, axis=-1, keepdims=True)
        p = jnp.exp(s - m).astype(BF16)
        vc = jnp.concatenate([blocks(v_refs[gi], first, nb), ones_v], axis=-1)
        if has_prev:
            if starts_stream:
                vp = jnp.concatenate([jnp.zeros((1, blk, 2 * B_DH), BF16), vc[:nb - 1]], axis=0)
            else:
                vp = jnp.concatenate([blocks(v_refs[gi], first - blk, nb), ones_v], axis=-1)
            vc = jnp.concatenate([vp, vc], axis=1)
        acc = jnp.einsum("bqk,bkd->bqd", p, vc, preferred_element_type=F32)
        den = acc[..., B_DH:]
        o = acc[..., :B_DH] / den
        lse = m + jnp.log(den)
        b0 = 0
        for rows, count in out_rows:
            og_ref[gi, rows, :] = o[b0:b0 + count].reshape(count * blk, B_DH)
            lg_ref[gi, rows, :] = lse[b0:b0 + count].reshape(count * blk, B_DH)
            b0 += count

    batches = []
    for gi in range(B_GROUPS):
        dil = B_DILATIONS[gi]
        length = seq // dil
        per_stream = length // blk
        for first in range(0, seq, tile):
            r, m0 = divmod(first, length)
            if per_stream >= nb:
                t0 = r + m0 * dil
                rows = pl.ds(t0, tile) if dil == 1 else pl.ds(t0, tile, stride=dil)
                batches.append((gi, first, m0 == 0, True, [(rows, nb)]))
            else:
                assert per_stream == 1
                rows = [(pl.ds(r + j, blk, stride=dil), 1) for j in range(nb)]
                batches.append((gi, first, True, False, rows))

    pending = {}
    for i in range(len(batches) + ATTN_LOOKAHEAD):
        if i < len(batches):
            pending[i] = scores(*batches[i][:4])
        if i >= ATTN_LOOKAHEAD:
            j = i - ATTN_LOOKAHEAD
            finish(*batches[j], pending.pop(j))

    def merge(t, carry):
        r0 = pl.multiple_of(t * tile, tile)
        l0 = lg_ref[0, pl.ds(r0, tile), :]
        l1 = lg_ref[1, pl.ds(r0, tile), :]
        l2 = lg_ref[2, pl.ds(r0, tile), :]
        m = jnp.maximum(jnp.maximum(l0, l1), l2)
        w0 = jnp.exp(l0 - m)
        w1 = jnp.exp(l1 - m)
        w2 = jnp.exp(l2 - m)
        num = (w0 * og_ref[0, pl.ds(r0, tile), :] + w1 * og_ref[1, pl.ds(r0, tile), :]
               + w2 * og_ref[2, pl.ds(r0, tile), :])
        o = num / (w0 + w1 + w2)
        z = z_ref[pl.ds(r0, tile), :].astype(F32)
        o_ref[pl.ds(r0, tile), :] = (o * (z * jax.nn.sigmoid(z))).astype(o_ref.dtype)
        return carry

    lax.fori_loop(0, seq // tile, merge, 0)


def _attn_core(p0, p1, p2):
    bsz, seq, _ = p0.shape

    def cols(which):
        return pl.BlockSpec((None, seq, B_DH), lambda b, h: (b, 0, which * B_HEADS + h))

    qkv = [cols(0), cols(1), cols(2)]
    return pl.pallas_call(
        _attn_kernel,
        grid=(bsz, B_HEADS),
        in_specs=[*qkv, *qkv, *qkv, cols(3)],
        out_specs=pl.BlockSpec((None, seq, B_DH), lambda b, h: (b, 0, h)),
        out_shape=jax.ShapeDtypeStruct((bsz, seq, B_W), BF16),
        scratch_shapes=[
            pltpu.VMEM((B_GROUPS, seq, B_DH), F32),
            pltpu.VMEM((B_GROUPS, seq, B_DH), F32),
        ],
        compiler_params=pltpu.CompilerParams(
            dimension_semantics=("parallel", "parallel"), vmem_limit_bytes=VMEM_LIMIT_BYTES),
        name="dilated_attn",
    )(p0, p0, p0, p1, p1, p1, p2, p2, p2, p0)


def _rotary_lane_order(a):
    gap = ROPE_SHIFT - ROPE_HALF
    return jnp.concatenate(
        [a[..., :ROPE_HALF], a[..., ROPE_DIMS:ROPE_DIMS + gap], a[..., ROPE_HALF:ROPE_DIMS],
         a[..., ROPE_DIMS + gap:]], axis=-1)


def _attn_weights_kernel(w_ref, o_ref):
    w = w_ref[...].astype(BF16)

    @pl.when(pl.program_id(0) < 2)
    def _():
        src = lax.broadcasted_iota(jnp.int32, (B_DH, B_DH), 0)
        dst = lax.broadcasted_iota(jnp.int32, (B_DH, B_DH), 1)
        gap = ROPE_SHIFT - ROPE_HALF
        wanted = jnp.where(dst < ROPE_HALF, dst,
                           jnp.where(dst < ROPE_HALF + gap, dst + (ROPE_DIMS - ROPE_HALF),
                                     jnp.where(dst < ROPE_DIMS + gap, dst - gap, dst)))
        perm = jnp.where(src == wanted, 1.0, 0.0).astype(BF16)
        for j in range(B_HEADS):
            cols = slice(j * B_DH, (j + 1) * B_DH)
            o_ref[:, cols] = jnp.dot(w[:, cols], perm, preferred_element_type=F32).astype(BF16)

    @pl.when(pl.program_id(0) >= 2)
    def _():
        o_ref[...] = w


def _attn_group_weights(wb, gi, with_gate):
    d = wb.shape[0]
    n_blocks = 4 if with_gate else 3
    gate_block = 3 * B_GROUPS

    def col_block(which):
        return (0, jnp.where(which < 3, which * B_GROUPS + gi, gate_block))

    return pl.pallas_call(
        _attn_weights_kernel,
        grid=(n_blocks,),
        in_specs=[pl.BlockSpec((d, B_W), col_block)],
        out_specs=pl.BlockSpec((d, B_W), lambda which: (0, which)),
        out_shape=jax.ShapeDtypeStruct((d, n_blocks * B_W), BF16),
        name=f"attn_weights_g{gi}",
    )(wb)


def kernel(x, positions, norm_g, a_w_in, a_conv_w, a_log, a_dt_bias, a_norm_g, a_w_out,
           b_w_in, b_q_norm_g, b_k_norm_g, b_w_out):
    bsz, seq, d = x.shape
    t = bsz * seq
    x2 = x.reshape(t, d)

    w_main = a_w_in[0, :, :A_MAIN].astype(BF16)
    w_gate_t = a_w_in[0, :, A_MAIN:].T.astype(BF16)
    zeros8 = jnp.zeros((A_HEADS,), F32)
    alog16 = jnp.concatenate([zeros8, a_log[0].astype(F32)])[:, None]
    dtb16 = jnp.concatenate([zeros8, a_dt_bias[0].astype(F32)])[:, None]
    proj, gates_t = _gdn_proj(x2, norm_g[0][None, :], w_main, w_gate_t, alog16, dtb16,
                              a_conv_w[0], seq)
    x3 = _gdn_core(proj.reshape(bsz, seq, A_MAIN), gates_t, a_norm_g[0][None, :],
                   a_w_out[0].astype(BF16), x)
    x2 = x3.reshape(t, d)

    w0 = _attn_group_weights(b_w_in[0], 0, with_gate=True)
    w1 = _attn_group_weights(b_w_in[0], 1, with_gate=False)
    w2 = _attn_group_weights(b_w_in[0], 2, with_gate=False)
    q_gain = _rotary_lane_order(b_q_norm_g[0]) * (B_DH ** -0.5)
    k_gain = _rotary_lane_order(b_k_norm_g[0])
    cos, sin = _rope_tables(positions)
    p0, p1, p2 = _attn_proj(x2, norm_g[1][None, :], w0, w1, w2, cos, sin, q_gain, k_gain, bsz, seq)
    o = _attn_core(p0.reshape(bsz, seq, -1), p1.reshape(bsz, seq, -1), p2.reshape(bsz, seq, -1))
    x2 = _out_proj(o.reshape(t, B_W), b_w_out[0].astype(BF16), x2, "attn_out_proj")
    return x2.reshape(bsz, seq, d)
```

```python
import functools

import jax
import jax.numpy as jnp
from jax import lax
from jax.experimental import pallas as pl
from jax.experimental.pallas import tpu as pltpu

F32 = jnp.float32
BF16 = jnp.bfloat16
EPS = 1e-6

D_MODEL = 1024
A_HEADS = 8
A_DK = 128
A_DV = 256
A_QK = A_HEADS * A_DK
A_VW = A_HEADS * A_DV
A_CONVW = 2 * A_QK + A_VW
A_MAIN = A_CONVW + A_VW
A_CONV = 4
CHUNK = 128
GROUP = 16
OUT_BATCH = 4
NEUMANN_BLOCK = 8
B_WINDOWS = (128, 512, 2048)
B_DILATIONS = (1, 4, 16)
B_GROUPS = 3
B_HEADS = 8
B_DH = 128
B_W = B_HEADS * B_DH
B_BLOCK = 128
ROPE_THETA = 500000.0
ROPE_DIMS = B_DH // 4
ROPE_HALF = ROPE_DIMS // 2
ROPE_SHIFT = B_DH // 2
ATTN_BATCH = 4
ATTN_LOOKAHEAD = 2

VMEM_LIMIT_BYTES = 56 * 1024 * 1024
OUT_ROW_TILE = 1024
GDN_ROW_TILE = 256
ATTN_ROW_TILE = 256
COL_TILE = 512
HALO = 16
CAST_COLS = 1024


def _split3(a):
    hi = a.astype(BF16)
    r = a - hi.astype(F32)
    mid = r.astype(BF16)
    lo = (r - mid.astype(F32)).astype(BF16)
    return hi, mid, lo


def _bdot(a, b):
    return jnp.einsum("cik,ckj->cij", a, b, preferred_element_type=F32)


def _normed_f32(x_ref, g_ref):
    x = x_ref[...]
    ms = jnp.mean(x * x, axis=-1, keepdims=True)
    return x * lax.rsqrt(ms + EPS) * g_ref[...]


def _resident(shape):
    return pl.BlockSpec(shape, lambda i: (0,) * len(shape), pipeline_mode=pl.Buffered(1))


def _gdn_proj_kernel(x_ref, xh_ref, g_ref, w_ref, wgt_ref, alog_ref, dtb_ref, cw_ref,
                     o_ref, gates_ref, *, tiles_per_seq):
    tm = x_ref.shape[0]
    first = (pl.program_id(0) % tiles_per_seq) == 0
    hn = _normed_f32(x_ref, g_ref).astype(BF16)
    halo = jnp.where(first, 0.0, _normed_f32(xh_ref, g_ref)).astype(BF16)
    hcat = jnp.concatenate([halo, hn], axis=0)

    for n0 in range(0, A_CONVW, COL_TILE):
        y = jnp.dot(hcat, w_ref[:, n0:n0 + COL_TILE], preferred_element_type=F32)
        cw = cw_ref[:, n0:n0 + COL_TILE]
        acc = y * cw[A_CONV - 1:A_CONV, :]
        for tap in range(A_CONV - 1):
            acc = acc + pltpu.roll(y, A_CONV - 1 - tap, 0) * cw[tap:tap + 1, :]
        o_ref[:, n0:n0 + COL_TILE] = acc[HALO:, :].astype(o_ref.dtype)

    for n0 in range(A_CONVW, A_MAIN, COL_TILE):
        o_ref[:, n0:n0 + COL_TILE] = jnp.dot(
            hn, w_ref[:, n0:n0 + COL_TILE], preferred_element_type=F32).astype(o_ref.dtype)

    logits = lax.dot_general(wgt_ref[...], hn, (((1,), (1,)), ((), ())), preferred_element_type=F32)
    row = lax.broadcasted_iota(jnp.int32, logits.shape, 0)
    xs = logits + dtb_ref[...]
    softplus = jnp.maximum(xs, 0.0) + jnp.log1p(jnp.exp(-jnp.abs(xs)))
    decay = -jnp.exp(alog_ref[...]) * softplus
    gates_ref[...] = jnp.where(row < A_HEADS, jax.nn.sigmoid(logits), decay)


def _gdn_proj(x2, g, w, wgt, alog16, dtb16, conv_w, seq):
    t, d = x2.shape
    tm = GDN_ROW_TILE
    kern = functools.partial(_gdn_proj_kernel, tiles_per_seq=seq // tm)
    return pl.pallas_call(
        kern,
        grid=(t // tm,),
        in_specs=[
            pl.BlockSpec((tm, d), lambda i: (i, 0)),
            pl.BlockSpec((HALO, d), lambda i: (jnp.maximum(i * (tm // HALO) - 1, 0), 0)),
            _resident((1, d)),
            _resident((d, A_MAIN)),
            _resident((2 * A_HEADS, d)),
            _resident((2 * A_HEADS, 1)),
            _resident((2 * A_HEADS, 1)),
            _resident((A_CONV, A_CONVW)),
        ],
        out_specs=[
            pl.BlockSpec((tm, A_MAIN), lambda i: (i, 0)),
            pl.BlockSpec((2 * A_HEADS, tm), lambda i: (0, i)),
        ],
        out_shape=[
            jax.ShapeDtypeStruct((t, A_MAIN), BF16),
            jax.ShapeDtypeStruct((2 * A_HEADS, t), F32),
        ],
        compiler_params=pltpu.CompilerParams(
            dimension_semantics=("parallel",), vmem_limit_bytes=VMEM_LIMIT_BYTES),
        name="gdn_norm_proj",
    )(x2, x2, g, w, wgt, alog16, dtb16, conv_w)


def _out_proj_kernel(o_ref, w_ref, x_ref, y_ref):
    y_ref[...] = x_ref[...] + jnp.dot(o_ref[...], w_ref[...], preferred_element_type=F32)


def _out_proj(o2, w, x2, name):
    t, k = o2.shape
    d = w.shape[1]
    tm = OUT_ROW_TILE
    return pl.pallas_call(
        _out_proj_kernel,
        grid=(t // tm,),
        in_specs=[
            pl.BlockSpec((tm, k), lambda i: (i, 0)),
            _resident((k, d)),
            pl.BlockSpec((tm, d), lambda i: (i, 0)),
        ],
        out_specs=pl.BlockSpec((tm, d), lambda i: (i, 0)),
        out_shape=jax.ShapeDtypeStruct((t, d), F32),
        compiler_params=pltpu.CompilerParams(
            dimension_semantics=("parallel",), vmem_limit_bytes=VMEM_LIMIT_BYTES),
        name=name,
    )(o2, w, x2)


def _gdn_kernel(q_ref, k_ref, v_ref, z_ref, gt_ref, ng_ref, wo_ref, x_ref, y_ref, s_ref, rf_ref):
    head = pl.program_id(1)
    rows = GROUP * CHUNK
    n_groups = q_ref.shape[0] // rows
    c = CHUNK

    ii = lax.broadcasted_iota(jnp.int32, (c, c), 0)
    jj = lax.broadcasted_iota(jnp.int32, (c, c), 1)
    eye = jnp.where(ii == jj, 1.0, 0.0).astype(F32)
    lower_incl = ii >= jj
    lower_strict = ii > jj
    cum_rhs = jnp.where(ii <= jj, 1.0, 0.0).astype(BF16)

    def same_block(size):
        return (ii // size) == (jj // size)

    def col(x):
        return jnp.swapaxes(jnp.broadcast_to(x[:, None, :], (GROUP, c, c)), 1, 2)

    s_ref[...] = jnp.zeros_like(s_ref)

    n_chunks = q_ref.shape[0] // c

    def chunk_rows(r):
        full = gt_ref[pl.ds(r, 1), :]
        return jnp.concatenate([full[:, n * c:(n + 1) * c] for n in range(n_chunks)], axis=0)

    beta_all = chunk_rows(head)
    gh, gm, gl = _split3(chunk_rows(head + A_HEADS))
    gc_all = (jnp.dot(gh, cum_rhs, preferred_element_type=F32)
              + jnp.dot(gm, cum_rhs, preferred_element_type=F32)
              + jnp.dot(gl, cum_rhs, preferred_element_type=F32))
    exp_all = jnp.exp(gc_all)
    last_all = jnp.broadcast_to(gc_all[:, c - 1:c], gc_all.shape)
    rowforms = (beta_all, gc_all, exp_all, beta_all * exp_all, jnp.exp(last_all - gc_all),
                jnp.exp(last_all))
    for idx, val in enumerate(rowforms):
        for n in range(n_groups):
            rf_ref[idx, n] = val[n * GROUP:(n + 1) * GROUP, :]

    def chunk_local(g):
        r0 = g * rows
        def silu_rows(ref):
            a = ref[pl.ds(r0, rows), :].astype(F32)
            return a * jax.nn.sigmoid(a)

        def l2n(a):
            return a * lax.rsqrt(jnp.sum(a * a, axis=-1, keepdims=True) + EPS)

        q3 = (l2n(silu_rows(q_ref)) * (A_DK ** -0.5)).reshape(GROUP, c, A_DK)
        k3 = l2n(silu_rows(k_ref)).reshape(GROUP, c, A_DK)
        v3 = silu_rows(v_ref).reshape(GROUP, c, A_DV)
        q3b = q3.astype(BF16)
        k3b = k3.astype(BF16)

        gc = rf_ref[1, g]
        beta_c = col(rf_ref[0, g])
        gc_c = col(gc)
        exp_gc_c = col(rf_ref[2, g])
        beta_exp_c = col(rf_ref[3, g])
        tail_c = col(rf_ref[4, g])
        e_last = rf_ref[5, g]
        e_last = jnp.concatenate([e_last, e_last], axis=-1)
        decay = jnp.exp(jnp.where(lower_incl[None], gc_c - gc[:, None, :], -jnp.inf))

        kb3 = (k3 * beta_c).astype(BF16)
        kk = jnp.einsum("cid,cjd->cij", kb3, k3b, preferred_element_type=F32)
        qk = jnp.einsum("cid,cjd->cij", q3b, k3b, preferred_element_type=F32) * decay
        low = jnp.where(lower_strict[None], kk * decay, 0.0)

        assert NEUMANN_BLOCK == 8
        m1 = jnp.where(same_block(NEUMANN_BLOCK)[None], -low, 0.0)
        m1b = m1.astype(BF16)
        tinv = eye[None] + m1
        m2b = _bdot(m1b, m1b).astype(BF16)
        tinv = tinv + _bdot(tinv.astype(BF16), m2b)
        m4b = _bdot(m2b, m2b).astype(BF16)
        tinv = tinv + _bdot(tinv.astype(BF16), m4b)
        size = NEUMANN_BLOCK
        while size < c:
            off = jnp.where((same_block(2 * size) & jnp.logical_not(same_block(size)))[None], low, 0.0)
            tb = tinv.astype(BF16)
            tinv = tinv - _bdot(_bdot(tb, off.astype(BF16)).astype(BF16), tb)
            size *= 2

        tb = tinv.astype(BF16)
        vb = (v3 * jnp.concatenate([beta_c, beta_c], axis=-1)).astype(BF16)
        kbg = (k3 * beta_exp_c).astype(BF16)
        u = _bdot(tb, vb)
        w = _bdot(tb, kbg)

        qt = q3 * exp_gc_c
        kd = (k3 * tail_c).astype(BF16)
        ub = u.astype(BF16)
        wb = w.astype(BF16)
        qkb = qk.astype(BF16)
        kdt = jnp.swapaxes(kd, 1, 2)
        a_neg = -_bdot(kdt, wb)
        b_add = _bdot(kdt, ub)
        q_eff = qt - _bdot(qkb, wb)
        o_loc = _bdot(qkb, ub)
        p_all = jnp.concatenate([a_neg, q_eff], axis=1).astype(BF16)
        return p_all, b_add, o_loc, e_last

    def project_out(row, gated):
        lhs = jnp.concatenate(gated, axis=0)
        y_ref[pl.ds(row, lhs.shape[0]), :] += jnp.dot(lhs, wo_ref[...], preferred_element_type=F32)

    def state_steps(g, p_all, b_add, o_loc, e_last):
        ng = ng_ref[...]
        ready = None
        gated = []
        for cc in range(GROUP):
            state = s_ref[...]
            r = jnp.dot(p_all[cc], state.astype(BF16), preferred_element_type=F32)
            o = r[A_DK:, :] + o_loc[cc]
            s_ref[...] = state * e_last[cc:cc + 1, :] + r[:A_DK, :] + b_add[cc]
            if ready is not None:
                project_out(*ready)
                ready = None
            on = o * lax.rsqrt(jnp.mean(o * o, axis=-1, keepdims=True) + EPS) * ng
            row = g * rows + cc * c
            zc = z_ref[pl.ds(row, c), :].astype(F32)
            gated.append((on * (zc * jax.nn.sigmoid(zc))).astype(BF16))
            if len(gated) == OUT_BATCH:
                ready = (row - (OUT_BATCH - 1) * c, gated)
                gated = []
        project_out(*ready)

    @pl.when(head == 0)
    def _():
        y_ref[...] = x_ref[...]

    local = chunk_local(0)
    for g in range(n_groups):
        nxt = chunk_local(g + 1) if g + 1 < n_groups else None
        state_steps(g, *local)
        local = nxt


def _gdn_core(proj3, gates_t, norm_g, w_out, x3):
    bsz, seq, _ = proj3.shape
    d = x3.shape[-1]
    kb = A_QK // A_DK
    vb = 2 * A_QK // A_DV
    zb = vb + A_VW // A_DV
    return pl.pallas_call(
        _gdn_kernel,
        grid=(bsz, A_HEADS),
        in_specs=[
            pl.BlockSpec((None, seq, A_DK), lambda b, h: (b, 0, h)),
            pl.BlockSpec((None, seq, A_DK), lambda b, h: (b, 0, kb + h)),
            pl.BlockSpec((None, seq, A_DV), lambda b, h: (b, 0, vb + h)),
            pl.BlockSpec((None, seq, A_DV), lambda b, h: (b, 0, zb + h)),
            pl.BlockSpec((2 * A_HEADS, seq), lambda b, h: (0, b)),
            pl.BlockSpec((1, A_DV), lambda b, h: (0, 0)),
            pl.BlockSpec((A_DV, d), lambda b, h: (h, 0)),
            pl.BlockSpec((None, seq, d), lambda b, h: (b, 0, 0)),
        ],
        out_specs=pl.BlockSpec((None, seq, d), lambda b, h: (b, 0, 0)),
        out_shape=jax.ShapeDtypeStruct((bsz, seq, d), F32),
        scratch_shapes=[
            pltpu.VMEM((A_DK, A_DV), F32),
            pltpu.VMEM((6, seq // (GROUP * CHUNK), GROUP, CHUNK), F32),
        ],
        compiler_params=pltpu.CompilerParams(
            dimension_semantics=("parallel", "arbitrary"), vmem_limit_bytes=VMEM_LIMIT_BYTES),
        name="gdn_core",
    )(proj3, proj3, proj3, proj3, gates_t, norm_g, w_out, x3)


def _rope_table_kernel(pos_ref, invf_ref, cos_ref, sin_ref):
    tr, per_row = pos_ref.shape
    nf = ROPE_HALF

    def dot_split(a, b, terms):
        parts = _split3(a)[:terms]
        out = jnp.dot(parts[0], b, preferred_element_type=F32)
        for part in parts[1:]:
            out = out + jnp.dot(part, b, preferred_element_type=F32)
        return out

    rk = lax.broadcasted_iota(jnp.int32, (per_row, B_DH), 0)
    rl = lax.broadcasted_iota(jnp.int32, (per_row, B_DH), 1)
    rep = jnp.where(rl // nf == rk, 1.0, 0.0).astype(BF16)
    ang = dot_split(pos_ref[...].astype(F32), rep, 3) * invf_ref[...]
    c = jnp.cos(ang)
    s = jnp.sin(ang)

    ek = lax.broadcasted_iota(jnp.int32, (B_DH, B_DH), 0)
    el = lax.broadcasted_iota(jnp.int32, (B_DH, B_DH), 1)
    lane = lax.broadcasted_iota(jnp.int32, (1, B_DH), 1)
    rotary = (lane < nf) | ((lane >= ROPE_SHIFT) & (lane < ROPE_SHIFT + nf))
    base = jnp.where(rotary, 0.0, 1.0)
    for p in range(per_row):
        mine = ek // nf == p
        first = mine & (el == ek % nf)
        second = mine & (el == ek % nf + ROPE_SHIFT)
        e_cos = jnp.where(first | second, 1.0, 0.0).astype(BF16)
        e_sin = (jnp.where(second, 1.0, 0.0) - jnp.where(first, 1.0, 0.0)).astype(BF16)
        rows = pl.ds(p, tr, stride=per_row)
        cos_ref[rows, :] = dot_split(c, e_cos, 2) + base
        sin_ref[rows, :] = dot_split(s, e_sin, 2)


def _rope_tables(positions):
    t = positions.size
    per_row = B_DH // ROPE_HALF
    inv_freq = ROPE_THETA ** (-jnp.arange(0, ROPE_DIMS, 2, dtype=F32) / ROPE_DIMS)
    invf = jnp.tile(inv_freq, per_row)[None, :]
    rows = t // per_row
    tr = min(rows, 512)
    return pl.pallas_call(
        _rope_table_kernel,
        grid=(rows // tr,),
        in_specs=[pl.BlockSpec((tr, per_row), lambda i: (i, 0)), pl.BlockSpec((1, B_DH), lambda i: (0, 0))],
        out_specs=[pl.BlockSpec((tr * per_row, B_DH), lambda i: (i, 0))] * 2,
        out_shape=[jax.ShapeDtypeStruct((t, B_DH), F32)] * 2,
        name="rope_tables",
    )(positions.reshape(rows, per_row), invf)


def _attn_proj_kernel(x_ref, g_ref, w0_ref, w1_ref, w2_ref, cos_ref, sin_ref, qg_ref, kg_ref,
                      p0_ref, p1_ref, p2_ref, hn_ref):
    tm = x_ref.shape[0]
    n_lane_blocks = hn_ref.shape[0]
    hn_nat = _normed_f32(x_ref, g_ref)
    for j in range(n_lane_blocks):
        hn_ref[j] = hn_nat[:, j * B_DH:(j + 1) * B_DH]

    def by_stream(ref, dil):
        if dil == 1:
            return ref[...]
        return jnp.concatenate([ref[pl.ds(r, tm // dil, stride=dil), :] for r in range(dil)], axis=0)

    def hn_by_stream(dil):
        if dil == 1:
            return hn_nat
        return jnp.concatenate([by_stream(hn_ref.at[j], dil) for j in range(n_lane_blocks)], axis=-1)

    def norm_rope(y, gain, cos, sin):
        y = y * lax.rsqrt(jnp.mean(y * y, axis=-1, keepdims=True) + EPS) * gain
        return y * cos + pltpu.roll(y, ROPE_SHIFT, 1) * sin

    w_refs = (w0_ref, w1_ref, w2_ref)
    for gi in range(B_GROUPS):
        dil = B_DILATIONS[gi]
        hn = hn_by_stream(dil).astype(BF16)
        cos = by_stream(cos_ref, dil)
        sin = by_stream(sin_ref, dil)
        w_ref = w_refs[gi]
        for n0 in range(0, w_ref.shape[1], COL_TILE):
            y = jnp.dot(hn, w_ref[:, n0:n0 + COL_TILE], preferred_element_type=F32)
            if n0 < 2 * B_W:
                gain = (qg_ref if n0 < B_W else kg_ref)[gi:gi + 1, :]
                y = jnp.concatenate(
                    [norm_rope(y[:, j * B_DH:(j + 1) * B_DH], gain, cos, sin)
                     for j in range(COL_TILE // B_DH)], axis=-1)
            y = y.astype(BF16)
            if gi == 0:
                p0_ref[:, n0:n0 + COL_TILE] = y
            else:
                out_ref = p1_ref if gi == 1 else p2_ref
                out_ref[:, :, n0:n0 + COL_TILE] = y.reshape(dil, tm // dil, COL_TILE)


def _attn_proj(x2, g, w0, w1, w2, cos, sin, q_gain, k_gain, bsz, seq):
    t, d = x2.shape
    tm = ATTN_ROW_TILE
    per_seq = seq // tm
    d1, d2 = B_DILATIONS[1], B_DILATIONS[2]
    return pl.pallas_call(
        _attn_proj_kernel,
        grid=(t // tm,),
        in_specs=[
            pl.BlockSpec((tm, d), lambda i: (i, 0)),
            _resident((1, d)),
            _resident(w0.shape),
            _resident(w1.shape),
            _resident(w2.shape),
            pl.BlockSpec((tm, B_DH), lambda i: (i, 0)),
            pl.BlockSpec((tm, B_DH), lambda i: (i, 0)),
            _resident((B_GROUPS, B_DH)),
            _resident((B_GROUPS, B_DH)),
        ],
        out_specs=[
            pl.BlockSpec((tm, w0.shape[1]), lambda i: (i, 0)),
            pl.BlockSpec((None, d1, tm // d1, w1.shape[1]), lambda i: (i // per_seq, 0, i % per_seq, 0)),
            pl.BlockSpec((None, d2, tm // d2, w2.shape[1]), lambda i: (i // per_seq, 0, i % per_seq, 0)),
        ],
        out_shape=[
            jax.ShapeDtypeStruct((t, w0.shape[1]), BF16),
            jax.ShapeDtypeStruct((bsz, d1, seq // d1, w1.shape[1]), BF16),
            jax.ShapeDtypeStruct((bsz, d2, seq // d2, w2.shape[1]), BF16),
        ],
        scratch_shapes=[pltpu.VMEM((d // B_DH, tm, B_DH), F32)],
        compiler_params=pltpu.CompilerParams(
            dimension_semantics=("parallel",), vmem_limit_bytes=VMEM_LIMIT_BYTES),
        name="attn_norm_proj",
    )(x2, g, w0, w1, w2, cos, sin, q_gain, k_gain)


def _attn_kernel(q0_ref, k0_ref, v0_ref, q1_ref, k1_ref, v1_ref, q2_ref, k2_ref, v2_ref, z_ref,
                 o_ref, og_ref, lg_ref):
    seq = q0_ref.shape[0]
    blk = B_BLOCK
    nb = ATTN_BATCH
    tile = nb * blk
    assert all(w // d == blk for w, d in zip(B_WINDOWS, B_DILATIONS))

    qi = lax.broadcasted_iota(jnp.int32, (nb, blk, 2 * blk), 1)
    kj = lax.broadcasted_iota(jnp.int32, (nb, blk, 2 * blk), 2)
    bi = lax.broadcasted_iota(jnp.int32, (nb, blk, 2 * blk), 0)
    band = ((kj < blk) & (kj >= qi)) | ((kj >= blk) & (kj - blk <= qi))
    band_first = band & ((kj >= blk) | (bi > 0))
    cur_only = (lax.broadcasted_iota(jnp.int32, (nb, blk, blk), 2)
                <= lax.broadcasted_iota(jnp.int32, (nb, blk, blk), 1))
    ones_v = jnp.ones((nb, blk, B_DH), BF16)
    zero_blk = jnp.zeros((1, blk, B_DH), BF16)

    q_refs = (q0_ref, q1_ref, q2_ref)
    k_refs = (k0_ref, k1_ref, k2_ref)
    v_refs = (v0_ref, v1_ref, v2_ref)

    def blocks(ref, first, count):
        return ref[pl.ds(first, count * blk), :].reshape(count, blk, B_DH)

    def scores(gi, first, starts_stream, has_prev):
        qb = blocks(q_refs[gi], first, nb)
        kc = blocks(k_refs[gi], first, nb)
        if not has_prev:
            s = jnp.einsum("bqd,bkd->bqk", qb, kc, preferred_element_type=F32)
            return jnp.where(cur_only, s, -jnp.inf)
        if starts_stream:
            kp = jnp.concatenate([zero_blk, kc[:nb - 1]], axis=0)
        else:
            kp = blocks(k_refs[gi], first - blk, nb)
        s = jnp.einsum("bqd,bkd->bqk", qb, jnp.concatenate([kp, kc], axis=1),
                       preferred_element_type=F32)
        return jnp.where(band_first if starts_stream else band, s, -jnp.inf)

    def finish(gi, first, starts_stream, has_prev, out_rows, s):
        m = jnp.max(s, axis=-1, keepdims=True)
        p = jnp.exp(s - m).astype(BF16)
        vc = jnp.concatenate([blocks(v_refs[gi], first, nb), ones_v], axis=-1)
        if has_prev:
            if starts_stream:
                vp = jnp.concatenate([jnp.zeros((1, blk, 2 * B_DH), BF16), vc[:nb - 1]], axis=0)
            else:
                vp = jnp.concatenate([blocks(v_refs[gi], first - blk, nb), ones_v], axis=-1)
            vc = jnp.concatenate([vp, vc], axis=1)
        acc = jnp.einsum("bqk,bkd->bqd", p, vc, preferred_element_type=F32)
        den = acc[..., B_DH:]
        o = acc[..., :B_DH] / den
        lse = m + jnp.log(den)
        b0 = 0
        for rows, count in out_rows:
            og_ref[gi, rows, :] = o[b0:b0 + count].reshape(count * blk, B_DH)
            lg_ref[gi, rows, :] = lse[b0:b0 + count].reshape(count * blk, B_DH)
            b0 += count

    batches = []
    for gi in range(B_GROUPS):
        dil = B_DILATIONS[gi]
        length = seq // dil
        per_stream = length // blk
        for first in range(0, seq, tile):
            r, m0 = divmod(first, length)
            if per_stream >= nb:
                t0 = r + m0 * dil
                rows = pl.ds(t0, tile) if dil == 1 else pl.ds(t0, tile, stride=dil)
                batches.append((gi, first, m0 == 0, True, [(rows, nb)]))
            else:
                assert per_stream == 1
                rows = [(pl.ds(r + j, blk, stride=dil), 1) for j in range(nb)]
                batches.append((gi, first, True, False, rows))

    pending = {}
    for i in range(len(batches) + ATTN_LOOKAHEAD):
        if i < len(batches):
            pending[i] = scores(*batches[i][:4])
        if i >= ATTN_LOOKAHEAD:
            j = i - ATTN_LOOKAHEAD
            finish(*batches[j], pending.pop(j))

    def merge(t, carry):
        r0 = pl.multiple_of(t * tile, tile)
        l0 = lg_ref[0, pl.ds(r0, tile), :]
        l1 = lg_ref[1, pl.ds(r0, tile), :]
        l2 = lg_ref[2, pl.ds(r0, tile), :]
        m = jnp.maximum(jnp.maximum(l0, l1), l2)
        w0 = jnp.exp(l0 - m)
        w1 = jnp.exp(l1 - m)
        w2 = jnp.exp(l2 - m)
        num = (w0 * og_ref[0, pl.ds(r0, tile), :] + w1 * og_ref[1, pl.ds(r0, tile), :]
               + w2 * og_ref[2, pl.ds(r0, tile), :])
        o = num / (w0 + w1 + w2)
        z = z_ref[pl.ds(r0, tile), :].astype(F32)
        o_ref[pl.ds(r0, tile), :] = (o * (z * jax.nn.sigmoid(z))).astype(o_ref.dtype)
        return carry

    lax.fori_loop(0, seq // tile, merge, 0)


def _attn_core(p0, p1, p2):
    bsz, seq, _ = p0.shape

    def cols(which):
        return pl.BlockSpec((None, seq, B_DH), lambda b, h: (b, 0, which * B_HEADS + h))

    qkv = [cols(0), cols(1), cols(2)]
    return pl.pallas_call(
        _attn_kernel,
        grid=(bsz, B_HEADS),
        in_specs=[*qkv, *qkv, *qkv, cols(3)],
        out_specs=pl.BlockSpec((None, seq, B_DH), lambda b, h: (b, 0, h)),
        out_shape=jax.ShapeDtypeStruct((bsz, seq, B_W), BF16),
        scratch_shapes=[
            pltpu.VMEM((B_GROUPS, seq, B_DH), F32),
            pltpu.VMEM((B_GROUPS, seq, B_DH), F32),
        ],
        compiler_params=pltpu.CompilerParams(
            dimension_semantics=("parallel", "parallel"), vmem_limit_bytes=VMEM_LIMIT_BYTES),
        name="dilated_attn",
    )(p0, p0, p0, p1, p1, p1, p2, p2, p2, p0)


def _rotary_lane_order(a):
    gap = ROPE_SHIFT - ROPE_HALF
    return jnp.concatenate(
        [a[..., :ROPE_HALF], a[..., ROPE_DIMS:ROPE_DIMS + gap], a[..., ROPE_HALF:ROPE_DIMS],
         a[..., ROPE_DIMS + gap:]], axis=-1)


def _cast_kernel(w_ref, o_ref):
    o_ref[...] = w_ref[...].astype(o_ref.dtype)


def _cast_columns(w, n_cols):
    d = w.shape[0]
    return pl.pallas_call(
        _cast_kernel,
        grid=(n_cols // CAST_COLS,),
        in_specs=[pl.BlockSpec((d, CAST_COLS), lambda j: (0, j))],
        out_specs=pl.BlockSpec((d, CAST_COLS), lambda j: (0, j)),
        out_shape=jax.ShapeDtypeStruct((d, n_cols), BF16),
        name="cast_weights",
    )(w)


def _attn_weights_kernel(w_ref, o_ref):
    w = w_ref[...].astype(BF16)

    @pl.when(pl.program_id(0) < 2)
    def _():
        src = lax.broadcasted_iota(jnp.int32, (B_DH, B_DH), 0)
        dst = lax.broadcasted_iota(jnp.int32, (B_DH, B_DH), 1)
        gap = ROPE_SHIFT - ROPE_HALF
        wanted = jnp.where(dst < ROPE_HALF, dst,
                           jnp.where(dst < ROPE_HALF + gap, dst + (ROPE_DIMS - ROPE_HALF),
                                     jnp.where(dst < ROPE_DIMS + gap, dst - gap, dst)))
        perm = jnp.where(src == wanted, 1.0, 0.0).astype(BF16)
        for j in range(B_HEADS):
            cols = slice(j * B_DH, (j + 1) * B_DH)
            o_ref[:, cols] = jnp.dot(w[:, cols], perm, preferred_element_type=F32).astype(BF16)

    @pl.when(pl.program_id(0) >= 2)
    def _():
        o_ref[...] = w


def _attn_group_weights(wb, gi, with_gate):
    d = wb.shape[0]
    n_blocks = 4 if with_gate else 3
    gate_block = 3 * B_GROUPS

    def col_block(which):
        return (0, jnp.where(which < 3, which * B_GROUPS + gi, gate_block))

    return pl.pallas_call(
        _attn_weights_kernel,
        grid=(n_blocks,),
        in_specs=[pl.BlockSpec((d, B_W), col_block)],
        out_specs=pl.BlockSpec((d, B_W), lambda which: (0, which)),
        out_shape=jax.ShapeDtypeStruct((d, n_blocks * B_W), BF16),
        name=f"attn_weights_g{gi}",
    )(wb)


def kernel(x, positions, norm_g, a_w_in, a_conv_w, a_log, a_dt_bias, a_norm_g, a_w_out,
           b_w_in, b_q_norm_g, b_k_norm_g, b_w_out):
    bsz, seq, d = x.shape
    t = bsz * seq
    x2 = x.reshape(t, d)

    w_main = _cast_columns(a_w_in[0], A_MAIN)
    w_gate_t = a_w_in[0, :, A_MAIN:].T.astype(BF16)
    zeros8 = jnp.zeros((A_HEADS,), F32)
    alog16 = jnp.concatenate([zeros8, a_log[0].astype(F32)])[:, None]
    dtb16 = jnp.concatenate([zeros8, a_dt_bias[0].astype(F32)])[:, None]
    proj, gates_t = _gdn_proj(x2, norm_g[0][None, :], w_main, w_gate_t, alog16, dtb16,
                              a_conv_w[0], seq)
    x3 = _gdn_core(proj.reshape(bsz, seq, A_MAIN), gates_t, a_norm_g[0][None, :],
                   _cast_columns(a_w_out[0], d), x)
    x2 = x3.reshape(t, d)

    w0 = _attn_group_weights(b_w_in[0], 0, with_gate=True)
    w1 = _attn_group_weights(b_w_in[0], 1, with_gate=False)
    w2 = _attn_group_weights(b_w_in[0], 2, with_gate=False)
    q_gain = _rotary_lane_order(b_q_norm_g[0]) * (B_DH ** -0.5)
    k_gain = _rotary_lane_order(b_k_norm_g[0])
    cos, sin = _rope_tables(positions)
    p0, p1, p2 = _attn_proj(x2, norm_g[1][None, :], w0, w1, w2, cos, sin, q_gain, k_gain, bsz, seq)
    o = _attn_core(p0.reshape(bsz, seq, -1), p1.reshape(bsz, seq, -1), p2.reshape(bsz, seq, -1))
    x2 = _out_proj(o.reshape(t, B_W), _cast_columns(b_w_out[0], d), x2, "attn_out_proj")
    return x2.reshape(bsz, seq, d)
```

```python
import functools

import jax
import jax.numpy as jnp
from jax import lax
from jax.experimental import pallas as pl
from jax.experimental.pallas import tpu as pltpu

F32 = jnp.float32
BF16 = jnp.bfloat16
EPS = 1e-6

D_MODEL = 1024
A_HEADS = 8
A_DK = 128
A_DV = 256
A_QK = A_HEADS * A_DK
A_VW = A_HEADS * A_DV
A_CONVW = 2 * A_QK + A_VW
A_MAIN = A_CONVW + A_VW
A_CONV = 4
CHUNK = 128
GROUP = 16
OUT_BATCH = 4
NEUMANN_BLOCK = 8
B_WINDOWS = (128, 512, 2048)
B_DILATIONS = (1, 4, 16)
B_GROUPS = 3
B_HEADS = 8
B_DH = 128
B_W = B_HEADS * B_DH
B_BLOCK = 128
ROPE_THETA = 500000.0
ROPE_DIMS = B_DH // 4
ROPE_HALF = ROPE_DIMS // 2
ROPE_SHIFT = B_DH // 2
ATTN_BATCH = 4
ATTN_LOOKAHEAD = 2

VMEM_LIMIT_BYTES = 56 * 1024 * 1024
OUT_ROW_TILE = 1024
GDN_ROW_TILE = 256
ATTN_ROW_TILE = 256
COL_TILE = 512
HALO = 16
CAST_COLS = 1024


def _split3(a):
    hi = a.astype(BF16)
    r = a - hi.astype(F32)
    mid = r.astype(BF16)
    lo = (r - mid.astype(F32)).astype(BF16)
    return hi, mid, lo


def _bdot(a, b):
    return jnp.einsum("cik,ckj->cij", a, b, preferred_element_type=F32)


def _normed_f32(x_ref, g_ref):
    x = x_ref[...]
    ms = jnp.mean(x * x, axis=-1, keepdims=True)
    return x * lax.rsqrt(ms + EPS) * g_ref[...]


def _resident(shape):
    return pl.BlockSpec(shape, lambda i: (0,) * len(shape), pipeline_mode=pl.Buffered(1))


def _gdn_proj_kernel(x_ref, xh_ref, g_ref, w_ref, wgt_ref, alog_ref, dtb_ref, cw_ref,
                     o_ref, gates_ref, *, tiles_per_seq):
    tm = x_ref.shape[0]
    first = (pl.program_id(0) % tiles_per_seq) == 0
    hn = _normed_f32(x_ref, g_ref).astype(BF16)
    halo = jnp.where(first, 0.0, _normed_f32(xh_ref, g_ref)).astype(BF16)
    hcat = jnp.concatenate([halo, hn], axis=0)

    for n0 in range(0, A_CONVW, COL_TILE):
        y = jnp.dot(hcat, w_ref[:, n0:n0 + COL_TILE], preferred_element_type=F32)
        cw = cw_ref[:, n0:n0 + COL_TILE]
        acc = y * cw[A_CONV - 1:A_CONV, :]
        for tap in range(A_CONV - 1):
            acc = acc + pltpu.roll(y, A_CONV - 1 - tap, 0) * cw[tap:tap + 1, :]
        o_ref[:, n0:n0 + COL_TILE] = acc[HALO:, :].astype(o_ref.dtype)

    for n0 in range(A_CONVW, A_MAIN, COL_TILE):
        o_ref[:, n0:n0 + COL_TILE] = jnp.dot(
            hn, w_ref[:, n0:n0 + COL_TILE], preferred_element_type=F32).astype(o_ref.dtype)

    logits = lax.dot_general(wgt_ref[...], hn, (((1,), (1,)), ((), ())), preferred_element_type=F32)
    row = lax.broadcasted_iota(jnp.int32, logits.shape, 0)
    xs = logits + dtb_ref[...]
    softplus = jnp.maximum(xs, 0.0) + jnp.log1p(jnp.exp(-jnp.abs(xs)))
    decay = -jnp.exp(alog_ref[...]) * softplus
    gates_ref[...] = jnp.where(row < A_HEADS, jax.nn.sigmoid(logits), decay)


def _gdn_proj(x2, g, w, wgt, alog16, dtb16, conv_w, seq):
    t, d = x2.shape
    tm = GDN_ROW_TILE
    kern = functools.partial(_gdn_proj_kernel, tiles_per_seq=seq // tm)
    return pl.pallas_call(
        kern,
        grid=(t // tm,),
        in_specs=[
            pl.BlockSpec((tm, d), lambda i: (i, 0)),
            pl.BlockSpec((HALO, d), lambda i: (jnp.maximum(i * (tm // HALO) - 1, 0), 0)),
            _resident((1, d)),
            _resident((d, A_MAIN)),
            _resident((2 * A_HEADS, d)),
            _resident((2 * A_HEADS, 1)),
            _resident((2 * A_HEADS, 1)),
            _resident((A_CONV, A_CONVW)),
        ],
        out_specs=[
            pl.BlockSpec((tm, A_MAIN), lambda i: (i, 0)),
            pl.BlockSpec((2 * A_HEADS, tm), lambda i: (0, i)),
        ],
        out_shape=[
            jax.ShapeDtypeStruct((t, A_MAIN), BF16),
            jax.ShapeDtypeStruct((2 * A_HEADS, t), F32),
        ],
        compiler_params=pltpu.CompilerParams(
            dimension_semantics=("parallel",), vmem_limit_bytes=VMEM_LIMIT_BYTES),
        name="gdn_norm_proj",
    )(x2, x2, g, w, wgt, alog16, dtb16, conv_w)


def _out_proj_kernel(o_ref, w_ref, x_ref, y_ref):
    y_ref[...] = x_ref[...] + jnp.dot(o_ref[...], w_ref[...], preferred_element_type=F32)


def _out_proj(o2, w, x2, name):
    t, k = o2.shape
    d = w.shape[1]
    tm = OUT_ROW_TILE
    return pl.pallas_call(
        _out_proj_kernel,
        grid=(t // tm,),
        in_specs=[
            pl.BlockSpec((tm, k), lambda i: (i, 0)),
            _resident((k, d)),
            pl.BlockSpec((tm, d), lambda i: (i, 0)),
        ],
        out_specs=pl.BlockSpec((tm, d), lambda i: (i, 0)),
        out_shape=jax.ShapeDtypeStruct((t, d), F32),
        compiler_params=pltpu.CompilerParams(
            dimension_semantics=("parallel",), vmem_limit_bytes=VMEM_LIMIT_BYTES),
        name=name,
    )(o2, w, x2)


def _gdn_kernel(q_ref, k_ref, v_ref, z_ref, gt_ref, ng_ref, wo_ref, x_ref, y_ref, s_ref, rf_ref):
    head = pl.program_id(1)
    rows = GROUP * CHUNK
    n_groups = q_ref.shape[0] // rows
    c = CHUNK

    ii = lax.broadcasted_iota(jnp.int32, (c, c), 0)
    jj = lax.broadcasted_iota(jnp.int32, (c, c), 1)
    eye = jnp.where(ii == jj, 1.0, 0.0).astype(F32)
    lower_incl = ii >= jj
    lower_strict = ii > jj
    cum_rhs = jnp.where(ii <= jj, 1.0, 0.0).astype(BF16)

    def same_block(size):
        return (ii // size) == (jj // size)

    def col(x):
        return jnp.swapaxes(jnp.broadcast_to(x[:, None, :], (GROUP, c, c)), 1, 2)

    s_ref[...] = jnp.zeros_like(s_ref)

    n_chunks = q_ref.shape[0] // c

    def chunk_rows(r):
        full = gt_ref[pl.ds(r, 1), :]
        return jnp.concatenate([full[:, n * c:(n + 1) * c] for n in range(n_chunks)], axis=0)

    beta_all = chunk_rows(head)
    gh, gm, gl = _split3(chunk_rows(head + A_HEADS))
    gc_all = (jnp.dot(gh, cum_rhs, preferred_element_type=F32)
              + jnp.dot(gm, cum_rhs, preferred_element_type=F32)
              + jnp.dot(gl, cum_rhs, preferred_element_type=F32))
    exp_all = jnp.exp(gc_all)
    last_all = jnp.broadcast_to(gc_all[:, c - 1:c], gc_all.shape)
    rowforms = (beta_all, gc_all, exp_all, beta_all * exp_all, jnp.exp(last_all - gc_all),
                jnp.exp(last_all))
    for idx, val in enumerate(rowforms):
        for n in range(n_groups):
            rf_ref[idx, n] = val[n * GROUP:(n + 1) * GROUP, :]

    def chunk_local(g):
        r0 = g * rows
        def silu_rows(ref):
            a = ref[pl.ds(r0, rows), :].astype(F32)
            return a * jax.nn.sigmoid(a)

        def l2n(a):
            return a * lax.rsqrt(jnp.sum(a * a, axis=-1, keepdims=True) + EPS)

        q3 = (l2n(silu_rows(q_ref)) * (A_DK ** -0.5)).reshape(GROUP, c, A_DK)
        k3 = l2n(silu_rows(k_ref)).reshape(GROUP, c, A_DK)
        v3 = silu_rows(v_ref).reshape(GROUP, c, A_DV)
        q3b = q3.astype(BF16)
        k3b = k3.astype(BF16)

        gc = rf_ref[1, g]
        beta_c = col(rf_ref[0, g])
        gc_c = col(gc)
        exp_gc_c = col(rf_ref[2, g])
        beta_exp_c = col(rf_ref[3, g])
        tail_c = col(rf_ref[4, g])
        e_last = rf_ref[5, g]
        e_last = jnp.concatenate([e_last, e_last], axis=-1)
        decay = jnp.exp(jnp.where(lower_incl[None], gc_c - gc[:, None, :], -jnp.inf))

        kb3 = (k3 * beta_c).astype(BF16)
        kk = jnp.einsum("cid,cjd->cij", kb3, k3b, preferred_element_type=F32)
        qk = jnp.einsum("cid,cjd->cij", q3b, k3b, preferred_element_type=F32) * decay
        low = jnp.where(lower_strict[None], kk * decay, 0.0)

        assert NEUMANN_BLOCK == 8
        m1 = jnp.where(same_block(NEUMANN_BLOCK)[None], -low, 0.0)
        m1b = m1.astype(BF16)
        tinv = eye[None] + m1
        m2b = _bdot(m1b, m1b).astype(BF16)
        tinv = tinv + _bdot(tinv.astype(BF16), m2b)
        m4b = _bdot(m2b, m2b).astype(BF16)
        tinv = tinv + _bdot(tinv.astype(BF16), m4b)
        size = NEUMANN_BLOCK
        while size < c:
            off = jnp.where((same_block(2 * size) & jnp.logical_not(same_block(size)))[None], low, 0.0)
            tb = tinv.astype(BF16)
            tinv = tinv - _bdot(_bdot(tb, off.astype(BF16)).astype(BF16), tb)
            size *= 2

        tb = tinv.astype(BF16)
        vb = (v3 * jnp.concatenate([beta_c, beta_c], axis=-1)).astype(BF16)
        kbg = (k3 * beta_exp_c).astype(BF16)
        u = _bdot(tb, vb)
        w = _bdot(tb, kbg)

        qt = q3 * exp_gc_c
        kd = (k3 * tail_c).astype(BF16)
        ub = u.astype(BF16)
        wb = w.astype(BF16)
        qkb = qk.astype(BF16)
        kdt = jnp.swapaxes(kd, 1, 2)
        a_neg = -_bdot(kdt, wb)
        b_add = _bdot(kdt, ub)
        q_eff = qt - _bdot(qkb, wb)
        o_loc = _bdot(qkb, ub)
        p_all = jnp.concatenate([a_neg, q_eff], axis=1).astype(BF16)
        return p_all, b_add, o_loc, e_last

    def project_out(row, gated):
        lhs = jnp.concatenate(gated, axis=0)
        y_ref[pl.ds(row, lhs.shape[0]), :] += jnp.dot(lhs, wo_ref[...], preferred_element_type=F32)

    def state_steps(g, p_all, b_add, o_loc, e_last):
        ng = ng_ref[...]
        ready = None
        gated = []
        for cc in range(GROUP):
            state = s_ref[...]
            r = jnp.dot(p_all[cc], state.astype(BF16), preferred_element_type=F32)
            o = r[A_DK:, :] + o_loc[cc]
            s_ref[...] = state * e_last[cc:cc + 1, :] + r[:A_DK, :] + b_add[cc]
            if ready is not None:
                project_out(*ready)
                ready = None
            on = o * lax.rsqrt(jnp.mean(o * o, axis=-1, keepdims=True) + EPS) * ng
            row = g * rows + cc * c
            zc = z_ref[pl.ds(row, c), :].astype(F32)
            gated.append((on * (zc * jax.nn.sigmoid(zc))).astype(BF16))
            if len(gated) == OUT_BATCH:
                ready = (row - (OUT_BATCH - 1) * c, gated)
                gated = []
        project_out(*ready)

    @pl.when(head == 0)
    def _():
        y_ref[...] = x_ref[...]

    local = chunk_local(0)
    for g in range(n_groups):
        nxt = chunk_local(g + 1) if g + 1 < n_groups else None
        state_steps(g, *local)
        local = nxt


def _gdn_core(proj3, gates_t, norm_g, w_out, x3):
    bsz, seq, _ = proj3.shape
    d = x3.shape[-1]
    kb = A_QK // A_DK
    vb = 2 * A_QK // A_DV
    zb = vb + A_VW // A_DV
    return pl.pallas_call(
        _gdn_kernel,
        grid=(bsz, A_HEADS),
        in_specs=[
            pl.BlockSpec((None, seq, A_DK), lambda b, h: (b, 0, h)),
            pl.BlockSpec((None, seq, A_DK), lambda b, h: (b, 0, kb + h)),
            pl.BlockSpec((None, seq, A_DV), lambda b, h: (b, 0, vb + h)),
            pl.BlockSpec((None, seq, A_DV), lambda b, h: (b, 0, zb + h)),
            pl.BlockSpec((2 * A_HEADS, seq), lambda b, h: (0, b)),
            pl.BlockSpec((1, A_DV), lambda b, h: (0, 0)),
            pl.BlockSpec((A_DV, d), lambda b, h: (h, 0)),
            pl.BlockSpec((None, seq, d), lambda b, h: (b, 0, 0)),
        ],
        out_specs=pl.BlockSpec((None, seq, d), lambda b, h: (b, 0, 0)),
        out_shape=jax.ShapeDtypeStruct((bsz, seq, d), F32),
        scratch_shapes=[
            pltpu.VMEM((A_DK, A_DV), F32),
            pltpu.VMEM((6, seq // (GROUP * CHUNK), GROUP, CHUNK), F32),
        ],
        compiler_params=pltpu.CompilerParams(
            dimension_semantics=("parallel", "arbitrary"), vmem_limit_bytes=VMEM_LIMIT_BYTES),
        name="gdn_core",
    )(proj3, proj3, proj3, proj3, gates_t, norm_g, w_out, x3)


def _rope_table_kernel(pos_ref, invf_ref, cos_ref, sin_ref):
    tr, per_row = pos_ref.shape
    nf = ROPE_HALF

    def dot_split(a, b, terms):
        parts = _split3(a)[:terms]
        out = jnp.dot(parts[0], b, preferred_element_type=F32)
        for part in parts[1:]:
            out = out + jnp.dot(part, b, preferred_element_type=F32)
        return out

    rk = lax.broadcasted_iota(jnp.int32, (per_row, B_DH), 0)
    rl = lax.broadcasted_iota(jnp.int32, (per_row, B_DH), 1)
    rep = jnp.where(rl // nf == rk, 1.0, 0.0).astype(BF16)
    ang = dot_split(pos_ref[...].astype(F32), rep, 3) * invf_ref[...]
    c = jnp.cos(ang)
    s = jnp.sin(ang)

    ek = lax.broadcasted_iota(jnp.int32, (B_DH, B_DH), 0)
    el = lax.broadcasted_iota(jnp.int32, (B_DH, B_DH), 1)
    lane = lax.broadcasted_iota(jnp.int32, (1, B_DH), 1)
    rotary = (lane < nf) | ((lane >= ROPE_SHIFT) & (lane < ROPE_SHIFT + nf))
    base = jnp.where(rotary, 0.0, 1.0)
    for p in range(per_row):
        mine = ek // nf == p
        first = mine & (el == ek % nf)
        second = mine & (el == ek % nf + ROPE_SHIFT)
        e_cos = jnp.where(first | second, 1.0, 0.0).astype(BF16)
        e_sin = (jnp.where(second, 1.0, 0.0) - jnp.where(first, 1.0, 0.0)).astype(BF16)
        rows = pl.ds(p, tr, stride=per_row)
        cos_ref[rows, :] = dot_split(c, e_cos, 2) + base
        sin_ref[rows, :] = dot_split(s, e_sin, 2)


def _rope_tables(positions):
    t = positions.size
    per_row = B_DH // ROPE_HALF
    inv_freq = ROPE_THETA ** (-jnp.arange(0, ROPE_DIMS, 2, dtype=F32) / ROPE_DIMS)
    invf = jnp.tile(inv_freq, per_row)[None, :]
    rows = t // per_row
    tr = min(rows, 512)
    return pl.pallas_call(
        _rope_table_kernel,
        grid=(rows // tr,),
        in_specs=[pl.BlockSpec((tr, per_row), lambda i: (i, 0)), pl.BlockSpec((1, B_DH), lambda i: (0, 0))],
        out_specs=[pl.BlockSpec((tr * per_row, B_DH), lambda i: (i, 0))] * 2,
        out_shape=[jax.ShapeDtypeStruct((t, B_DH), F32)] * 2,
        name="rope_tables",
    )(positions.reshape(rows, per_row), invf)


def _attn_proj_kernel(x_ref, g_ref, w0_ref, w1_ref, w2_ref, cos_ref, sin_ref, qg_ref, kg_ref,
                      p0_ref, p1_ref, p2_ref, hn_ref):
    tm = x_ref.shape[0]
    n_lane_blocks = hn_ref.shape[0]
    hn_nat = _normed_f32(x_ref, g_ref)
    for j in range(n_lane_blocks):
        hn_ref[j] = hn_nat[:, j * B_DH:(j + 1) * B_DH]

    def by_stream(ref, dil):
        if dil == 1:
            return ref[...]
        return jnp.concatenate([ref[pl.ds(r, tm // dil, stride=dil), :] for r in range(dil)], axis=0)

    def hn_by_stream(dil):
        if dil == 1:
            return hn_nat
        return jnp.concatenate([by_stream(hn_ref.at[j], dil) for j in range(n_lane_blocks)], axis=-1)

    def norm_rope(y, gain, cos, sin):
        y = y * lax.rsqrt(jnp.mean(y * y, axis=-1, keepdims=True) + EPS) * gain
        return y * cos + pltpu.roll(y, ROPE_SHIFT, 1) * sin

    w_refs = (w0_ref, w1_ref, w2_ref)
    for gi in range(B_GROUPS):
        dil = B_DILATIONS[gi]
        hn = hn_by_stream(dil).astype(BF16)
        cos = by_stream(cos_ref, dil)
        sin = by_stream(sin_ref, dil)
        w_ref = w_refs[gi]
        for n0 in range(0, w_ref.shape[1], COL_TILE):
            y = jnp.dot(hn, w_ref[:, n0:n0 + COL_TILE], preferred_element_type=F32)
            if n0 < 2 * B_W:
                gain = (qg_ref if n0 < B_W else kg_ref)[gi:gi + 1, :]
                y = jnp.concatenate(
                    [norm_rope(y[:, j * B_DH:(j + 1) * B_DH], gain, cos, sin)
                     for j in range(COL_TILE // B_DH)], axis=-1)
            y = y.astype(BF16)
            for j in range(COL_TILE // B_DH):
                head_cols = y[:, j * B_DH:(j + 1) * B_DH]
                slab = n0 // B_DH + j
                if gi == 0:
                    p0_ref[slab] = head_cols
                else:
                    out_ref = p1_ref if gi == 1 else p2_ref
                    out_ref[slab] = head_cols.reshape(dil, tm // dil, B_DH)


def _attn_proj(x2, g, w0, w1, w2, cos, sin, q_gain, k_gain, bsz, seq):
    t, d = x2.shape
    tm = ATTN_ROW_TILE
    per_seq = seq // tm
    d1, d2 = B_DILATIONS[1], B_DILATIONS[2]
    n0, n1, n2 = (w.shape[1] // B_DH for w in (w0, w1, w2))
    return pl.pallas_call(
        _attn_proj_kernel,
        grid=(t // tm,),
        in_specs=[
            pl.BlockSpec((tm, d), lambda i: (i, 0)),
            _resident((1, d)),
            _resident(w0.shape),
            _resident(w1.shape),
            _resident(w2.shape),
            pl.BlockSpec((tm, B_DH), lambda i: (i, 0)),
            pl.BlockSpec((tm, B_DH), lambda i: (i, 0)),
            _resident((B_GROUPS, B_DH)),
            _resident((B_GROUPS, B_DH)),
        ],
        out_specs=[
            pl.BlockSpec((None, n0, tm, B_DH), lambda i: (i // per_seq, 0, i % per_seq, 0)),
            pl.BlockSpec((None, n1, d1, tm // d1, B_DH), lambda i: (i // per_seq, 0, 0, i % per_seq, 0)),
            pl.BlockSpec((None, n2, d2, tm // d2, B_DH), lambda i: (i // per_seq, 0, 0, i % per_seq, 0)),
        ],
        out_shape=[
            jax.ShapeDtypeStruct((bsz, n0, seq, B_DH), BF16),
            jax.ShapeDtypeStruct((bsz, n1, d1, seq // d1, B_DH), BF16),
            jax.ShapeDtypeStruct((bsz, n2, d2, seq // d2, B_DH), BF16),
        ],
        scratch_shapes=[pltpu.VMEM((d // B_DH, tm, B_DH), F32)],
        compiler_params=pltpu.CompilerParams(
            dimension_semantics=("parallel",), vmem_limit_bytes=VMEM_LIMIT_BYTES),
        name="attn_norm_proj",
    )(x2, g, w0, w1, w2, cos, sin, q_gain, k_gain)


def _attn_kernel(q0_ref, k0_ref, v0_ref, q1_ref, k1_ref, v1_ref, q2_ref, k2_ref, v2_ref, z_ref,
                 o_ref, og_ref, lg_ref):
    seq = q0_ref.shape[0]
    blk = B_BLOCK
    nb = ATTN_BATCH
    tile = nb * blk
    assert all(w // d == blk for w, d in zip(B_WINDOWS, B_DILATIONS))

    qi = lax.broadcasted_iota(jnp.int32, (nb, blk, 2 * blk), 1)
    kj = lax.broadcasted_iota(jnp.int32, (nb, blk, 2 * blk), 2)
    bi = lax.broadcasted_iota(jnp.int32, (nb, blk, 2 * blk), 0)
    band = ((kj < blk) & (kj >= qi)) | ((kj >= blk) & (kj - blk <= qi))
    band_first = band & ((kj >= blk) | (bi > 0))
    cur_only = (lax.broadcasted_iota(jnp.int32, (nb, blk, blk), 2)
                <= lax.broadcasted_iota(jnp.int32, (nb, blk, blk), 1))
    ones_v = jnp.ones((nb, blk, B_DH), BF16)
    zero_blk = jnp.zeros((1, blk, B_DH), BF16)

    q_refs = (q0_ref, q1_ref, q2_ref)
    k_refs = (k0_ref, k1_ref, k2_ref)
    v_refs = (v0_ref, v1_ref, v2_ref)

    def blocks(ref, first, count):
        if len(ref.shape) == 2:
            return ref[pl.ds(first, count * blk), :].reshape(count, blk, B_DH)
        stream, start = divmod(first, ref.shape[1])
        if ref.shape[1] == blk:
            return ref[pl.ds(stream, count), :, :]
        return ref[stream, pl.ds(start, count * blk), :].reshape(count, blk, B_DH)

    def scores(gi, first, starts_stream, has_prev):
        qb = blocks(q_refs[gi], first, nb)
        kc = blocks(k_refs[gi], first, nb)
        if not has_prev:
            s = jnp.einsum("bqd,bkd->bqk", qb, kc, preferred_element_type=F32)
            return jnp.where(cur_only, s, -jnp.inf)
        if starts_stream:
            kp = jnp.concatenate([zero_blk, kc[:nb - 1]], axis=0)
        else:
            kp = blocks(k_refs[gi], first - blk, nb)
        s = jnp.einsum("bqd,bkd->bqk", qb, jnp.concatenate([kp, kc], axis=1),
                       preferred_element_type=F32)
        return jnp.where(band_first if starts_stream else band, s, -jnp.inf)

    def finish(gi, first, starts_stream, has_prev, out_rows, s):
        m = jnp.max(s, axis=-1, keepdims=True)
        p = jnp.exp(s - m).astype(BF16)
        vc = jnp.concatenate([blocks(v_refs[gi], first, nb), ones_v], axis=-1)
        if has_prev:
            if starts_stream:
                vp = jnp.concatenate([jnp.zeros((1, blk, 2 * B_DH), BF16), vc[:nb - 1]], axis=0)
            else:
                vp = jnp.concatenate([blocks(v_refs[gi], first - blk, nb), ones_v], axis=-1)
            vc = jnp.concatenate([vp, vc], axis=1)
        acc = jnp.einsum("bqk,bkd->bqd", p, vc, preferred_element_type=F32)
        den = acc[..., B_DH:]
        o = acc[..., :B_DH] / den
        lse = m + jnp.log(den)
        b0 = 0
        for rows, count in out_rows:
            og_ref[gi, rows, :] = o[b0:b0 + count].reshape(count * blk, B_DH)
            lg_ref[gi, rows, :] = lse[b0:b0 + count].reshape(count * blk, B_DH)
            b0 += count

    batches = []
    for gi in range(B_GROUPS):
        dil = B_DILATIONS[gi]
        length = seq // dil
        per_stream = length // blk
        for first in range(0, seq, tile):
            r, m0 = divmod(first, length)
            if per_stream >= nb:
                t0 = r + m0 * dil
                rows = pl.ds(t0, tile) if dil == 1 else pl.ds(t0, tile, stride=dil)
                batches.append((gi, first, m0 == 0, True, [(rows, nb)]))
            else:
                assert per_stream == 1
                rows = [(pl.ds(r + j, blk, stride=dil), 1) for j in range(nb)]
                batches.append((gi, first, True, False, rows))

    pending = {}
    for i in range(len(batches) + ATTN_LOOKAHEAD):
        if i < len(batches):
            pending[i] = scores(*batches[i][:4])
        if i >= ATTN_LOOKAHEAD:
            j = i - ATTN_LOOKAHEAD
            finish(*batches[j], pending.pop(j))

    def merge(t, carry):
        r0 = pl.multiple_of(t * tile, tile)
        l0 = lg_ref[0, pl.ds(r0, tile), :]
        l1 = lg_ref[1, pl.ds(r0, tile), :]
        l2 = lg_ref[2, pl.ds(r0, tile), :]
        m = jnp.maximum(jnp.maximum(l0, l1), l2)
        w0 = jnp.exp(l0 - m)
        w1 = jnp.exp(l1 - m)
        w2 = jnp.exp(l2 - m)
        num = (w0 * og_ref[0, pl.ds(r0, tile), :] + w1 * og_ref[1, pl.ds(r0, tile), :]
               + w2 * og_ref[2, pl.ds(r0, tile), :])
        o = num / (w0 + w1 + w2)
        z = z_ref[pl.ds(r0, tile), :].astype(F32)
        o_ref[pl.ds(r0, tile), :] = (o * (z * jax.nn.sigmoid(z))).astype(o_ref.dtype)
        return carry

    lax.fori_loop(0, seq // tile, merge, 0)


def _attn_core(p0, p1, p2):
    bsz, _, seq, _ = p0.shape

    def slab(p, which):
        inner = p.shape[2:]
        return pl.BlockSpec((None, None) + inner,
                            lambda b, h: (b, which * B_HEADS + h) + (0,) * len(inner))

    return pl.pallas_call(
        _attn_kernel,
        grid=(bsz, B_HEADS),
        in_specs=[slab(p, which) for p in (p0, p1, p2) for which in range(3)] + [slab(p0, 3)],
        out_specs=pl.BlockSpec((None, seq, B_DH), lambda b, h: (b, 0, h)),
        out_shape=jax.ShapeDtypeStruct((bsz, seq, B_W), BF16),
        scratch_shapes=[
            pltpu.VMEM((B_GROUPS, seq, B_DH), F32),
            pltpu.VMEM((B_GROUPS, seq, B_DH), F32),
        ],
        compiler_params=pltpu.CompilerParams(
            dimension_semantics=("parallel", "parallel"), vmem_limit_bytes=VMEM_LIMIT_BYTES),
        name="dilated_attn",
    )(p0, p0, p0, p1, p1, p1, p2, p2, p2, p0)


def _rotary_lane_order(a):
    gap = ROPE_SHIFT - ROPE_HALF
    return jnp.concatenate(
        [a[..., :ROPE_HALF], a[..., ROPE_DIMS:ROPE_DIMS + gap], a[..., ROPE_HALF:ROPE_DIMS],
         a[..., ROPE_DIMS + gap:]], axis=-1)


def _cast_kernel(w_ref, o_ref):
    o_ref[...] = w_ref[...].astype(o_ref.dtype)


def _cast_columns(w, n_cols):
    d = w.shape[0]
    return pl.pallas_call(
        _cast_kernel,
        grid=(n_cols // CAST_COLS,),
        in_specs=[pl.BlockSpec((d, CAST_COLS), lambda j: (0, j))],
        out_specs=pl.BlockSpec((d, CAST_COLS), lambda j: (0, j)),
        out_shape=jax.ShapeDtypeStruct((d, n_cols), BF16),
        name="cast_weights",
    )(w)


def _attn_weights_kernel(w_ref, o_ref):
    w = w_ref[...].astype(BF16)

    @pl.when(pl.program_id(0) < 2)
    def _():
        src = lax.broadcasted_iota(jnp.int32, (B_DH, B_DH), 0)
        dst = lax.broadcasted_iota(jnp.int32, (B_DH, B_DH), 1)
        gap = ROPE_SHIFT - ROPE_HALF
        wanted = jnp.where(dst < ROPE_HALF, dst,
                           jnp.where(dst < ROPE_HALF + gap, dst + (ROPE_DIMS - ROPE_HALF),
                                     jnp.where(dst < ROPE_DIMS + gap, dst - gap, dst)))
        perm = jnp.where(src == wanted, 1.0, 0.0).astype(BF16)
        for j in range(B_HEADS):
            cols = slice(j * B_DH, (j + 1) * B_DH)
            o_ref[:, cols] = jnp.dot(w[:, cols], perm, preferred_element_type=F32).astype(BF16)

    @pl.when(pl.program_id(0) >= 2)
    def _():
        o_ref[...] = w


def _attn_group_weights(wb, gi, with_gate):
    d = wb.shape[0]
    n_blocks = 4 if with_gate else 3
    gate_block = 3 * B_GROUPS

    def col_block(which):
        return (0, jnp.where(which < 3, which * B_GROUPS + gi, gate_block))

    return pl.pallas_call(
        _attn_weights_kernel,
        grid=(n_blocks,),
        in_specs=[pl.BlockSpec((d, B_W), col_block)],
        out_specs=pl.BlockSpec((d, B_W), lambda which: (0, which)),
        out_shape=jax.ShapeDtypeStruct((d, n_blocks * B_W), BF16),
        name=f"attn_weights_g{gi}",
    )(wb)


def kernel(x, positions, norm_g, a_w_in, a_conv_w, a_log, a_dt_bias, a_norm_g, a_w_out,
           b_w_in, b_q_norm_g, b_k_norm_g, b_w_out):
    bsz, seq, d = x.shape
    t = bsz * seq
    x2 = x.reshape(t, d)

    w_main = _cast_columns(a_w_in[0], A_MAIN)
    w_gate_t = a_w_in[0, :, A_MAIN:].T.astype(BF16)
    zeros8 = jnp.zeros((A_HEADS,), F32)
    alog16 = jnp.concatenate([zeros8, a_log[0].astype(F32)])[:, None]
    dtb16 = jnp.concatenate([zeros8, a_dt_bias[0].astype(F32)])[:, None]
    proj, gates_t = _gdn_proj(x2, norm_g[0][None, :], w_main, w_gate_t, alog16, dtb16,
                              a_conv_w[0], seq)
    x3 = _gdn_core(proj.reshape(bsz, seq, A_MAIN), gates_t, a_norm_g[0][None, :],
                   _cast_columns(a_w_out[0], d), x)
    x2 = x3.reshape(t, d)

    w0 = _attn_group_weights(b_w_in[0], 0, with_gate=True)
    w1 = _attn_group_weights(b_w_in[0], 1, with_gate=False)
    w2 = _attn_group_weights(b_w_in[0], 2, with_gate=False)
    q_gain = _rotary_lane_order(b_q_norm_g[0]) * (B_DH ** -0.5)
    k_gain = _rotary_lane_order(b_k_norm_g[0])
    cos, sin = _rope_tables(positions)
    p0, p1, p2 = _attn_proj(x2, norm_g[1][None, :], w0, w1, w2, cos, sin, q_gain, k_gain, bsz, seq)
    o = _attn_core(p0, p1, p2)
    x2 = _out_proj(o.reshape(t, B_W), _cast_columns(b_w_out[0], d), x2, "attn_out_proj")
    return x2.reshape(bsz, seq, d)
```

```python
import functools

import jax
import jax.numpy as jnp
from jax import lax
from jax.experimental import pallas as pl
from jax.experimental.pallas import tpu as pltpu

F32 = jnp.float32
BF16 = jnp.bfloat16
EPS = 1e-6

D_MODEL = 1024
A_HEADS = 8
A_DK = 128
A_DV = 256
A_QK = A_HEADS * A_DK
A_VW = A_HEADS * A_DV
A_CONVW = 2 * A_QK + A_VW
A_MAIN = A_CONVW + A_VW
A_CONV = 4
CHUNK = 128
GROUP = 16
OUT_BATCH = 4
NEUMANN_BLOCK = 8
B_WINDOWS = (128, 512, 2048)
B_DILATIONS = (1, 4, 16)
B_GROUPS = 3
B_HEADS = 8
B_DH = 128
B_W = B_HEADS * B_DH
B_BLOCK = 128
ROPE_THETA = 500000.0
ROPE_DIMS = B_DH // 4
ROPE_HALF = ROPE_DIMS // 2
ROPE_SHIFT = B_DH // 2
ATTN_BATCH = 4
ATTN_LOOKAHEAD = 2

VMEM_LIMIT_BYTES = 56 * 1024 * 1024
OUT_ROW_TILE = 2048
GDN_ROW_TILE = 256
ATTN_ROW_TILE = 256
COL_TILE = 512
HALO = 16


def _split3(a):
    hi = a.astype(BF16)
    r = a - hi.astype(F32)
    mid = r.astype(BF16)
    lo = (r - mid.astype(F32)).astype(BF16)
    return hi, mid, lo


def _bdot(a, b):
    return jnp.einsum("cik,ckj->cij", a, b, preferred_element_type=F32)


def _normed_f32(x_ref, g_ref):
    x = x_ref[...]
    ms = jnp.mean(x * x, axis=-1, keepdims=True)
    return x * lax.rsqrt(ms + EPS) * g_ref[...]


def _resident(shape):
    return pl.BlockSpec(shape, lambda i: (0,) * len(shape), pipeline_mode=pl.Buffered(1))


def _gdn_proj_kernel(x_ref, xh_ref, g_ref, w_ref, wgt_ref, alog_ref, dtb_ref, cw_ref,
                     o_ref, gates_ref, *, tiles_per_seq):
    tm = x_ref.shape[0]
    first = (pl.program_id(0) % tiles_per_seq) == 0
    hn = _normed_f32(x_ref, g_ref).astype(BF16)
    halo = jnp.where(first, 0.0, _normed_f32(xh_ref, g_ref)).astype(BF16)
    hcat = jnp.concatenate([halo, hn], axis=0)

    for n0 in range(0, A_CONVW, COL_TILE):
        y = jnp.dot(hcat, w_ref[:, n0:n0 + COL_TILE], preferred_element_type=F32)
        cw = cw_ref[:, n0:n0 + COL_TILE]
        acc = y * cw[A_CONV - 1:A_CONV, :]
        for tap in range(A_CONV - 1):
            acc = acc + pltpu.roll(y, A_CONV - 1 - tap, 0) * cw[tap:tap + 1, :]
        o_ref[:, n0:n0 + COL_TILE] = acc[HALO:, :].astype(o_ref.dtype)

    for n0 in range(A_CONVW, A_MAIN, COL_TILE):
        o_ref[:, n0:n0 + COL_TILE] = jnp.dot(
            hn, w_ref[:, n0:n0 + COL_TILE], preferred_element_type=F32).astype(o_ref.dtype)

    logits = lax.dot_general(wgt_ref[...], hn, (((1,), (1,)), ((), ())), preferred_element_type=F32)
    row = lax.broadcasted_iota(jnp.int32, logits.shape, 0)
    xs = logits + dtb_ref[...]
    softplus = jnp.maximum(xs, 0.0) + jnp.log1p(jnp.exp(-jnp.abs(xs)))
    decay = -jnp.exp(alog_ref[...]) * softplus
    gates_ref[...] = jnp.where(row < A_HEADS, jax.nn.sigmoid(logits), decay)


def _gdn_proj(x2, g, w, wgt, alog16, dtb16, conv_w, seq):
    t, d = x2.shape
    tm = GDN_ROW_TILE
    kern = functools.partial(_gdn_proj_kernel, tiles_per_seq=seq // tm)
    return pl.pallas_call(
        kern,
        grid=(t // tm,),
        in_specs=[
            pl.BlockSpec((tm, d), lambda i: (i, 0)),
            pl.BlockSpec((HALO, d), lambda i: (jnp.maximum(i * (tm // HALO) - 1, 0), 0)),
            _resident((1, d)),
            _resident((d, A_MAIN)),
            _resident((2 * A_HEADS, d)),
            _resident((2 * A_HEADS, 1)),
            _resident((2 * A_HEADS, 1)),
            _resident((A_CONV, A_CONVW)),
        ],
        out_specs=[
            pl.BlockSpec((tm, A_MAIN), lambda i: (i, 0)),
            pl.BlockSpec((2 * A_HEADS, tm), lambda i: (0, i)),
        ],
        out_shape=[
            jax.ShapeDtypeStruct((t, A_MAIN), BF16),
            jax.ShapeDtypeStruct((2 * A_HEADS, t), F32),
        ],
        compiler_params=pltpu.CompilerParams(
            dimension_semantics=("parallel",), vmem_limit_bytes=VMEM_LIMIT_BYTES),
        name="gdn_norm_proj",
    )(x2, x2, g, w, wgt, alog16, dtb16, conv_w)


def _out_proj_kernel(o_ref, w_ref, x_ref, y_ref):
    y_ref[...] = x_ref[...] + jnp.dot(o_ref[...], w_ref[...], preferred_element_type=F32)


def _out_proj(o2, w, x2, name):
    t, k = o2.shape
    d = w.shape[1]
    tm = OUT_ROW_TILE
    return pl.pallas_call(
        _out_proj_kernel,
        grid=(t // tm,),
        in_specs=[
            pl.BlockSpec((tm, k), lambda i: (i, 0)),
            _resident((k, d)),
            pl.BlockSpec((tm, d), lambda i: (i, 0)),
        ],
        out_specs=pl.BlockSpec((tm, d), lambda i: (i, 0)),
        out_shape=jax.ShapeDtypeStruct((t, d), F32),
        compiler_params=pltpu.CompilerParams(
            dimension_semantics=("parallel",), vmem_limit_bytes=VMEM_LIMIT_BYTES),
        name=name,
    )(o2, w, x2)


def _gdn_kernel(q_ref, k_ref, v_ref, z_ref, gt_ref, ng_ref, wo_ref, x_ref, y_ref, s_ref, rf_ref):
    head = pl.program_id(1)
    rows = GROUP * CHUNK
    n_groups = q_ref.shape[0] // rows
    c = CHUNK

    ii = lax.broadcasted_iota(jnp.int32, (c, c), 0)
    jj = lax.broadcasted_iota(jnp.int32, (c, c), 1)
    eye = jnp.where(ii == jj, 1.0, 0.0).astype(F32)
    lower_incl = ii >= jj
    lower_strict = ii > jj
    cum_rhs = jnp.where(ii <= jj, 1.0, 0.0).astype(BF16)

    def same_block(size):
        return (ii // size) == (jj // size)

    def col(x):
        return jnp.swapaxes(jnp.broadcast_to(x[:, None, :], (GROUP, c, c)), 1, 2)

    s_ref[...] = jnp.zeros_like(s_ref)

    n_chunks = q_ref.shape[0] // c

    def chunk_rows(r):
        full = gt_ref[pl.ds(r, 1), :]
        return jnp.concatenate([full[:, n * c:(n + 1) * c] for n in range(n_chunks)], axis=0)

    beta_all = chunk_rows(head)
    gh, gm, gl = _split3(chunk_rows(head + A_HEADS))
    gc_all = (jnp.dot(gh, cum_rhs, preferred_element_type=F32)
              + jnp.dot(gm, cum_rhs, preferred_element_type=F32)
              + jnp.dot(gl, cum_rhs, preferred_element_type=F32))
    exp_all = jnp.exp(gc_all)
    last_all = jnp.broadcast_to(gc_all[:, c - 1:c], gc_all.shape)
    rowforms = (beta_all, gc_all, exp_all, beta_all * exp_all, jnp.exp(last_all - gc_all),
                jnp.exp(last_all))
    for idx, val in enumerate(rowforms):
        for n in range(n_groups):
            rf_ref[idx, n] = val[n * GROUP:(n + 1) * GROUP, :]

    def chunk_local(g):
        r0 = g * rows
        def silu_rows(ref):
            a = ref[pl.ds(r0, rows), :].astype(F32)
            return a * jax.nn.sigmoid(a)

        def l2n(a):
            return a * lax.rsqrt(jnp.sum(a * a, axis=-1, keepdims=True) + EPS)

        q3 = (l2n(silu_rows(q_ref)) * (A_DK ** -0.5)).reshape(GROUP, c, A_DK)
        k3 = l2n(silu_rows(k_ref)).reshape(GROUP, c, A_DK)
        v3 = silu_rows(v_ref).reshape(GROUP, c, A_DV)
        q3b = q3.astype(BF16)
        k3b = k3.astype(BF16)

        gc = rf_ref[1, g]
        beta_c = col(rf_ref[0, g])
        gc_c = col(gc)
        exp_gc_c = col(rf_ref[2, g])
        beta_exp_c = col(rf_ref[3, g])
        tail_c = col(rf_ref[4, g])
        e_last = rf_ref[5, g]
        e_last = jnp.concatenate([e_last, e_last], axis=-1)
        decay = jnp.exp(jnp.where(lower_incl[None], gc_c - gc[:, None, :], -jnp.inf))

        kb3 = (k3 * beta_c).astype(BF16)
        kk = jnp.einsum("cid,cjd->cij", kb3, k3b, preferred_element_type=F32)
        qk = jnp.einsum("cid,cjd->cij", q3b, k3b, preferred_element_type=F32) * decay
        low = jnp.where(lower_strict[None], kk * decay, 0.0)

        assert NEUMANN_BLOCK == 8
        m1 = jnp.where(same_block(NEUMANN_BLOCK)[None], -low, 0.0)
        m1b = m1.astype(BF16)
        tinv = eye[None] + m1
        m2b = _bdot(m1b, m1b).astype(BF16)
        tinv = tinv + _bdot(tinv.astype(BF16), m2b)
        m4b = _bdot(m2b, m2b).astype(BF16)
        tinv = tinv + _bdot(tinv.astype(BF16), m4b)
        size = NEUMANN_BLOCK
        while size < c:
            off = jnp.where((same_block(2 * size) & jnp.logical_not(same_block(size)))[None], low, 0.0)
            tb = tinv.astype(BF16)
            tinv = tinv - _bdot(_bdot(tb, off.astype(BF16)).astype(BF16), tb)
            size *= 2

        tb = tinv.astype(BF16)
        vb = (v3 * jnp.concatenate([beta_c, beta_c], axis=-1)).astype(BF16)
        kbg = (k3 * beta_exp_c).astype(BF16)
        u = _bdot(tb, vb)
        w = _bdot(tb, kbg)

        qt = q3 * exp_gc_c
        kd = (k3 * tail_c).astype(BF16)
        ub = u.astype(BF16)
        wb = w.astype(BF16)
        qkb = qk.astype(BF16)
        kdt = jnp.swapaxes(kd, 1, 2)
        a_neg = -_bdot(kdt, wb)
        b_add = _bdot(kdt, ub)
        q_eff = qt - _bdot(qkb, wb)
        o_loc = _bdot(qkb, ub)
        p_all = jnp.concatenate([a_neg, q_eff], axis=1).astype(BF16)
        return p_all, b_add, o_loc, e_last

    def project_out(row, gated):
        lhs = jnp.concatenate(gated, axis=0)
        y_ref[pl.ds(row, lhs.shape[0]), :] += jnp.dot(lhs, wo_ref[...], preferred_element_type=F32)

    def state_steps(g, p_all, b_add, o_loc, e_last):
        ng = ng_ref[...]
        ready = None
        gated = []
        for cc in range(GROUP):
            state = s_ref[...]
            r = jnp.dot(p_all[cc], state.astype(BF16), preferred_element_type=F32)
            o = r[A_DK:, :] + o_loc[cc]
            s_ref[...] = state * e_last[cc:cc + 1, :] + r[:A_DK, :] + b_add[cc]
            if ready is not None:
                project_out(*ready)
                ready = None
            on = o * lax.rsqrt(jnp.mean(o * o, axis=-1, keepdims=True) + EPS) * ng
            row = g * rows + cc * c
            zc = z_ref[pl.ds(row, c), :].astype(F32)
            gated.append((on * (zc * jax.nn.sigmoid(zc))).astype(BF16))
            if len(gated) == OUT_BATCH:
                ready = (row - (OUT_BATCH - 1) * c, gated)
                gated = []
        project_out(*ready)

    @pl.when(head == 0)
    def _():
        y_ref[...] = x_ref[...]

    local = chunk_local(0)
    for g in range(n_groups):
        nxt = chunk_local(g + 1) if g + 1 < n_groups else None
        state_steps(g, *local)
        local = nxt


def _gdn_core(proj3, gates_t, norm_g, w_out, x3):
    bsz, seq, _ = proj3.shape
    d = x3.shape[-1]
    kb = A_QK // A_DK
    vb = 2 * A_QK // A_DV
    zb = vb + A_VW // A_DV
    return pl.pallas_call(
        _gdn_kernel,
        grid=(bsz, A_HEADS),
        in_specs=[
            pl.BlockSpec((None, seq, A_DK), lambda b, h: (b, 0, h)),
            pl.BlockSpec((None, seq, A_DK), lambda b, h: (b, 0, kb + h)),
            pl.BlockSpec((None, seq, A_DV), lambda b, h: (b, 0, vb + h)),
            pl.BlockSpec((None, seq, A_DV), lambda b, h: (b, 0, zb + h)),
            pl.BlockSpec((2 * A_HEADS, seq), lambda b, h: (0, b)),
            pl.BlockSpec((1, A_DV), lambda b, h: (0, 0)),
            pl.BlockSpec((A_DV, d), lambda b, h: (h, 0)),
            pl.BlockSpec((None, seq, d), lambda b, h: (b, 0, 0)),
        ],
        out_specs=pl.BlockSpec((None, seq, d), lambda b, h: (b, 0, 0)),
        out_shape=jax.ShapeDtypeStruct((bsz, seq, d), F32),
        scratch_shapes=[
            pltpu.VMEM((A_DK, A_DV), F32),
            pltpu.VMEM((6, seq // (GROUP * CHUNK), GROUP, CHUNK), F32),
        ],
        compiler_params=pltpu.CompilerParams(
            dimension_semantics=("parallel", "arbitrary"), vmem_limit_bytes=VMEM_LIMIT_BYTES),
        name="gdn_core",
    )(proj3, proj3, proj3, proj3, gates_t, norm_g, w_out, x3)


def _rope_table_kernel(pos_ref, invf_ref, cos_ref, sin_ref):
    tr, per_row = pos_ref.shape
    nf = ROPE_HALF

    def dot_split(a, b, terms):
        parts = _split3(a)[:terms]
        out = jnp.dot(parts[0], b, preferred_element_type=F32)
        for part in parts[1:]:
            out = out + jnp.dot(part, b, preferred_element_type=F32)
        return out

    rk = lax.broadcasted_iota(jnp.int32, (per_row, B_DH), 0)
    rl = lax.broadcasted_iota(jnp.int32, (per_row, B_DH), 1)
    rep = jnp.where(rl // nf == rk, 1.0, 0.0).astype(BF16)
    ang = dot_split(pos_ref[...].astype(F32), rep, 3) * invf_ref[...]
    c = jnp.cos(ang)
    s = jnp.sin(ang)

    ek = lax.broadcasted_iota(jnp.int32, (B_DH, B_DH), 0)
    el = lax.broadcasted_iota(jnp.int32, (B_DH, B_DH), 1)
    lane = lax.broadcasted_iota(jnp.int32, (1, B_DH), 1)
    rotary = (lane < nf) | ((lane >= ROPE_SHIFT) & (lane < ROPE_SHIFT + nf))
    base = jnp.where(rotary, 0.0, 1.0)
    for p in range(per_row):
        mine = ek // nf == p
        first = mine & (el == ek % nf)
        second = mine & (el == ek % nf + ROPE_SHIFT)
        e_cos = jnp.where(first | second, 1.0, 0.0).astype(BF16)
        e_sin = (jnp.where(second, 1.0, 0.0) - jnp.where(first, 1.0, 0.0)).astype(BF16)
        rows = pl.ds(p, tr, stride=per_row)
        cos_ref[rows, :] = dot_split(c, e_cos, 2) + base
        sin_ref[rows, :] = dot_split(s, e_sin, 2)


def _rope_tables(positions):
    t = positions.size
    per_row = B_DH // ROPE_HALF
    inv_freq = ROPE_THETA ** (-jnp.arange(0, ROPE_DIMS, 2, dtype=F32) / ROPE_DIMS)
    invf = jnp.tile(inv_freq, per_row)[None, :]
    rows = t // per_row
    tr = min(rows, 512)
    return pl.pallas_call(
        _rope_table_kernel,
        grid=(rows // tr,),
        in_specs=[pl.BlockSpec((tr, per_row), lambda i: (i, 0)), pl.BlockSpec((1, B_DH), lambda i: (0, 0))],
        out_specs=[pl.BlockSpec((tr * per_row, B_DH), lambda i: (i, 0))] * 2,
        out_shape=[jax.ShapeDtypeStruct((t, B_DH), F32)] * 2,
        name="rope_tables",
    )(positions.reshape(rows, per_row), invf)


def _attn_proj_kernel(x_ref, g_ref, w0_ref, w1_ref, w2_ref, cos_ref, sin_ref, qg_ref, kg_ref,
                      p0_ref, p1_ref, p2_ref, hn_ref):
    tm = x_ref.shape[0]
    n_lane_blocks = hn_ref.shape[0]
    hn_nat = _normed_f32(x_ref, g_ref)
    for j in range(n_lane_blocks):
        hn_ref[j] = hn_nat[:, j * B_DH:(j + 1) * B_DH]

    def by_stream(ref, dil):
        if dil == 1:
            return ref[...]
        return jnp.concatenate([ref[pl.ds(r, tm // dil, stride=dil), :] for r in range(dil)], axis=0)

    def hn_by_stream(dil):
        if dil == 1:
            return hn_nat
        return jnp.concatenate([by_stream(hn_ref.at[j], dil) for j in range(n_lane_blocks)], axis=-1)

    def norm_rope(y, gain, cos, sin):
        y = y * lax.rsqrt(jnp.mean(y * y, axis=-1, keepdims=True) + EPS) * gain
        return y * cos + pltpu.roll(y, ROPE_SHIFT, 1) * sin

    w_refs = (w0_ref, w1_ref, w2_ref)
    for gi in range(B_GROUPS):
        dil = B_DILATIONS[gi]
        hn = hn_by_stream(dil).astype(BF16)
        cos = by_stream(cos_ref, dil)
        sin = by_stream(sin_ref, dil)
        w_ref = w_refs[gi]
        for n0 in range(0, w_ref.shape[1], COL_TILE):
            y = jnp.dot(hn, w_ref[:, n0:n0 + COL_TILE], preferred_element_type=F32)
            if n0 < 2 * B_W:
                gain = (qg_ref if n0 < B_W else kg_ref)[gi:gi + 1, :]
                y = jnp.concatenate(
                    [norm_rope(y[:, j * B_DH:(j + 1) * B_DH], gain, cos, sin)
                     for j in range(COL_TILE // B_DH)], axis=-1)
            y = y.astype(BF16)
            if gi == 0:
                p0_ref[:, n0:n0 + COL_TILE] = y
            else:
                out_ref = p1_ref if gi == 1 else p2_ref
                out_ref[:, :, n0:n0 + COL_TILE] = y.reshape(dil, tm // dil, COL_TILE)


def _attn_proj(x2, g, w0, w1, w2, cos, sin, q_gain, k_gain, bsz, seq):
    t, d = x2.shape
    tm = ATTN_ROW_TILE
    per_seq = seq // tm
    d1, d2 = B_DILATIONS[1], B_DILATIONS[2]
    return pl.pallas_call(
        _attn_proj_kernel,
        grid=(t // tm,),
        in_specs=[
            pl.BlockSpec((tm, d), lambda i: (i, 0)),
            _resident((1, d)),
            _resident(w0.shape),
            _resident(w1.shape),
            _resident(w2.shape),
            pl.BlockSpec((tm, B_DH), lambda i: (i, 0)),
            pl.BlockSpec((tm, B_DH), lambda i: (i, 0)),
            _resident((B_GROUPS, B_DH)),
            _resident((B_GROUPS, B_DH)),
        ],
        out_specs=[
            pl.BlockSpec((tm, w0.shape[1]), lambda i: (i, 0)),
            pl.BlockSpec((None, d1, tm // d1, w1.shape[1]), lambda i: (i // per_seq, 0, i % per_seq, 0)),
            pl.BlockSpec((None, d2, tm // d2, w2.shape[1]), lambda i: (i // per_seq, 0, i % per_seq, 0)),
        ],
        out_shape=[
            jax.ShapeDtypeStruct((t, w0.shape[1]), BF16),
            jax.ShapeDtypeStruct((bsz, d1, seq // d1, w1.shape[1]), BF16),
            jax.ShapeDtypeStruct((bsz, d2, seq // d2, w2.shape[1]), BF16),
        ],
        scratch_shapes=[pltpu.VMEM((d // B_DH, tm, B_DH), F32)],
        compiler_params=pltpu.CompilerParams(
            dimension_semantics=("parallel",), vmem_limit_bytes=VMEM_LIMIT_BYTES),
        name="attn_norm_proj",
    )(x2, g, w0, w1, w2, cos, sin, q_gain, k_gain)


def _attn_kernel(q0_ref, k0_ref, v0_ref, q1_ref, k1_ref, v1_ref, q2_ref, k2_ref, v2_ref, z_ref,
                 o_ref, og_ref, lg_ref):
    seq = q0_ref.shape[0]
    blk = B_BLOCK
    nb = ATTN_BATCH
    tile = nb * blk
    assert all(w // d == blk for w, d in zip(B_WINDOWS, B_DILATIONS))

    qi = lax.broadcasted_iota(jnp.int32, (nb, blk, 2 * blk), 1)
    kj = lax.broadcasted_iota(jnp.int32, (nb, blk, 2 * blk), 2)
    bi = lax.broadcasted_iota(jnp.int32, (nb, blk, 2 * blk), 0)
    band = ((kj < blk) & (kj >= qi)) | ((kj >= blk) & (kj - blk <= qi))
    band_first = band & ((kj >= blk) | (bi > 0))
    cur_only = (lax.broadcasted_iota(jnp.int32, (nb, blk, blk), 2)
                <= lax.broadcasted_iota(jnp.int32, (nb, blk, blk), 1))
    ones_v = jnp.ones((nb, blk, B_DH), BF16)
    zero_blk = jnp.zeros((1, blk, B_DH), BF16)

    q_refs = (q0_ref, q1_ref, q2_ref)
    k_refs = (k0_ref, k1_ref, k2_ref)
    v_refs = (v0_ref, v1_ref, v2_ref)

    def blocks(ref, first, count):
        return ref[pl.ds(first, count * blk), :].reshape(count, blk, B_DH)

    def scores(gi, first, starts_stream, has_prev):
        qb = blocks(q_refs[gi], first, nb)
        kc = blocks(k_refs[gi], first, nb)
        if not has_prev:
            s = jnp.einsum("bqd,bkd->bqk", qb, kc, preferred_element_type=F32)
            return jnp.where(cur_only, s, -jnp.inf)
        if starts_stream:
            kp = jnp.concatenate([zero_blk, kc[:nb - 1]], axis=0)
        else:
            kp = blocks(k_refs[gi], first - blk, nb)
        s = jnp.einsum("bqd,bkd->bqk", qb, jnp.concatenate([kp, kc], axis=1),
                       preferred_element_type=F32)
        return jnp.where(band_first if starts_stream else band, s, -jnp.inf)

    def finish(gi, first, starts_stream, has_prev, out_rows, s):
        m = jnp.max(s, axis=-1, keepdims=True)
        p = jnp.exp(s - m).astype(BF16)
        vc = jnp.concatenate([blocks(v_refs[gi], first, nb), ones_v], axis=-1)
        if has_prev:
            if starts_stream:
                vp = jnp.concatenate([jnp.zeros((1, blk, 2 * B_DH), BF16), vc[:nb - 1]], axis=0)
            else:
                vp = jnp.concatenate([blocks(v_refs[gi], first - blk, nb), ones_v], axis=-1)
            vc = jnp.concatenate([vp, vc], axis=1)
        acc = jnp.einsum("bqk,bkd->bqd", p, vc, preferred_element_type=F32)
        den = acc[..., B_DH:]
        o = acc[..., :B_DH] / den
        lse = m + jnp.log(den)
        b0 = 0
        for rows, count in out_rows:
            og_ref[gi, rows, :] = o[b0:b0 + count].reshape(count * blk, B_DH)
            lg_ref[gi, rows, :] = lse[b0:b0 + count].reshape(count * blk, B_DH)
            b0 += count

    batches = []
    for gi in range(B_GROUPS):
        dil = B_DILATIONS[gi]
        length = seq // dil
        per_stream = length // blk
        for first in range(0, seq, tile):
            r, m0 = divmod(first, length)
            if per_stream >= nb:
                t0 = r + m0 * dil
                rows = pl.ds(t0, tile) if dil == 1 else pl.ds(t0, tile, stride=dil)
                batches.append((gi, first, m0 == 0, True, [(rows, nb)]))
            else:
                assert per_stream == 1
                rows = [(pl.ds(r + j, blk, stride=dil), 1) for j in range(nb)]
                batches.append((gi, first, True, False, rows))

    pending = {}
    for i in range(len(batches) + ATTN_LOOKAHEAD):
        if i < len(batches):
            pending[i] = scores(*batches[i][:4])
        if i >= ATTN_LOOKAHEAD:
            j = i - ATTN_LOOKAHEAD
            finish(*batches[j], pending.pop(j))

    def merge(t, carry):
        r0 = pl.multiple_of(t * tile, tile)
        l0 = lg_ref[0, pl.ds(r0, tile), :]
        l1 = lg_ref[1, pl.ds(r0, tile), :]
        l2 = lg_ref[2, pl.ds(r0, tile), :]
        m = jnp.maximum(jnp.maximum(l0, l1), l2)
        w0 = jnp.exp(l0 - m)
        w1 = jnp.exp(l1 - m)
        w2 = jnp.exp(l2 - m)
        num = (w0 * og_ref[0, pl.ds(r0, tile), :] + w1 * og_ref[1, pl.ds(r0, tile), :]
               + w2 * og_ref[2, pl.ds(r0, tile), :])
        o = num / (w0 + w1 + w2)
        z = z_ref[pl.ds(r0, tile), :].astype(F32)
        o_ref[pl.ds(r0, tile), :] = (o * (z * jax.nn.sigmoid(z))).astype(o_ref.dtype)
        return carry

    lax.fori_loop(0, seq // tile, merge, 0)


def _attn_core(p0, p1, p2):
    bsz, seq, _ = p0.shape

    def cols(which):
        return pl.BlockSpec((None, seq, B_DH), lambda b, h: (b, 0, which * B_HEADS + h))

    qkv = [cols(0), cols(1), cols(2)]
    return pl.pallas_call(
        _attn_kernel,
        grid=(bsz, B_HEADS),
        in_specs=[*qkv, *qkv, *qkv, cols(3)],
        out_specs=pl.BlockSpec((None, seq, B_DH), lambda b, h: (b, 0, h)),
        out_shape=jax.ShapeDtypeStruct((bsz, seq, B_W), BF16),
        scratch_shapes=[
            pltpu.VMEM((B_GROUPS, seq, B_DH), F32),
            pltpu.VMEM((B_GROUPS, seq, B_DH), F32),
        ],
        compiler_params=pltpu.CompilerParams(
            dimension_semantics=("parallel", "parallel"), vmem_limit_bytes=VMEM_LIMIT_BYTES),
        name="dilated_attn",
    )(p0, p0, p0, p1, p1, p1, p2, p2, p2, p0)


def _rotary_lane_order(a):
    gap = ROPE_SHIFT - ROPE_HALF
    return jnp.concatenate(
        [a[..., :ROPE_HALF], a[..., ROPE_DIMS:ROPE_DIMS + gap], a[..., ROPE_HALF:ROPE_DIMS],
         a[..., ROPE_DIMS + gap:]], axis=-1)


def _attn_weights_kernel(w_ref, o_ref):
    w = w_ref[...].astype(BF16)

    @pl.when(pl.program_id(0) < 2)
    def _():
        src = lax.broadcasted_iota(jnp.int32, (B_DH, B_DH), 0)
        dst = lax.broadcasted_iota(jnp.int32, (B_DH, B_DH), 1)
        gap = ROPE_SHIFT - ROPE_HALF
        wanted = jnp.where(dst < ROPE_HALF, dst,
                           jnp.where(dst < ROPE_HALF + gap, dst + (ROPE_DIMS - ROPE_HALF),
                                     jnp.where(dst < ROPE_DIMS + gap, dst - gap, dst)))
        perm = jnp.where(src == wanted, 1.0, 0.0).astype(BF16)
        for j in range(B_HEADS):
            cols = slice(j * B_DH, (j + 1) * B_DH)
            o_ref[:, cols] = jnp.dot(w[:, cols], perm, preferred_element_type=F32).astype(BF16)

    @pl.when(pl.program_id(0) >= 2)
    def _():
        o_ref[...] = w


def _attn_group_weights(wb, gi, with_gate):
    d = wb.shape[0]
    n_blocks = 4 if with_gate else 3
    gate_block = 3 * B_GROUPS

    def col_block(which):
        return (0, jnp.where(which < 3, which * B_GROUPS + gi, gate_block))

    return pl.pallas_call(
        _attn_weights_kernel,
        grid=(n_blocks,),
        in_specs=[pl.BlockSpec((d, B_W), col_block)],
        out_specs=pl.BlockSpec((d, B_W), lambda which: (0, which)),
        out_shape=jax.ShapeDtypeStruct((d, n_blocks * B_W), BF16),
        name=f"attn_weights_g{gi}",
    )(wb)


def kernel(x, positions, norm_g, a_w_in, a_conv_w, a_log, a_dt_bias, a_norm_g, a_w_out,
           b_w_in, b_q_norm_g, b_k_norm_g, b_w_out):
    bsz, seq, d = x.shape
    t = bsz * seq
    x2 = x.reshape(t, d)

    w_main = a_w_in[0, :, :A_MAIN].astype(BF16)
    w_gate_t = a_w_in[0, :, A_MAIN:].T.astype(BF16)
    zeros8 = jnp.zeros((A_HEADS,), F32)
    alog16 = jnp.concatenate([zeros8, a_log[0].astype(F32)])[:, None]
    dtb16 = jnp.concatenate([zeros8, a_dt_bias[0].astype(F32)])[:, None]
    proj, gates_t = _gdn_proj(x2, norm_g[0][None, :], w_main, w_gate_t, alog16, dtb16,
                              a_conv_w[0], seq)
    x3 = _gdn_core(proj.reshape(bsz, seq, A_MAIN), gates_t, a_norm_g[0][None, :],
                   a_w_out[0].astype(BF16), x)
    x2 = x3.reshape(t, d)

    w0 = _attn_group_weights(b_w_in[0], 0, with_gate=True)
    w1 = _attn_group_weights(b_w_in[0], 1, with_gate=False)
    w2 = _attn_group_weights(b_w_in[0], 2, with_gate=False)
    q_gain = _rotary_lane_order(b_q_norm_g[0]) * (B_DH ** -0.5)
    k_gain = _rotary_lane_order(b_k_norm_g[0])
    cos, sin = _rope_tables(positions)
    p0, p1, p2 = _attn_proj(x2, norm_g[1][None, :], w0, w1, w2, cos, sin, q_gain, k_gain, bsz, seq)
    o = _attn_core(p0.reshape(bsz, seq, -1), p1.reshape(bsz, seq, -1), p2.reshape(bsz, seq, -1))
    x2 = _out_proj(o.reshape(t, B_W), b_w_out[0].astype(BF16), x2, "attn_out_proj")
    return x2.reshape(bsz, seq, d)
```

```python
import functools

import jax
import jax.numpy as jnp
from jax import lax
from jax.experimental import pallas as pl
from jax.experimental.pallas import tpu as pltpu

F32 = jnp.float32
BF16 = jnp.bfloat16
EPS = 1e-6

D_MODEL = 1024
A_HEADS = 8
A_DK = 128
A_DV = 256
A_QK = A_HEADS * A_DK
A_VW = A_HEADS * A_DV
A_CONVW = 2 * A_QK + A_VW
A_MAIN = A_CONVW + A_VW
A_CONV = 4
CHUNK = 128
GROUP = 16
OUT_BATCH = 4
NEUMANN_BLOCK = 8
B_WINDOWS = (128, 512, 2048)
B_DILATIONS = (1, 4, 16)
B_GROUPS = 3
B_HEADS = 8
B_DH = 128
B_W = B_HEADS * B_DH
B_BLOCK = 128
ROPE_THETA = 500000.0
ROPE_DIMS = B_DH // 4
ROPE_HALF = ROPE_DIMS // 2
ROPE_SHIFT = B_DH // 2
ATTN_BATCH = 4
ATTN_LOOKAHEAD = 2

VMEM_LIMIT_BYTES = 56 * 1024 * 1024
OUT_ROW_TILE = 1024
GDN_ROW_TILE = 256
ATTN_ROW_TILE = 256
COL_TILE = 512
HALO = 16


def _split3(a):
    hi = a.astype(BF16)
    r = a - hi.astype(F32)
    mid = r.astype(BF16)
    lo = (r - mid.astype(F32)).astype(BF16)
    return hi, mid, lo


def _bdot(a, b):
    return jnp.einsum("cik,ckj->cij", a, b, preferred_element_type=F32)


def _normed_f32(x_ref, g_ref):
    x = x_ref[...]
    ms = jnp.mean(x * x, axis=-1, keepdims=True)
    return x * lax.rsqrt(ms + EPS) * g_ref[...]


def _resident(shape):
    return pl.BlockSpec(shape, lambda i: (0,) * len(shape), pipeline_mode=pl.Buffered(1))


def _gdn_proj_kernel(x_ref, xh_ref, g_ref, w_ref, wgt_ref, alog_ref, dtb_ref, cw_ref,
                     o_ref, gates_ref, *, tiles_per_seq):
    tm = x_ref.shape[0]
    first = (pl.program_id(0) % tiles_per_seq) == 0
    hn = _normed_f32(x_ref, g_ref).astype(BF16)
    halo = jnp.where(first, 0.0, _normed_f32(xh_ref, g_ref)).astype(BF16)
    hcat = jnp.concatenate([halo, hn], axis=0)

    for n0 in range(0, A_CONVW, COL_TILE):
        y = jnp.dot(hcat, w_ref[:, n0:n0 + COL_TILE], preferred_element_type=F32)
        cw = cw_ref[:, n0:n0 + COL_TILE]
        acc = y * cw[A_CONV - 1:A_CONV, :]
        for tap in range(A_CONV - 1):
            acc = acc + pltpu.roll(y, A_CONV - 1 - tap, 0) * cw[tap:tap + 1, :]
        o_ref[:, n0:n0 + COL_TILE] = acc[HALO:, :].astype(o_ref.dtype)

    for n0 in range(A_CONVW, A_MAIN, COL_TILE):
        o_ref[:, n0:n0 + COL_TILE] = jnp.dot(
            hn, w_ref[:, n0:n0 + COL_TILE], preferred_element_type=F32).astype(o_ref.dtype)

    logits = lax.dot_general(wgt_ref[...], hn, (((1,), (1,)), ((), ())), preferred_element_type=F32)
    row = lax.broadcasted_iota(jnp.int32, logits.shape, 0)
    xs = logits + dtb_ref[...]
    softplus = jnp.maximum(xs, 0.0) + jnp.log1p(jnp.exp(-jnp.abs(xs)))
    decay = -jnp.exp(alog_ref[...]) * softplus
    gates_ref[...] = jnp.where(row < A_HEADS, jax.nn.sigmoid(logits), decay)


def _gdn_proj(x2, g, w, wgt, alog16, dtb16, conv_w, seq):
    t, d = x2.shape
    tm = GDN_ROW_TILE
    kern = functools.partial(_gdn_proj_kernel, tiles_per_seq=seq // tm)
    return pl.pallas_call(
        kern,
        grid=(t // tm,),
        in_specs=[
            pl.BlockSpec((tm, d), lambda i: (i, 0)),
            pl.BlockSpec((HALO, d), lambda i: (jnp.maximum(i * (tm // HALO) - 1, 0), 0)),
            _resident((1, d)),
            _resident((d, A_MAIN)),
            _resident((2 * A_HEADS, d)),
            _resident((2 * A_HEADS, 1)),
            _resident((2 * A_HEADS, 1)),
            _resident((A_CONV, A_CONVW)),
        ],
        out_specs=[
            pl.BlockSpec((tm, A_MAIN), lambda i: (i, 0)),
            pl.BlockSpec((2 * A_HEADS, tm), lambda i: (0, i)),
        ],
        out_shape=[
            jax.ShapeDtypeStruct((t, A_MAIN), BF16),
            jax.ShapeDtypeStruct((2 * A_HEADS, t), F32),
        ],
        compiler_params=pltpu.CompilerParams(
            dimension_semantics=("parallel",), vmem_limit_bytes=VMEM_LIMIT_BYTES),
        name="gdn_norm_proj",
    )(x2, x2, g, w, wgt, alog16, dtb16, conv_w)


def _out_proj_kernel(o_ref, w_ref, x_ref, y_ref):
    y_ref[...] = x_ref[...] + jnp.dot(o_ref[...], w_ref[...], preferred_element_type=F32)


def _out_proj(o2, w, x2, name):
    t, k = o2.shape
    d = w.shape[1]
    tm = OUT_ROW_TILE
    return pl.pallas_call(
        _out_proj_kernel,
        grid=(t // tm,),
        in_specs=[
            pl.BlockSpec((tm, k), lambda i: (i, 0)),
            _resident((k, d)),
            pl.BlockSpec((tm, d), lambda i: (i, 0)),
        ],
        out_specs=pl.BlockSpec((tm, d), lambda i: (i, 0)),
        out_shape=jax.ShapeDtypeStruct((t, d), F32),
        compiler_params=pltpu.CompilerParams(
            dimension_semantics=("parallel",), vmem_limit_bytes=VMEM_LIMIT_BYTES),
        name=name,
    )(o2, w, x2)


def _gdn_kernel(q_ref, k_ref, v_ref, z_ref, gt_ref, ng_ref, wo_ref, x_ref, y_ref, s_ref, rf_ref):
    head = pl.program_id(1)
    rows = GROUP * CHUNK
    n_groups = q_ref.shape[0] // rows
    c = CHUNK

    ii = lax.broadcasted_iota(jnp.int32, (c, c), 0)
    jj = lax.broadcasted_iota(jnp.int32, (c, c), 1)
    eye = jnp.where(ii == jj, 1.0, 0.0).astype(F32)
    lower_incl = ii >= jj
    lower_strict = ii > jj
    cum_rhs = jnp.where(ii <= jj, 1.0, 0.0).astype(BF16)

    def same_block(size):
        return (ii // size) == (jj // size)

    def col(x):
        return jnp.swapaxes(jnp.broadcast_to(x[:, None, :], (GROUP, c, c)), 1, 2)

    s_ref[...] = jnp.zeros_like(s_ref)

    n_chunks = q_ref.shape[0] // c

    def chunk_rows(r):
        full = gt_ref[pl.ds(r, 1), :]
        return jnp.concatenate([full[:, n * c:(n + 1) * c] for n in range(n_chunks)], axis=0)

    beta_all = chunk_rows(head)
    gh, gm, gl = _split3(chunk_rows(head + A_HEADS))
    gc_all = (jnp.dot(gh, cum_rhs, preferred_element_type=F32)
              + jnp.dot(gm, cum_rhs, preferred_element_type=F32)
              + jnp.dot(gl, cum_rhs, preferred_element_type=F32))
    exp_all = jnp.exp(gc_all)
    last_all = jnp.broadcast_to(gc_all[:, c - 1:c], gc_all.shape)
    rowforms = (beta_all, gc_all, exp_all, beta_all * exp_all, jnp.exp(last_all - gc_all),
                jnp.exp(last_all))
    for idx, val in enumerate(rowforms):
        for n in range(n_groups):
            rf_ref[idx, n] = val[n * GROUP:(n + 1) * GROUP, :]

    def chunk_local(g):
        r0 = g * rows
        def silu_rows(ref):
            a = ref[pl.ds(r0, rows), :].astype(F32)
            return a * jax.nn.sigmoid(a)

        def l2n(a):
            return a * lax.rsqrt(jnp.sum(a * a, axis=-1, keepdims=True) + EPS)

        q3 = (l2n(silu_rows(q_ref)) * (A_DK ** -0.5)).reshape(GROUP, c, A_DK)
        k3 = l2n(silu_rows(k_ref)).reshape(GROUP, c, A_DK)
        v3 = silu_rows(v_ref).reshape(GROUP, c, A_DV)
        q3b = q3.astype(BF16)
        k3b = k3.astype(BF16)

        gc = rf_ref[1, g]
        beta_c = col(rf_ref[0, g])
        gc_c = col(gc)
        exp_gc_c = col(rf_ref[2, g])
        beta_exp_c = col(rf_ref[3, g])
        tail_c = col(rf_ref[4, g])
        e_last = rf_ref[5, g]
        e_last = jnp.concatenate([e_last, e_last], axis=-1)
        decay = jnp.exp(jnp.where(lower_incl[None], gc_c - gc[:, None, :], -jnp.inf))

        kb3 = (k3 * beta_c).astype(BF16)
        kk = jnp.einsum("cid,cjd->cij", kb3, k3b, preferred_element_type=F32)
        qk = jnp.einsum("cid,cjd->cij", q3b, k3b, preferred_element_type=F32) * decay
        low = jnp.where(lower_strict[None], kk * decay, 0.0)

        assert NEUMANN_BLOCK == 8
        m1 = jnp.where(same_block(NEUMANN_BLOCK)[None], -low, 0.0)
        m1b = m1.astype(BF16)
        tinv = eye[None] + m1
        m2b = _bdot(m1b, m1b).astype(BF16)
        tinv = tinv + _bdot(tinv.astype(BF16), m2b)
        m4b = _bdot(m2b, m2b).astype(BF16)
        tinv = tinv + _bdot(tinv.astype(BF16), m4b)
        size = NEUMANN_BLOCK
        while size < c:
            off = jnp.where((same_block(2 * size) & jnp.logical_not(same_block(size)))[None], low, 0.0)
            tb = tinv.astype(BF16)
            tinv = tinv - _bdot(_bdot(tb, off.astype(BF16)).astype(BF16), tb)
            size *= 2

        tb = tinv.astype(BF16)
        vb = (v3 * jnp.concatenate([beta_c, beta_c], axis=-1)).astype(BF16)
        kbg = (k3 * beta_exp_c).astype(BF16)
        u = _bdot(tb, vb)
        w = _bdot(tb, kbg)

        qt = q3 * exp_gc_c
        kd = (k3 * tail_c).astype(BF16)
        ub = u.astype(BF16)
        wb = w.astype(BF16)
        qkb = qk.astype(BF16)
        kdt = jnp.swapaxes(kd, 1, 2)
        a_neg = -_bdot(kdt, wb)
        b_add = _bdot(kdt, ub)
        q_eff = qt - _bdot(qkb, wb)
        o_loc = _bdot(qkb, ub)
        p_all = jnp.concatenate([a_neg, q_eff], axis=1).astype(BF16)
        return p_all, b_add, o_loc, e_last

    def project_out(row, gated):
        lhs = jnp.concatenate(gated, axis=0)
        y_ref[pl.ds(row, lhs.shape[0]), :] += jnp.dot(lhs, wo_ref[...], preferred_element_type=F32)

    def state_steps(g, p_all, b_add, o_loc, e_last):
        ng = ng_ref[...]
        ready = None
        gated = []
        for cc in range(GROUP):
            state = s_ref[...]
            r = jnp.dot(p_all[cc], state.astype(BF16), preferred_element_type=F32)
            o = r[A_DK:, :] + o_loc[cc]
            s_ref[...] = state * e_last[cc:cc + 1, :] + r[:A_DK, :] + b_add[cc]
            if ready is not None:
                project_out(*ready)
                ready = None
            on = o * lax.rsqrt(jnp.mean(o * o, axis=-1, keepdims=True) + EPS) * ng
            row = g * rows + cc * c
            zc = z_ref[pl.ds(row, c), :].astype(F32)
            gated.append((on * (zc * jax.nn.sigmoid(zc))).astype(BF16))
            if len(gated) == OUT_BATCH:
                ready = (row - (OUT_BATCH - 1) * c, gated)
                gated = []
        project_out(*ready)

    @pl.when(head == 0)
    def _():
        y_ref[...] = x_ref[...]

    local = chunk_local(0)
    for g in range(n_groups):
        nxt = chunk_local(g + 1) if g + 1 < n_groups else None
        state_steps(g, *local)
        local = nxt


def _gdn_core(proj3, gates_t, norm_g, w_out, x3):
    bsz, seq, _ = proj3.shape
    d = x3.shape[-1]
    kb = A_QK // A_DK
    vb = 2 * A_QK // A_DV
    zb = vb + A_VW // A_DV
    return pl.pallas_call(
        _gdn_kernel,
        grid=(bsz, A_HEADS),
        in_specs=[
            pl.BlockSpec((None, seq, A_DK), lambda b, h: (b, 0, h)),
            pl.BlockSpec((None, seq, A_DK), lambda b, h: (b, 0, kb + h)),
            pl.BlockSpec((None, seq, A_DV), lambda b, h: (b, 0, vb + h)),
            pl.BlockSpec((None, seq, A_DV), lambda b, h: (b, 0, zb + h)),
            pl.BlockSpec((2 * A_HEADS, seq), lambda b, h: (0, b)),
            pl.BlockSpec((1, A_DV), lambda b, h: (0, 0)),
            pl.BlockSpec((A_DV, d), lambda b, h: (h, 0)),
            pl.BlockSpec((None, seq, d), lambda b, h: (b, 0, 0)),
        ],
        out_specs=pl.BlockSpec((None, seq, d), lambda b, h: (b, 0, 0)),
        out_shape=jax.ShapeDtypeStruct((bsz, seq, d), F32),
        scratch_shapes=[
            pltpu.VMEM((A_DK, A_DV), F32),
            pltpu.VMEM((6, seq // (GROUP * CHUNK), GROUP, CHUNK), F32),
        ],
        compiler_params=pltpu.CompilerParams(
            dimension_semantics=("parallel", "arbitrary"), vmem_limit_bytes=VMEM_LIMIT_BYTES),
        name="gdn_core",
    )(proj3, proj3, proj3, proj3, gates_t, norm_g, w_out, x3)


def _rope_table_kernel(pos_ref, invf_ref, cos_ref, sin_ref):
    tr, per_row = pos_ref.shape
    nf = ROPE_HALF

    def dot_split(a, b, terms):
        parts = _split3(a)[:terms]
        out = jnp.dot(parts[0], b, preferred_element_type=F32)
        for part in parts[1:]:
            out = out + jnp.dot(part, b, preferred_element_type=F32)
        return out

    rk = lax.broadcasted_iota(jnp.int32, (per_row, B_DH), 0)
    rl = lax.broadcasted_iota(jnp.int32, (per_row, B_DH), 1)
    rep = jnp.where(rl // nf == rk, 1.0, 0.0).astype(BF16)
    ang = dot_split(pos_ref[...].astype(F32), rep, 3) * invf_ref[...]
    c = jnp.cos(ang)
    s = jnp.sin(ang)

    ek = lax.broadcasted_iota(jnp.int32, (B_DH, B_DH), 0)
    el = lax.broadcasted_iota(jnp.int32, (B_DH, B_DH), 1)
    lane = lax.broadcasted_iota(jnp.int32, (1, B_DH), 1)
    rotary = (lane < nf) | ((lane >= ROPE_SHIFT) & (lane < ROPE_SHIFT + nf))
    base = jnp.where(rotary, 0.0, 1.0)
    for p in range(per_row):
        mine = ek // nf == p
        first = mine & (el == ek % nf)
        second = mine & (el == ek % nf + ROPE_SHIFT)
        e_cos = jnp.where(first | second, 1.0, 0.0).astype(BF16)
        e_sin = (jnp.where(second, 1.0, 0.0) - jnp.where(first, 1.0, 0.0)).astype(BF16)
        rows = pl.ds(p, tr, stride=per_row)
        cos_ref[rows, :] = dot_split(c, e_cos, 2) + base
        sin_ref[rows, :] = dot_split(s, e_sin, 2)


def _rope_tables(positions):
    t = positions.size
    per_row = B_DH // ROPE_HALF
    inv_freq = ROPE_THETA ** (-jnp.arange(0, ROPE_DIMS, 2, dtype=F32) / ROPE_DIMS)
    invf = jnp.tile(inv_freq, per_row)[None, :]
    rows = t // per_row
    tr = min(rows, 512)
    return pl.pallas_call(
        _rope_table_kernel,
        grid=(rows // tr,),
        in_specs=[pl.BlockSpec((tr, per_row), lambda i: (i, 0)), pl.BlockSpec((1, B_DH), lambda i: (0, 0))],
        out_specs=[pl.BlockSpec((tr * per_row, B_DH), lambda i: (i, 0))] * 2,
        out_shape=[jax.ShapeDtypeStruct((t, B_DH), F32)] * 2,
        name="rope_tables",
    )(positions.reshape(rows, per_row), invf)


def _attn_proj_kernel(x_ref, g_ref, w0_ref, w1_ref, w2_ref, cos_ref, sin_ref, qg_ref, kg_ref,
                      p0_ref, p1_ref, p2_ref, hn_ref):
    tm = x_ref.shape[0]
    n_lane_blocks = hn_ref.shape[0]
    hn_nat = _normed_f32(x_ref, g_ref)
    for j in range(n_lane_blocks):
        hn_ref[j] = hn_nat[:, j * B_DH:(j + 1) * B_DH]

    def by_stream(ref, dil):
        if dil == 1:
            return ref[...]
        return jnp.concatenate([ref[pl.ds(r, tm // dil, stride=dil), :] for r in range(dil)], axis=0)

    def hn_by_stream(dil):
        if dil == 1:
            return hn_nat
        return jnp.concatenate([by_stream(hn_ref.at[j], dil) for j in range(n_lane_blocks)], axis=-1)

    def norm_rope(y, gain, cos, sin):
        y = y * lax.rsqrt(jnp.mean(y * y, axis=-1, keepdims=True) + EPS) * gain
        return y * cos + pltpu.roll(y, ROPE_SHIFT, 1) * sin

    w_refs = (w0_ref, w1_ref, w2_ref)
    for gi in range(B_GROUPS):
        dil = B_DILATIONS[gi]
        hn = hn_by_stream(dil).astype(BF16)
        cos = by_stream(cos_ref, dil)
        sin = by_stream(sin_ref, dil)
        w_ref = w_refs[gi]
        for n0 in range(0, w_ref.shape[1], COL_TILE):
            y = jnp.dot(hn, w_ref[:, n0:n0 + COL_TILE], preferred_element_type=F32)
            if n0 < 2 * B_W:
                gain = (qg_ref if n0 < B_W else kg_ref)[gi:gi + 1, :]
                y = jnp.concatenate(
                    [norm_rope(y[:, j * B_DH:(j + 1) * B_DH], gain, cos, sin)
                     for j in range(COL_TILE // B_DH)], axis=-1)
            elif n0 >= 3 * B_W:
                y = y * jax.nn.sigmoid(y)
            y = y.astype(BF16)
            if gi == 0:
                p0_ref[:, n0:n0 + COL_TILE] = y
            else:
                out_ref = p1_ref if gi == 1 else p2_ref
                out_ref[:, :, n0:n0 + COL_TILE] = y.reshape(dil, tm // dil, COL_TILE)


def _attn_proj(x2, g, w0, w1, w2, cos, sin, q_gain, k_gain, bsz, seq):
    t, d = x2.shape
    tm = ATTN_ROW_TILE
    per_seq = seq // tm
    d1, d2 = B_DILATIONS[1], B_DILATIONS[2]
    return pl.pallas_call(
        _attn_proj_kernel,
        grid=(t // tm,),
        in_specs=[
            pl.BlockSpec((tm, d), lambda i: (i, 0)),
            _resident((1, d)),
            _resident(w0.shape),
            _resident(w1.shape),
            _resident(w2.shape),
            pl.BlockSpec((tm, B_DH), lambda i: (i, 0)),
            pl.BlockSpec((tm, B_DH), lambda i: (i, 0)),
            _resident((B_GROUPS, B_DH)),
            _resident((B_GROUPS, B_DH)),
        ],
        out_specs=[
            pl.BlockSpec((tm, w0.shape[1]), lambda i: (i, 0)),
            pl.BlockSpec((None, d1, tm // d1, w1.shape[1]), lambda i: (i // per_seq, 0, i % per_seq, 0)),
            pl.BlockSpec((None, d2, tm // d2, w2.shape[1]), lambda i: (i // per_seq, 0, i % per_seq, 0)),
        ],
        out_shape=[
            jax.ShapeDtypeStruct((t, w0.shape[1]), BF16),
            jax.ShapeDtypeStruct((bsz, d1, seq // d1, w1.shape[1]), BF16),
            jax.ShapeDtypeStruct((bsz, d2, seq // d2, w2.shape[1]), BF16),
        ],
        scratch_shapes=[pltpu.VMEM((d // B_DH, tm, B_DH), F32)],
        compiler_params=pltpu.CompilerParams(
            dimension_semantics=("parallel",), vmem_limit_bytes=VMEM_LIMIT_BYTES),
        name="attn_norm_proj",
    )(x2, g, w0, w1, w2, cos, sin, q_gain, k_gain)


def _attn_kernel(q0_ref, k0_ref, v0_ref, q1_ref, k1_ref, v1_ref, q2_ref, k2_ref, v2_ref, z_ref,
                 o_ref, og_ref, lg_ref):
    seq = q0_ref.shape[0]
    blk = B_BLOCK
    nb = ATTN_BATCH
    tile = nb * blk
    assert all(w // d == blk for w, d in zip(B_WINDOWS, B_DILATIONS))

    qi = lax.broadcasted_iota(jnp.int32, (nb, blk, 2 * blk), 1)
    kj = lax.broadcasted_iota(jnp.int32, (nb, blk, 2 * blk), 2)
    bi = lax.broadcasted_iota(jnp.int32, (nb, blk, 2 * blk), 0)
    band = ((kj < blk) & (kj >= qi)) | ((kj >= blk) & (kj - blk <= qi))
    band_first = band & ((kj >= blk) | (bi > 0))
    cur_only = (lax.broadcasted_iota(jnp.int32, (nb, blk, blk), 2)
                <= lax.broadcasted_iota(jnp.int32, (nb, blk, blk), 1))
    ones_v = jnp.ones((nb, blk, B_DH), BF16)
    zero_blk = jnp.zeros((1, blk, B_DH), BF16)

    q_refs = (q0_ref, q1_ref, q2_ref)
    k_refs = (k0_ref, k1_ref, k2_ref)
    v_refs = (v0_ref, v1_ref, v2_ref)

    def blocks(ref, first, count):
        return ref[pl.ds(first, count * blk), :].reshape(count, blk, B_DH)

    def scores(gi, first, starts_stream, has_prev):
        qb = blocks(q_refs[gi], first, nb)
        kc = blocks(k_refs[gi], first, nb)
        if not has_prev:
            s = jnp.einsum("bqd,bkd->bqk", qb, kc, preferred_element_type=F32)
            return jnp.where(cur_only, s, -jnp.inf)
        if starts_stream:
            kp = jnp.concatenate([zero_blk, kc[:nb - 1]], axis=0)
        else:
            kp = blocks(k_refs[gi], first - blk, nb)
        s = jnp.einsum("bqd,bkd->bqk", qb, jnp.concatenate([kp, kc], axis=1),
                       preferred_element_type=F32)
        return jnp.where(band_first if starts_stream else band, s, -jnp.inf)

    def finish(gi, first, starts_stream, has_prev, out_rows, s):
        m = jnp.max(s, axis=-1, keepdims=True)
        p = jnp.exp(s - m).astype(BF16)
        vc = jnp.concatenate([blocks(v_refs[gi], first, nb), ones_v], axis=-1)
        if has_prev:
            if starts_stream:
                vp = jnp.concatenate([jnp.zeros((1, blk, 2 * B_DH), BF16), vc[:nb - 1]], axis=0)
            else:
                vp = jnp.concatenate([blocks(v_refs[gi], first - blk, nb), ones_v], axis=-1)
            vc = jnp.concatenate([vp, vc], axis=1)
        acc = jnp.einsum("bqk,bkd->bqd", p, vc, preferred_element_type=F32)
        den = acc[..., B_DH:]
        o = acc[..., :B_DH] / den
        lse = m + jnp.log(den)
        b0 = 0
        for rows, count in out_rows:
            og_ref[gi, rows, :] = o[b0:b0 + count].reshape(count * blk, B_DH)
            lg_ref[gi, rows, :] = lse[b0:b0 + count].reshape(count * blk, B_DH)
            b0 += count

    batches = []
    for gi in range(B_GROUPS):
        dil = B_DILATIONS[gi]
        length = seq // dil
        per_stream = length // blk
        for first in range(0, seq, tile):
            r, m0 = divmod(first, length)
            if per_stream >= nb:
                t0 = r + m0 * dil
                rows = pl.ds(t0, tile) if dil == 1 else pl.ds(t0, tile, stride=dil)
                batches.append((gi, first, m0 == 0, True, [(rows, nb)]))
            else:
                assert per_stream == 1
                rows = [(pl.ds(r + j, blk, stride=dil), 1) for j in range(nb)]
                batches.append((gi, first, True, False, rows))

    pending = {}
    for i in range(len(batches) + ATTN_LOOKAHEAD):
        if i < len(batches):
            pending[i] = scores(*batches[i][:4])
        if i >= ATTN_LOOKAHEAD:
            j = i - ATTN_LOOKAHEAD
            finish(*batches[j], pending.pop(j))

    def merge(t, carry):
        r0 = pl.multiple_of(t * tile, tile)
        l0 = lg_ref[0, pl.ds(r0, tile), :]
        l1 = lg_ref[1, pl.ds(r0, tile), :]
        l2 = lg_ref[2, pl.ds(r0, tile), :]
        m = jnp.maximum(jnp.maximum(l0, l1), l2)
        w0 = jnp.exp(l0 - m)
        w1 = jnp.exp(l1 - m)
        w2 = jnp.exp(l2 - m)
        num = (w0 * og_ref[0, pl.ds(r0, tile), :] + w1 * og_ref[1, pl.ds(r0, tile), :]
               + w2 * og_ref[2, pl.ds(r0, tile), :])
        o = num / (w0 + w1 + w2)
        gate = z_ref[pl.ds(r0, tile), :].astype(F32)
        o_ref[pl.ds(r0, tile), :] = (o * gate).astype(o_ref.dtype)
        return carry

    lax.fori_loop(0, seq // tile, merge, 0)


def _attn_core(p0, p1, p2):
    bsz, seq, _ = p0.shape

    def cols(which):
        return pl.BlockSpec((None, seq, B_DH), lambda b, h: (b, 0, which * B_HEADS + h))

    qkv = [cols(0), cols(1), cols(2)]
    return pl.pallas_call(
        _attn_kernel,
        grid=(bsz, B_HEADS),
        in_specs=[*qkv, *qkv, *qkv, cols(3)],
        out_specs=pl.BlockSpec((None, seq, B_DH), lambda b, h: (b, 0, h)),
        out_shape=jax.ShapeDtypeStruct((bsz, seq, B_W), BF16),
        scratch_shapes=[
            pltpu.VMEM((B_GROUPS, seq, B_DH), F32),
            pltpu.VMEM((B_GROUPS, seq, B_DH), F32),
        ],
        compiler_params=pltpu.CompilerParams(
            dimension_semantics=("parallel", "parallel"), vmem_limit_bytes=VMEM_LIMIT_BYTES),
        name="dilated_attn",
    )(p0, p0, p0, p1, p1, p1, p2, p2, p2, p0)


def _rotary_lane_order(a):
    gap = ROPE_SHIFT - ROPE_HALF
    return jnp.concatenate(
        [a[..., :ROPE_HALF], a[..., ROPE_DIMS:ROPE_DIMS + gap], a[..., ROPE_HALF:ROPE_DIMS],
         a[..., ROPE_DIMS + gap:]], axis=-1)


def _attn_weights_kernel(w_ref, o_ref):
    w = w_ref[...].astype(BF16)

    @pl.when(pl.program_id(0) < 2)
    def _():
        src = lax.broadcasted_iota(jnp.int32, (B_DH, B_DH), 0)
        dst = lax.broadcasted_iota(jnp.int32, (B_DH, B_DH), 1)
        gap = ROPE_SHIFT - ROPE_HALF
        wanted = jnp.where(dst < ROPE_HALF, dst,
                           jnp.where(dst < ROPE_HALF + gap, dst + (ROPE_DIMS - ROPE_HALF),
                                     jnp.where(dst < ROPE_DIMS + gap, dst - gap, dst)))
        perm = jnp.where(src == wanted, 1.0, 0.0).astype(BF16)
        for j in range(B_HEADS):
            cols = slice(j * B_DH, (j + 1) * B_DH)
            o_ref[:, cols] = jnp.dot(w[:, cols], perm, preferred_element_type=F32).astype(BF16)

    @pl.when(pl.program_id(0) >= 2)
    def _():
        o_ref[...] = w


def _attn_group_weights(wb, gi, with_gate):
    d = wb.shape[0]
    n_blocks = 4 if with_gate else 3
    gate_block = 3 * B_GROUPS

    def col_block(which):
        return (0, jnp.where(which < 3, which * B_GROUPS + gi, gate_block))

    return pl.pallas_call(
        _attn_weights_kernel,
        grid=(n_blocks,),
        in_specs=[pl.BlockSpec((d, B_W), col_block)],
        out_specs=pl.BlockSpec((d, B_W), lambda which: (0, which)),
        out_shape=jax.ShapeDtypeStruct((d, n_blocks * B_W), BF16),
        name=f"attn_weights_g{gi}",
    )(wb)


def kernel(x, positions, norm_g, a_w_in, a_conv_w, a_log, a_dt_bias, a_norm_g, a_w_out,
           b_w_in, b_q_norm_g, b_k_norm_g, b_w_out):
    bsz, seq, d = x.shape
    t = bsz * seq
    x2 = x.reshape(t, d)

    w_main = a_w_in[0, :, :A_MAIN].astype(BF16)
    w_gate_t = a_w_in[0, :, A_MAIN:].T.astype(BF16)
    zeros8 = jnp.zeros((A_HEADS,), F32)
    alog16 = jnp.concatenate([zeros8, a_log[0].astype(F32)])[:, None]
    dtb16 = jnp.concatenate([zeros8, a_dt_bias[0].astype(F32)])[:, None]
    proj, gates_t = _gdn_proj(x2, norm_g[0][None, :], w_main, w_gate_t, alog16, dtb16,
                              a_conv_w[0], seq)
    x3 = _gdn_core(proj.reshape(bsz, seq, A_MAIN), gates_t, a_norm_g[0][None, :],
                   a_w_out[0].astype(BF16), x)
    x2 = x3.reshape(t, d)

    w0 = _attn_group_weights(b_w_in[0], 0, with_gate=True)
    w1 = _attn_group_weights(b_w_in[0], 1, with_gate=False)
    w2 = _attn_group_weights(b_w_in[0], 2, with_gate=False)
    q_gain = _rotary_lane_order(b_q_norm_g[0]) * (B_DH ** -0.5)
    k_gain = _rotary_lane_order(b_k_norm_g[0])
    cos, sin = _rope_tables(positions)
    p0, p1, p2 = _attn_proj(x2, norm_g[1][None, :], w0, w1, w2, cos, sin, q_gain, k_gain, bsz, seq)
    o = _attn_core(p0.reshape(bsz, seq, -1), p1.reshape(bsz, seq, -1), p2.reshape(bsz, seq, -1))
    x2 = _out_proj(o.reshape(t, B_W), b_w_out[0].astype(BF16), x2, "attn_out_proj")
    return x2.reshape(bsz, seq, d)
```

```python
import functools

import jax
import jax.numpy as jnp
from jax import lax
from jax.experimental import pallas as pl
from jax.experimental.pallas import tpu as pltpu

F32 = jnp.float32
BF16 = jnp.bfloat16
EPS = 1e-6
LOG2_E = 1.4426950408889634

D_MODEL = 1024
A_HEADS = 8
A_DK = 128
A_DV = 256
A_QK = A_HEADS * A_DK
A_VW = A_HEADS * A_DV
A_CONVW = 2 * A_QK + A_VW
A_MAIN = A_CONVW + A_VW
A_CONV = 4
CHUNK = 128
GROUP = 16
OUT_BATCH = 4
NEUMANN_BLOCK = 8
B_WINDOWS = (128, 512, 2048)
B_DILATIONS = (1, 4, 16)
B_GROUPS = 3
B_HEADS = 8
B_DH = 128
B_W = B_HEADS * B_DH
B_BLOCK = 128
ROPE_THETA = 500000.0
ROPE_DIMS = B_DH // 4
ROPE_HALF = ROPE_DIMS // 2
ROPE_SHIFT = B_DH // 2
ATTN_BATCH = 4
ATTN_LOOKAHEAD = 2

VMEM_LIMIT_BYTES = 56 * 1024 * 1024
OUT_ROW_TILE = 1024
GDN_ROW_TILE = 256
ATTN_ROW_TILE = 256
COL_TILE = 512
HALO = 16


def _split3(a):
    hi = a.astype(BF16)
    r = a - hi.astype(F32)
    mid = r.astype(BF16)
    lo = (r - mid.astype(F32)).astype(BF16)
    return hi, mid, lo


def _bdot(a, b):
    return jnp.einsum("cik,ckj->cij", a, b, preferred_element_type=F32)


def _normed_f32(x_ref, g_ref):
    x = x_ref[...]
    ms = jnp.mean(x * x, axis=-1, keepdims=True)
    return x * lax.rsqrt(ms + EPS) * g_ref[...]


def _resident(shape):
    return pl.BlockSpec(shape, lambda i: (0,) * len(shape), pipeline_mode=pl.Buffered(1))


def _gdn_proj_kernel(x_ref, xh_ref, g_ref, w_ref, wgt_ref, alog_ref, dtb_ref, cw_ref,
                     o_ref, gates_ref, *, tiles_per_seq):
    tm = x_ref.shape[0]
    first = (pl.program_id(0) % tiles_per_seq) == 0
    hn = _normed_f32(x_ref, g_ref).astype(BF16)
    halo = jnp.where(first, 0.0, _normed_f32(xh_ref, g_ref)).astype(BF16)
    hcat = jnp.concatenate([halo, hn], axis=0)

    for n0 in range(0, A_CONVW, COL_TILE):
        y = jnp.dot(hcat, w_ref[:, n0:n0 + COL_TILE], preferred_element_type=F32)
        cw = cw_ref[:, n0:n0 + COL_TILE]
        acc = y * cw[A_CONV - 1:A_CONV, :]
        for tap in range(A_CONV - 1):
            acc = acc + pltpu.roll(y, A_CONV - 1 - tap, 0) * cw[tap:tap + 1, :]
        o_ref[:, n0:n0 + COL_TILE] = acc[HALO:, :].astype(o_ref.dtype)

    for n0 in range(A_CONVW, A_MAIN, COL_TILE):
        z = jnp.dot(hn, w_ref[:, n0:n0 + COL_TILE], preferred_element_type=F32)
        o_ref[:, n0:n0 + COL_TILE] = (z * jax.nn.sigmoid(z)).astype(o_ref.dtype)

    logits = lax.dot_general(wgt_ref[...], hn, (((1,), (1,)), ((), ())), preferred_element_type=F32)
    row = lax.broadcasted_iota(jnp.int32, logits.shape, 0)
    xs = logits + dtb_ref[...]
    softplus = jnp.maximum(xs, 0.0) + jnp.log1p(jnp.exp(-jnp.abs(xs)))
    decay = -jnp.exp(alog_ref[...]) * softplus
    gates_ref[...] = jnp.where(row < A_HEADS, jax.nn.sigmoid(logits), decay)


def _gdn_proj(x2, g, w, wgt, alog16, dtb16, conv_w, seq):
    t, d = x2.shape
    tm = GDN_ROW_TILE
    kern = functools.partial(_gdn_proj_kernel, tiles_per_seq=seq // tm)
    return pl.pallas_call(
        kern,
        grid=(t // tm,),
        in_specs=[
            pl.BlockSpec((tm, d), lambda i: (i, 0)),
            pl.BlockSpec((HALO, d), lambda i: (jnp.maximum(i * (tm // HALO) - 1, 0), 0)),
            _resident((1, d)),
            _resident((d, A_MAIN)),
            _resident((2 * A_HEADS, d)),
            _resident((2 * A_HEADS, 1)),
            _resident((2 * A_HEADS, 1)),
            _resident((A_CONV, A_CONVW)),
        ],
        out_specs=[
            pl.BlockSpec((tm, A_MAIN), lambda i: (i, 0)),
            pl.BlockSpec((2 * A_HEADS, tm), lambda i: (0, i)),
        ],
        out_shape=[
            jax.ShapeDtypeStruct((t, A_MAIN), BF16),
            jax.ShapeDtypeStruct((2 * A_HEADS, t), F32),
        ],
        compiler_params=pltpu.CompilerParams(
            dimension_semantics=("parallel",), vmem_limit_bytes=VMEM_LIMIT_BYTES),
        name="gdn_norm_proj",
    )(x2, x2, g, w, wgt, alog16, dtb16, conv_w)


def _out_proj_kernel(o_ref, w_ref, x_ref, y_ref):
    y_ref[...] = x_ref[...] + jnp.dot(o_ref[...], w_ref[...], preferred_element_type=F32)


def _out_proj(o2, w, x2, name):
    t, k = o2.shape
    d = w.shape[1]
    tm = OUT_ROW_TILE
    return pl.pallas_call(
        _out_proj_kernel,
        grid=(t // tm,),
        in_specs=[
            pl.BlockSpec((tm, k), lambda i: (i, 0)),
            _resident((k, d)),
            pl.BlockSpec((tm, d), lambda i: (i, 0)),
        ],
        out_specs=pl.BlockSpec((tm, d), lambda i: (i, 0)),
        out_shape=jax.ShapeDtypeStruct((t, d), F32),
        compiler_params=pltpu.CompilerParams(
            dimension_semantics=("parallel",), vmem_limit_bytes=VMEM_LIMIT_BYTES),
        name=name,
    )(o2, w, x2)


def _gdn_kernel(q_ref, k_ref, v_ref, z_ref, gt_ref, ng_ref, wo_ref, x_ref, y_ref, s_ref, rf_ref):
    head = pl.program_id(1)
    rows = GROUP * CHUNK
    n_groups = q_ref.shape[0] // rows
    c = CHUNK

    ii = lax.broadcasted_iota(jnp.int32, (c, c), 0)
    jj = lax.broadcasted_iota(jnp.int32, (c, c), 1)
    eye = jnp.where(ii == jj, 1.0, 0.0).astype(F32)
    lower_incl = ii >= jj
    lower_strict = ii > jj
    cum_rhs = jnp.where(ii <= jj, 1.0, 0.0).astype(BF16)

    def same_block(size):
        return (ii // size) == (jj // size)

    def col(x):
        return jnp.swapaxes(jnp.broadcast_to(x[:, None, :], (GROUP, c, c)), 1, 2)

    s_ref[...] = jnp.zeros_like(s_ref)

    n_chunks = q_ref.shape[0] // c

    def chunk_rows(r):
        full = gt_ref[pl.ds(r, 1), :]
        return jnp.concatenate([full[:, n * c:(n + 1) * c] for n in range(n_chunks)], axis=0)

    beta_all = chunk_rows(head)
    gh, gm, gl = _split3(chunk_rows(head + A_HEADS))
    gc_all = (jnp.dot(gh, cum_rhs, preferred_element_type=F32)
              + jnp.dot(gm, cum_rhs, preferred_element_type=F32)
              + jnp.dot(gl, cum_rhs, preferred_element_type=F32))
    exp_all = jnp.exp(gc_all)
    last_all = jnp.broadcast_to(gc_all[:, c - 1:c], gc_all.shape)
    rowforms = (beta_all, gc_all, exp_all, beta_all * exp_all, jnp.exp(last_all - gc_all),
                jnp.exp(last_all))
    for idx, val in enumerate(rowforms):
        for n in range(n_groups):
            rf_ref[idx, n] = val[n * GROUP:(n + 1) * GROUP, :]

    def chunk_local(g):
        r0 = g * rows
        def silu_rows(ref):
            a = ref[pl.ds(r0, rows), :].astype(F32)
            return a * jax.nn.sigmoid(a)

        def l2n(a):
            return a * lax.rsqrt(jnp.sum(a * a, axis=-1, keepdims=True) + EPS)

        q3 = (l2n(silu_rows(q_ref)) * (A_DK ** -0.5)).reshape(GROUP, c, A_DK)
        k3 = l2n(silu_rows(k_ref)).reshape(GROUP, c, A_DK)
        v3 = silu_rows(v_ref).reshape(GROUP, c, A_DV)
        q3b = q3.astype(BF16)
        k3b = k3.astype(BF16)

        gc = rf_ref[1, g]
        beta_c = col(rf_ref[0, g])
        gc_c = col(gc)
        exp_gc_c = col(rf_ref[2, g])
        beta_exp_c = col(rf_ref[3, g])
        tail_c = col(rf_ref[4, g])
        e_last = rf_ref[5, g]
        e_last = jnp.concatenate([e_last, e_last], axis=-1)
        decay = jnp.exp(jnp.where(lower_incl[None], gc_c - gc[:, None, :], -jnp.inf))

        kb3 = (k3 * beta_c).astype(BF16)
        kk = jnp.einsum("cid,cjd->cij", kb3, k3b, preferred_element_type=F32)
        qk = jnp.einsum("cid,cjd->cij", q3b, k3b, preferred_element_type=F32) * decay
        low = jnp.where(lower_strict[None], kk * decay, 0.0)

        assert NEUMANN_BLOCK == 8
        m1 = jnp.where(same_block(NEUMANN_BLOCK)[None], -low, 0.0)
        m1b = m1.astype(BF16)
        tinv = eye[None] + m1
        m2b = _bdot(m1b, m1b).astype(BF16)
        tinv = tinv + _bdot(tinv.astype(BF16), m2b)
        m4b = _bdot(m2b, m2b).astype(BF16)
        tinv = tinv + _bdot(tinv.astype(BF16), m4b)
        size = NEUMANN_BLOCK
        while size < c:
            off = jnp.where((same_block(2 * size) & jnp.logical_not(same_block(size)))[None], low, 0.0)
            tb = tinv.astype(BF16)
            tinv = tinv - _bdot(_bdot(tb, off.astype(BF16)).astype(BF16), tb)
            size *= 2

        tb = tinv.astype(BF16)
        vb = (v3 * jnp.concatenate([beta_c, beta_c], axis=-1)).astype(BF16)
        kbg = (k3 * beta_exp_c).astype(BF16)
        u = _bdot(tb, vb)
        w = _bdot(tb, kbg)

        qt = q3 * exp_gc_c
        kd = (k3 * tail_c).astype(BF16)
        ub = u.astype(BF16)
        wb = w.astype(BF16)
        qkb = qk.astype(BF16)
        kdt = jnp.swapaxes(kd, 1, 2)
        a_neg = -_bdot(kdt, wb)
        b_add = _bdot(kdt, ub)
        q_eff = qt - _bdot(qkb, wb)
        o_loc = _bdot(qkb, ub)
        p_all = jnp.concatenate([a_neg, q_eff], axis=1).astype(BF16)
        return p_all, b_add, o_loc, e_last

    def project_out(row, gated):
        lhs = jnp.concatenate(gated, axis=0)
        y_ref[pl.ds(row, lhs.shape[0]), :] += jnp.dot(lhs, wo_ref[...], preferred_element_type=F32)

    def state_steps(g, p_all, b_add, o_loc, e_last):
        ng = ng_ref[...]
        ready = None
        gated = []
        for cc in range(GROUP):
            state = s_ref[...]
            r = jnp.dot(p_all[cc], state.astype(BF16), preferred_element_type=F32)
            o = r[A_DK:, :] + o_loc[cc]
            s_ref[...] = state * e_last[cc:cc + 1, :] + r[:A_DK, :] + b_add[cc]
            if ready is not None:
                project_out(*ready)
                ready = None
            on = o * lax.rsqrt(jnp.mean(o * o, axis=-1, keepdims=True) + EPS) * ng
            row = g * rows + cc * c
            gate = z_ref[pl.ds(row, c), :].astype(F32)
            gated.append((on * gate).astype(BF16))
            if len(gated) == OUT_BATCH:
                ready = (row - (OUT_BATCH - 1) * c, gated)
                gated = []
        project_out(*ready)

    @pl.when(head == 0)
    def _():
        y_ref[...] = x_ref[...]

    local = chunk_local(0)
    for g in range(n_groups):
        nxt = chunk_local(g + 1) if g + 1 < n_groups else None
        state_steps(g, *local)
        local = nxt


def _gdn_core(proj3, gates_t, norm_g, w_out, x3):
    bsz, seq, _ = proj3.shape
    d = x3.shape[-1]
    kb = A_QK // A_DK
    vb = 2 * A_QK // A_DV
    zb = vb + A_VW // A_DV
    return pl.pallas_call(
        _gdn_kernel,
        grid=(bsz, A_HEADS),
        in_specs=[
            pl.BlockSpec((None, seq, A_DK), lambda b, h: (b, 0, h)),
            pl.BlockSpec((None, seq, A_DK), lambda b, h: (b, 0, kb + h)),
            pl.BlockSpec((None, seq, A_DV), lambda b, h: (b, 0, vb + h)),
            pl.BlockSpec((None, seq, A_DV), lambda b, h: (b, 0, zb + h)),
            pl.BlockSpec((2 * A_HEADS, seq), lambda b, h: (0, b)),
            pl.BlockSpec((1, A_DV), lambda b, h: (0, 0)),
            pl.BlockSpec((A_DV, d), lambda b, h: (h, 0)),
            pl.BlockSpec((None, seq, d), lambda b, h: (b, 0, 0)),
        ],
        out_specs=pl.BlockSpec((None, seq, d), lambda b, h: (b, 0, 0)),
        out_shape=jax.ShapeDtypeStruct((bsz, seq, d), F32),
        scratch_shapes=[
            pltpu.VMEM((A_DK, A_DV), F32),
            pltpu.VMEM((6, seq // (GROUP * CHUNK), GROUP, CHUNK), F32),
        ],
        compiler_params=pltpu.CompilerParams(
            dimension_semantics=("parallel", "arbitrary"), vmem_limit_bytes=VMEM_LIMIT_BYTES),
        name="gdn_core",
    )(proj3, proj3, proj3, proj3, gates_t, norm_g, w_out, x3)


def _rope_table_kernel(pos_ref, invf_ref, cos_ref, sin_ref):
    tr, per_row = pos_ref.shape
    nf = ROPE_HALF

    def dot_split(a, b, terms):
        parts = _split3(a)[:terms]
        out = jnp.dot(parts[0], b, preferred_element_type=F32)
        for part in parts[1:]:
            out = out + jnp.dot(part, b, preferred_element_type=F32)
        return out

    rk = lax.broadcasted_iota(jnp.int32, (per_row, B_DH), 0)
    rl = lax.broadcasted_iota(jnp.int32, (per_row, B_DH), 1)
    rep = jnp.where(rl // nf == rk, 1.0, 0.0).astype(BF16)
    ang = dot_split(pos_ref[...].astype(F32), rep, 3) * invf_ref[...]
    c = jnp.cos(ang)
    s = jnp.sin(ang)

    ek = lax.broadcasted_iota(jnp.int32, (B_DH, B_DH), 0)
    el = lax.broadcasted_iota(jnp.int32, (B_DH, B_DH), 1)
    lane = lax.broadcasted_iota(jnp.int32, (1, B_DH), 1)
    rotary = (lane < nf) | ((lane >= ROPE_SHIFT) & (lane < ROPE_SHIFT + nf))
    base = jnp.where(rotary, 0.0, 1.0)
    for p in range(per_row):
        mine = ek // nf == p
        first = mine & (el == ek % nf)
        second = mine & (el == ek % nf + ROPE_SHIFT)
        e_cos = jnp.where(first | second, 1.0, 0.0).astype(BF16)
        e_sin = (jnp.where(second, 1.0, 0.0) - jnp.where(first, 1.0, 0.0)).astype(BF16)
        rows = pl.ds(p, tr, stride=per_row)
        cos_ref[rows, :] = dot_split(c, e_cos, 2) + base
        sin_ref[rows, :] = dot_split(s, e_sin, 2)


def _rope_tables(positions):
    t = positions.size
    per_row = B_DH // ROPE_HALF
    inv_freq = ROPE_THETA ** (-jnp.arange(0, ROPE_DIMS, 2, dtype=F32) / ROPE_DIMS)
    invf = jnp.tile(inv_freq, per_row)[None, :]
    rows = t // per_row
    tr = min(rows, 512)
    return pl.pallas_call(
        _rope_table_kernel,
        grid=(rows // tr,),
        in_specs=[pl.BlockSpec((tr, per_row), lambda i: (i, 0)), pl.BlockSpec((1, B_DH), lambda i: (0, 0))],
        out_specs=[pl.BlockSpec((tr * per_row, B_DH), lambda i: (i, 0))] * 2,
        out_shape=[jax.ShapeDtypeStruct((t, B_DH), F32)] * 2,
        name="rope_tables",
    )(positions.reshape(rows, per_row), invf)


def _attn_proj_kernel(x_ref, g_ref, w0_ref, w1_ref, w2_ref, cos_ref, sin_ref, qg_ref, kg_ref,
                      p0_ref, p1_ref, p2_ref, hn_ref):
    tm = x_ref.shape[0]
    n_lane_blocks = hn_ref.shape[0]
    hn_nat = _normed_f32(x_ref, g_ref)
    for j in range(n_lane_blocks):
        hn_ref[j] = hn_nat[:, j * B_DH:(j + 1) * B_DH]

    def by_stream(ref, dil):
        if dil == 1:
            return ref[...]
        return jnp.concatenate([ref[pl.ds(r, tm // dil, stride=dil), :] for r in range(dil)], axis=0)

    def hn_by_stream(dil):
        if dil == 1:
            return hn_nat
        return jnp.concatenate([by_stream(hn_ref.at[j], dil) for j in range(n_lane_blocks)], axis=-1)

    def norm_rope(y, gain, cos, sin):
        y = y * lax.rsqrt(jnp.mean(y * y, axis=-1, keepdims=True) + EPS) * gain
        return y * cos + pltpu.roll(y, ROPE_SHIFT, 1) * sin

    w_refs = (w0_ref, w1_ref, w2_ref)
    for gi in range(B_GROUPS):
        dil = B_DILATIONS[gi]
        hn = hn_by_stream(dil).astype(BF16)
        cos = by_stream(cos_ref, dil)
        sin = by_stream(sin_ref, dil)
        w_ref = w_refs[gi]
        for n0 in range(0, w_ref.shape[1], COL_TILE):
            y = jnp.dot(hn, w_ref[:, n0:n0 + COL_TILE], preferred_element_type=F32)
            if n0 < 2 * B_W:
                gain = (qg_ref if n0 < B_W else kg_ref)[gi:gi + 1, :]
                y = jnp.concatenate(
                    [norm_rope(y[:, j * B_DH:(j + 1) * B_DH], gain, cos, sin)
                     for j in range(COL_TILE // B_DH)], axis=-1)
            elif n0 >= 3 * B_W:
                y = y * jax.nn.sigmoid(y)
            y = y.astype(BF16)
            if gi == 0:
                p0_ref[:, n0:n0 + COL_TILE] = y
            else:
                out_ref = p1_ref if gi == 1 else p2_ref
                out_ref[:, :, n0:n0 + COL_TILE] = y.reshape(dil, tm // dil, COL_TILE)


def _attn_proj(x2, g, w0, w1, w2, cos, sin, q_gain, k_gain, bsz, seq):
    t, d = x2.shape
    tm = ATTN_ROW_TILE
    per_seq = seq // tm
    d1, d2 = B_DILATIONS[1], B_DILATIONS[2]
    return pl.pallas_call(
        _attn_proj_kernel,
        grid=(t // tm,),
        in_specs=[
            pl.BlockSpec((tm, d), lambda i: (i, 0)),
            _resident((1, d)),
            _resident(w0.shape),
            _resident(w1.shape),
            _resident(w2.shape),
            pl.BlockSpec((tm, B_DH), lambda i: (i, 0)),
            pl.BlockSpec((tm, B_DH), lambda i: (i, 0)),
            _resident((B_GROUPS, B_DH)),
            _resident((B_GROUPS, B_DH)),
        ],
        out_specs=[
            pl.BlockSpec((tm, w0.shape[1]), lambda i: (i, 0)),
            pl.BlockSpec((None, d1, tm // d1, w1.shape[1]), lambda i: (i // per_seq, 0, i % per_seq, 0)),
            pl.BlockSpec((None, d2, tm // d2, w2.shape[1]), lambda i: (i // per_seq, 0, i % per_seq, 0)),
        ],
        out_shape=[
            jax.ShapeDtypeStruct((t, w0.shape[1]), BF16),
            jax.ShapeDtypeStruct((bsz, d1, seq // d1, w1.shape[1]), BF16),
            jax.ShapeDtypeStruct((bsz, d2, seq // d2, w2.shape[1]), BF16),
        ],
        scratch_shapes=[pltpu.VMEM((d // B_DH, tm, B_DH), F32)],
        compiler_params=pltpu.CompilerParams(
            dimension_semantics=("parallel",), vmem_limit_bytes=VMEM_LIMIT_BYTES),
        name="attn_norm_proj",
    )(x2, g, w0, w1, w2, cos, sin, q_gain, k_gain)


def _attn_kernel(q0_ref, k0_ref, v0_ref, q1_ref, k1_ref, v1_ref, q2_ref, k2_ref, v2_ref, z_ref,
                 o_ref, og_ref, lg_ref):
    seq = q0_ref.shape[0]
    blk = B_BLOCK
    nb = ATTN_BATCH
    tile = nb * blk
    assert all(w // d == blk for w, d in zip(B_WINDOWS, B_DILATIONS))

    qi = lax.broadcasted_iota(jnp.int32, (nb, blk, 2 * blk), 1)
    kj = lax.broadcasted_iota(jnp.int32, (nb, blk, 2 * blk), 2)
    bi = lax.broadcasted_iota(jnp.int32, (nb, blk, 2 * blk), 0)
    band = ((kj < blk) & (kj >= qi)) | ((kj >= blk) & (kj - blk <= qi))
    band_first = band & ((kj >= blk) | (bi > 0))
    cur_only = (lax.broadcasted_iota(jnp.int32, (nb, blk, blk), 2)
                <= lax.broadcasted_iota(jnp.int32, (nb, blk, blk), 1))
    ones_v = jnp.ones((nb, blk, B_DH), BF16)
    zero_blk = jnp.zeros((1, blk, B_DH), BF16)

    q_refs = (q0_ref, q1_ref, q2_ref)
    k_refs = (k0_ref, k1_ref, k2_ref)
    v_refs = (v0_ref, v1_ref, v2_ref)

    def blocks(ref, first, count):
        return ref[pl.ds(first, count * blk), :].reshape(count, blk, B_DH)

    def scores(gi, first, starts_stream, has_prev):
        qb = blocks(q_refs[gi], first, nb)
        kc = blocks(k_refs[gi], first, nb)
        if not has_prev:
            s = jnp.einsum("bqd,bkd->bqk", qb, kc, preferred_element_type=F32)
            return jnp.where(cur_only, s, -jnp.inf)
        if starts_stream:
            kp = jnp.concatenate([zero_blk, kc[:nb - 1]], axis=0)
        else:
            kp = blocks(k_refs[gi], first - blk, nb)
        s = jnp.einsum("bqd,bkd->bqk", qb, jnp.concatenate([kp, kc], axis=1),
                       preferred_element_type=F32)
        return jnp.where(band_first if starts_stream else band, s, -jnp.inf)

    def finish(gi, first, starts_stream, has_prev, out_rows, s):
        m = jnp.max(s, axis=-1, keepdims=True)
        p = jnp.exp2(s - m).astype(BF16)
        vc = jnp.concatenate([blocks(v_refs[gi], first, nb), ones_v], axis=-1)
        if has_prev:
            if starts_stream:
                vp = jnp.concatenate([jnp.zeros((1, blk, 2 * B_DH), BF16), vc[:nb - 1]], axis=0)
            else:
                vp = jnp.concatenate([blocks(v_refs[gi], first - blk, nb), ones_v], axis=-1)
            vc = jnp.concatenate([vp, vc], axis=1)
        acc = jnp.einsum("bqk,bkd->bqd", p, vc, preferred_element_type=F32)
        den = acc[..., B_DH:]
        o = acc[..., :B_DH] / den
        lse = m + jnp.log2(den)
        b0 = 0
        for rows, count in out_rows:
            og_ref[gi, rows, :] = o[b0:b0 + count].reshape(count * blk, B_DH)
            lg_ref[gi, rows, :] = lse[b0:b0 + count].reshape(count * blk, B_DH)
            b0 += count

    batches = []
    for gi in range(B_GROUPS):
        dil = B_DILATIONS[gi]
        length = seq // dil
        per_stream = length // blk
        for first in range(0, seq, tile):
            r, m0 = divmod(first, length)
            if per_stream >= nb:
                t0 = r + m0 * dil
                rows = pl.ds(t0, tile) if dil == 1 else pl.ds(t0, tile, stride=dil)
                batches.append((gi, first, m0 == 0, True, [(rows, nb)]))
            else:
                assert per_stream == 1
                rows = [(pl.ds(r + j, blk, stride=dil), 1) for j in range(nb)]
                batches.append((gi, first, True, False, rows))

    pending = {}
    for i in range(len(batches) + ATTN_LOOKAHEAD):
        if i < len(batches):
            pending[i] = scores(*batches[i][:4])
        if i >= ATTN_LOOKAHEAD:
            j = i - ATTN_LOOKAHEAD
            finish(*batches[j], pending.pop(j))

    def merge(t, carry):
        r0 = pl.multiple_of(t * tile, tile)
        l0 = lg_ref[0, pl.ds(r0, tile), :]
        l1 = lg_ref[1, pl.ds(r0, tile), :]
        l2 = lg_ref[2, pl.ds(r0, tile), :]
        m = jnp.maximum(jnp.maximum(l0, l1), l2)
        w0 = jnp.exp2(l0 - m)
        w1 = jnp.exp2(l1 - m)
        w2 = jnp.exp2(l2 - m)
        num = (w0 * og_ref[0, pl.ds(r0, tile), :] + w1 * og_ref[1, pl.ds(r0, tile), :]
               + w2 * og_ref[2, pl.ds(r0, tile), :])
        o = num / (w0 + w1 + w2)
        gate = z_ref[pl.ds(r0, tile), :].astype(F32)
        o_ref[pl.ds(r0, tile), :] = (o * gate).astype(o_ref.dtype)
        return carry

    lax.fori_loop(0, seq // tile, merge, 0)


def _attn_core(p0, p1, p2):
    bsz, seq, _ = p0.shape

    def cols(which):
        return pl.BlockSpec((None, seq, B_DH), lambda b, h: (b, 0, which * B_HEADS + h))

    qkv = [cols(0), cols(1), cols(2)]
    return pl.pallas_call(
        _attn_kernel,
        grid=(bsz, B_HEADS),
        in_specs=[*qkv, *qkv, *qkv, cols(3)],
        out_specs=pl.BlockSpec((None, seq, B_DH), lambda b, h: (b, 0, h)),
        out_shape=jax.ShapeDtypeStruct((bsz, seq, B_W), BF16),
        scratch_shapes=[
            pltpu.VMEM((B_GROUPS, seq, B_DH), F32),
            pltpu.VMEM((B_GROUPS, seq, B_DH), F32),
        ],
        compiler_params=pltpu.CompilerParams(
            dimension_semantics=("parallel", "parallel"), vmem_limit_bytes=VMEM_LIMIT_BYTES),
        name="dilated_attn",
    )(p0, p0, p0, p1, p1, p1, p2, p2, p2, p0)


def _rotary_lane_order(a):
    gap = ROPE_SHIFT - ROPE_HALF
    return jnp.concatenate(
        [a[..., :ROPE_HALF], a[..., ROPE_DIMS:ROPE_DIMS + gap], a[..., ROPE_HALF:ROPE_DIMS],
         a[..., ROPE_DIMS + gap:]], axis=-1)


def _attn_weights_kernel(w_ref, o_ref):
    w = w_ref[...].astype(BF16)

    @pl.when(pl.program_id(0) < 2)
    def _():
        src = lax.broadcasted_iota(jnp.int32, (B_DH, B_DH), 0)
        dst = lax.broadcasted_iota(jnp.int32, (B_DH, B_DH), 1)
        gap = ROPE_SHIFT - ROPE_HALF
        wanted = jnp.where(dst < ROPE_HALF, dst,
                           jnp.where(dst < ROPE_HALF + gap, dst + (ROPE_DIMS - ROPE_HALF),
                                     jnp.where(dst < ROPE_DIMS + gap, dst - gap, dst)))
        perm = jnp.where(src == wanted, 1.0, 0.0).astype(BF16)
        for j in range(B_HEADS):
            cols = slice(j * B_DH, (j + 1) * B_DH)
            o_ref[:, cols] = jnp.dot(w[:, cols], perm, preferred_element_type=F32).astype(BF16)

    @pl.when(pl.program_id(0) >= 2)
    def _():
        o_ref[...] = w


def _attn_group_weights(wb, gi, with_gate):
    d = wb.shape[0]
    n_blocks = 4 if with_gate else 3
    gate_block = 3 * B_GROUPS

    def col_block(which):
        return (0, jnp.where(which < 3, which * B_GROUPS + gi, gate_block))

    return pl.pallas_call(
        _attn_weights_kernel,
        grid=(n_blocks,),
        in_specs=[pl.BlockSpec((d, B_W), col_block)],
        out_specs=pl.BlockSpec((d, B_W), lambda which: (0, which)),
        out_shape=jax.ShapeDtypeStruct((d, n_blocks * B_W), BF16),
        name=f"attn_weights_g{gi}",
    )(wb)


def kernel(x, positions, norm_g, a_w_in, a_conv_w, a_log, a_dt_bias, a_norm_g, a_w_out,
           b_w_in, b_q_norm_g, b_k_norm_g, b_w_out):
    bsz, seq, d = x.shape
    t = bsz * seq
    x2 = x.reshape(t, d)

    w_main = a_w_in[0, :, :A_MAIN].astype(BF16)
    w_gate_t = a_w_in[0, :, A_MAIN:].T.astype(BF16)
    zeros8 = jnp.zeros((A_HEADS,), F32)
    alog16 = jnp.concatenate([zeros8, a_log[0].astype(F32)])[:, None]
    dtb16 = jnp.concatenate([zeros8, a_dt_bias[0].astype(F32)])[:, None]
    proj, gates_t = _gdn_proj(x2, norm_g[0][None, :], w_main, w_gate_t, alog16, dtb16,
                              a_conv_w[0], seq)
    x3 = _gdn_core(proj.reshape(bsz, seq, A_MAIN), gates_t, a_norm_g[0][None, :],
                   a_w_out[0].astype(BF16), x)
    x2 = x3.reshape(t, d)

    w0 = _attn_group_weights(b_w_in[0], 0, with_gate=True)
    w1 = _attn_group_weights(b_w_in[0], 1, with_gate=False)
    w2 = _attn_group_weights(b_w_in[0], 2, with_gate=False)
    q_gain = _rotary_lane_order(b_q_norm_g[0]) * (B_DH ** -0.5 * LOG2_E)
    k_gain = _rotary_lane_order(b_k_norm_g[0])
    cos, sin = _rope_tables(positions)
    p0, p1, p2 = _attn_proj(x2, norm_g[1][None, :], w0, w1, w2, cos, sin, q_gain, k_gain, bsz, seq)
    o = _attn_core(p0.reshape(bsz, seq, -1), p1.reshape(bsz, seq, -1), p2.reshape(bsz, seq, -1))
    x2 = _out_proj(o.reshape(t, B_W), b_w_out[0].astype(BF16), x2, "attn_out_proj")
    return x2.reshape(bsz, seq, d)
```
